```python
import math
import jax
import jax.numpy as jnp
from jax import lax
import numpy as np

D_MODEL = 2048
BATCH = 2
SEQ = 4096
DEPTH = 1
DEC_BATCH = 32
DEC_SEQ = 1
PAST_LEN = 8192
PAGE_SIZE = 128

HEAD_DIM = 64
N_HEADS = (D_MODEL // 2) // HEAD_DIM
N_KV = N_HEADS // 4
Q_PER_KV = N_HEADS // N_KV
ATTN_W = N_HEADS * HEAD_DIM
KV_W = 2 * N_KV * HEAD_DIM
CMP_BLOCK = 32
CMP_STRIDE = 16
CMP_R = CMP_BLOCK // CMP_STRIDE
CMP_HIDDEN = 2 * HEAD_DIM
SEL_BLOCK = 64
SEL_TOPN = 16
WINDOW = 512
SEL_Q_BLOCK = 64
WIN_Q_BLOCK = 128
SELECT_FORCE = 1.0e4
ATTN_SCALE = HEAD_DIM ** -0.5
D_INNER = D_MODEL // 2
SSM_HEAD_DIM = 64
N_SSM_HEADS = D_INNER // SSM_HEAD_DIM
N_SSM_GROUPS = 2
SSM_HEADS_PER_GROUP = N_SSM_HEADS // N_SSM_GROUPS
D_STATE = 128
CONV_W = 4
CONV_DIM = D_INNER + 2 * N_SSM_GROUPS * D_STATE
SSD_CHUNK = 256
MIX_W = ATTN_W + D_INNER
D_FF = 4 * D_MODEL
IN_SPLITS = (ATTN_W, KV_W, KV_W, KV_W, 3 * N_HEADS, D_INNER, CONV_DIM, N_SSM_HEADS)
IN_COLS = sum(IN_SPLITS)
NORM_EPS = 1e-5

kernel_name = 'nsa_ssd_parallel_hybrid_step'


def rms_norm(x, w):
    xf = x.astype(jnp.float32)
    y = xf * lax.rsqrt(jnp.mean(xf * xf, axis=-1, keepdims=True) + NORM_EPS)
    return (y * w.astype(jnp.float32)).astype(x.dtype)


def masked_softmax(s, mask):
    s = jnp.where(mask, s.astype(jnp.float32), -jnp.inf)
    m = jnp.max(s, axis=-1, keepdims=True)
    m = jnp.where(jnp.isfinite(m), m, 0.0)
    e = jnp.where(mask, jnp.exp(s - m), 0.0)
    return e / jnp.maximum(jnp.sum(e, axis=-1, keepdims=True), 1e-30)


def split_projection(h, w_in):
    B, T, _ = h.shape
    proj = h @ w_in
    parts, off = [], 0
    for width in IN_SPLITS:
        parts.append(proj[..., off:off + width])
        off += width
    q, kc, ks, kw, g, z, xbc, dt_raw = parts
    q = q.reshape(B, T, N_KV, Q_PER_KV, HEAD_DIM)
    kv_shape = (B, T, 2, N_KV, HEAD_DIM)
    gates = jax.nn.sigmoid(g.astype(jnp.float32)).reshape(B, T, N_KV, Q_PER_KV, 3)
    return (q, kc.reshape(kv_shape), ks.reshape(kv_shape), kw.reshape(kv_shape),
            gates, z, xbc, dt_raw)


def compress_kv(kv, pe, w1, b1, w2, b2):
    B, T = kv.shape[:2]
    n_seg = T // CMP_STRIDE
    n_cmp = n_seg - CMP_R + 1
    seg = kv[:, :n_seg * CMP_STRIDE].reshape(B, n_seg, CMP_STRIDE, 2, N_KV, HEAD_DIM)
    w1r = w1.reshape(2, CMP_R, CMP_STRIDE, HEAD_DIM, CMP_HIDDEN)
    proj = jnp.einsum('bsjcgd,crjdf->bsrcgf', seg, w1r)
    hid = sum(proj[:, r:r + n_cmp, r] for r in range(CMP_R))
    pe_term = jnp.einsum('cld,cldf->cf', pe, w1)
    hid = jax.nn.silu(hid + (pe_term + b1)[None, None, :, None, :])
    return jnp.einsum('bncgf,cfd->bncgd', hid, w2) + b2[None, None, :, None, :]


def compressed_attention(q, kc, qpos):
    n_cmp = kc.shape[1]
    s = jnp.einsum('btgrd,bngd->bgrtn', q, kc[:, :, 0]) * ATTN_SCALE
    blk_last = jnp.arange(n_cmp) * CMP_STRIDE + (CMP_BLOCK - 1)
    p = masked_softmax(s, blk_last[None, :] <= qpos[:, None])
    o = jnp.einsum('bgrtn,bngd->btgrd', p, kc[:, :, 1])
    return o.astype(q.dtype), p


def cmp_to_sel_matrix(n_cmp, n_slc):
    ratio = SEL_BLOCK // CMP_STRIDE
    diff = jnp.arange(n_cmp)[:, None] - ratio * jnp.arange(n_slc)[None, :]
    mat = jnp.zeros((n_cmp, n_slc), jnp.float32)
    for n in range(CMP_R):
        mat = mat + ((diff + n >= 0) & (diff + n < ratio)).astype(jnp.float32)
    return mat


def select_blocks(p_cmp, qpos, n_slc):
    imp = jnp.einsum('bgrtn,nj->btgj', p_cmp, cmp_to_sel_matrix(p_cmp.shape[-1], n_slc))
    jt = (qpos // SEL_BLOCK)[:, None]
    j = jnp.arange(n_slc)[None, :]
    forced = (j == 0) | (j == jt) | (j == jt - 1)
    imp = jnp.where(forced[None, :, None, :], SELECT_FORCE, imp)
    imp = jnp.where((j > jt)[None, :, None, :], -SELECT_FORCE, imp)
    _, idx = lax.top_k(imp, min(SEL_TOPN, n_slc))
    return idx


def selected_attention(q, idx, kv, qpos):
    B, T = kv.shape[:2]
    Tq, K = q.shape[1], idx.shape[-1]
    kpos = idx[..., None] * SEL_BLOCK + jnp.arange(SEL_BLOCK)
    bi = jnp.arange(B)[:, None, None, None, None]
    gi = jnp.arange(N_KV)[None, None, :, None, None]
    rows = kv[bi, jnp.minimum(kpos, T - 1), :, gi]
    rows = rows.reshape(B, Tq, N_KV, K * SEL_BLOCK, 2, HEAD_DIM)
    mask = (kpos <= qpos[None, :, None, None, None]).reshape(B, Tq, N_KV, 1, K * SEL_BLOCK)
    s = jnp.einsum('btgrd,btgnd->btgrn', q, rows[..., 0, :]) * ATTN_SCALE
    p = masked_softmax(s, mask)
    return jnp.einsum('btgrn,btgnd->btgrd', p, rows[..., 1, :]).astype(q.dtype)


def window_attention_prompt(q, kv):
    B, T = kv.shape[:2]
    nb, nk = T // WIN_Q_BLOCK, WINDOW // WIN_Q_BLOCK
    kvp = jnp.pad(kv, ((0, 0), (WINDOW, 0), (0, 0), (0, 0), (0, 0)))
    kvp = kvp.reshape(B, nb + nk, WIN_Q_BLOCK, 2, N_KV, HEAD_DIM)
    band = jnp.concatenate([kvp[:, j:j + nb] for j in range(nk + 1)], axis=2)
    qb = q.reshape(B, nb, WIN_Q_BLOCK, N_KV, Q_PER_KV, HEAD_DIM)
    s = jnp.einsum('bnqgrd,bnkgd->bngrqk', qb, band[:, :, :, 0]) * ATTN_SCALE
    start = jnp.arange(nb) * WIN_Q_BLOCK
    qpos = start[:, None] + jnp.arange(WIN_Q_BLOCK)
    kpos = (start - WINDOW)[:, None] + jnp.arange((nk + 1) * WIN_Q_BLOCK)
    d = qpos[:, :, None] - kpos[:, None, :]
    mask = (d >= 0) & (d < WINDOW) & (kpos[:, None, :] >= 0)
    p = masked_softmax(s, mask[None, :, None, None])
    o = jnp.einsum('bngrqk,bnkgd->bnqgrd', p, band[:, :, :, 1])
    return o.reshape(B, T, N_KV, Q_PER_KV, HEAD_DIM).astype(q.dtype)


def window_attention_rows(q, kv, qpos, kpos):
    s = jnp.einsum('btgrd,bkgd->bgrtk', q, kv[:, :, 0]) * ATTN_SCALE
    d = qpos[:, None] - kpos[None, :]
    p = masked_softmax(s, (d >= 0) & (d < WINDOW))
    return jnp.einsum('bgrtk,bkgd->btgrd', p, kv[:, :, 1]).astype(q.dtype)


def gate_branches(gates, o_cmp, o_sel, o_win):
    return gates[..., 0, None] * o_cmp + gates[..., 1, None] * o_sel + gates[..., 2, None] * o_win


def ssd_chunked(x, dt, A, Bm, Cm, s0):
    f32 = jnp.float32
    Bsz, L = x.shape[:2]
    Q = min(SSD_CHUNK, L)
    nc = -(-L // Q)
    pad = nc * Q - L

    def pad_len(a):
        return jnp.pad(a.astype(f32), ((0, 0), (0, pad)) + ((0, 0),) * (a.ndim - 2))

    x, dt, Bm, Cm = pad_len(x), pad_len(dt), pad_len(Bm), pad_len(Cm)
    Gs, Hg = N_SSM_GROUPS, SSM_HEADS_PER_GROUP
    xdt = (x * dt[..., None]).reshape(Bsz, nc, Q, Gs, Hg, SSM_HEAD_DIM)
    a = (dt * A.astype(f32)).reshape(Bsz, nc, Q, Gs, Hg)
    Bc = Bm.reshape(Bsz, nc, Q, Gs, D_STATE)
    Cc = Cm.reshape(Bsz, nc, Q, Gs, D_STATE)
    a_cs = jnp.cumsum(a, axis=2)
    causal = jnp.tril(jnp.ones((Q, Q), bool))[None, None, :, :, None, None]
    seg = a_cs[:, :, :, None] - a_cs[:, :, None, :]
    decay = jnp.exp(jnp.where(causal, seg, -jnp.inf))
    cb = jnp.einsum('bclgn,bcsgn->bclsg', Cc, Bc)
    y_diag = jnp.einsum('bclsg,bclsgh,bcsghp->bclghp', cb, decay, xdt)
    to_end = jnp.exp(a_cs[:, :, -1:] - a_cs)
    chunk_states = jnp.einsum('bcsgn,bcsgh,bcsghp->bcghpn', Bc, to_end, xdt)
    chunk_decay = jnp.exp(a_cs[:, :, -1])

    def carry_step(s, inp):
        st, dc = inp
        return s * dc[..., None, None] + st, s

    s_init = s0.astype(f32).reshape(Bsz, Gs, Hg, SSM_HEAD_DIM, D_STATE)
    s_fin, s_in = lax.scan(carry_step, s_init,
                           (jnp.moveaxis(chunk_states, 1, 0), jnp.moveaxis(chunk_decay, 1, 0)))
    s_in = jnp.moveaxis(s_in, 0, 1)
    y_off = jnp.einsum('bclgn,bcghpn,bclgh->bclghp', Cc, s_in, jnp.exp(a_cs))
    y = (y_diag + y_off).reshape(Bsz, nc * Q, N_SSM_HEADS, SSM_HEAD_DIM)[:, :L]
    return y, s_fin.reshape(Bsz, N_SSM_HEADS, SSM_HEAD_DIM, D_STATE)


def mamba2_mixer(z, xbc, dt_raw, conv_prefix, s0, prm):
    f32 = jnp.float32
    B, L, _ = xbc.shape
    full = jnp.concatenate([conv_prefix.astype(xbc.dtype), xbc], axis=1)
    conv = lax.conv_general_dilated(full, prm['conv_w'][:, None, :].astype(full.dtype),
                                    window_strides=(1,), padding='VALID',
                                    dimension_numbers=('NWC', 'WIO', 'NWC'),
                                    feature_group_count=CONV_DIM)
    act = jax.nn.silu(conv + prm['conv_b'])
    nbc = N_SSM_GROUPS * D_STATE
    xs = act[..., :D_INNER].reshape(B, L, N_SSM_HEADS, SSM_HEAD_DIM)
    Bm = act[..., D_INNER:D_INNER + nbc].reshape(B, L, N_SSM_GROUPS, D_STATE)
    Cm = act[..., D_INNER + nbc:].reshape(B, L, N_SSM_GROUPS, D_STATE)
    dt = jax.nn.softplus(dt_raw.astype(f32) + prm['dt_bias'].astype(f32))
    A = -jnp.exp(prm['a_log'].astype(f32))
    y, s_fin = ssd_chunked(xs, dt, A, Bm, Cm, s0)
    y = y + prm['d_skip'].astype(f32)[:, None] * xs.astype(f32)
    y = y.reshape(B, L, D_INNER) * jax.nn.silu(z.astype(f32))
    yg = y.reshape(B, L, N_SSM_GROUPS, D_INNER // N_SSM_GROUPS)
    yg = yg * lax.rsqrt(jnp.mean(yg * yg, axis=-1, keepdims=True) + NORM_EPS)
    y = yg.reshape(B, L, D_INNER) * prm['ssm_norm'].astype(f32)
    return y.astype(z.dtype), full[:, -(CONV_W - 1):], s_fin


def finish_layer(x, attn, ssd_y, prm):
    B, T, _ = x.shape
    mix = jnp.concatenate([attn.reshape(B, T, ATTN_W).astype(x.dtype), ssd_y], axis=-1) @ prm['w_out']
    x = x + mix
    h = rms_norm(x, prm['ln_mlp'])
    return x + jnp.square(jax.nn.relu(h @ prm['w_up'])) @ prm['w_down']


def cmp_params(prm):
    return prm['cmp_pe'], prm['cmp_w1'], prm['cmp_b1'], prm['cmp_w2'], prm['cmp_b2']


def prompt_layer(x, prm):
    B, T, _ = x.shape
    h = rms_norm(x, prm['ln_mix'])
    q, kv_c, kv_s, kv_w, gates, z, xbc, dt_raw = split_projection(h, prm['w_in'])
    qpos = jnp.arange(T)
    o_cmp, p_cmp = compressed_attention(q, compress_kv(kv_c, *cmp_params(prm)), qpos)
    idx = select_blocks(p_cmp, qpos, -(-T // SEL_BLOCK))
    nb = T // SEL_Q_BLOCK

    def qblocks(a):
        return jnp.moveaxis(a.reshape((B, nb, SEL_Q_BLOCK) + a.shape[2:]), 1, 0)

    o_sel = lax.map(lambda a: selected_attention(a[0], a[1], kv_s, a[2]),
                    (qblocks(q), qblocks(idx), qpos.reshape(nb, SEL_Q_BLOCK)))
    o_sel = jnp.moveaxis(o_sel, 0, 1).reshape(q.shape)
    o_win = window_attention_prompt(q, kv_w)
    attn = gate_branches(gates, o_cmp, o_sel, o_win)
    ssd_y, conv_new, ssm_new = mamba2_mixer(
        z, xbc, dt_raw, jnp.zeros((B, CONV_W - 1, CONV_DIM), xbc.dtype),
        jnp.zeros((B, N_SSM_HEADS, SSM_HEAD_DIM, D_STATE), jnp.float32), prm)
    y = finish_layer(x, attn, ssd_y, prm)
    return y, (kv_c, kv_s, kv_w[:, -min(WINDOW, T):], conv_new, ssm_new.astype(x.dtype))


def sample_layer(x, cache_c, cache_s, win_buf, conv_st, ssm_st, page_table, prm):
    B, T, _ = x.shape
    past = page_table.shape[1] * cache_c.shape[1]
    h = rms_norm(x, prm['ln_mix'])
    q, kv_c, kv_s, kv_w, gates, z, xbc, dt_raw = split_projection(h, prm['w_in'])
    qpos = past + jnp.arange(T)

    def paged_rows(cache):
        return cache[page_table].reshape((B, past) + cache.shape[2:]).astype(x.dtype)

    full_c = jnp.concatenate([paged_rows(cache_c), kv_c], axis=1)
    full_s = jnp.concatenate([paged_rows(cache_s), kv_s], axis=1)
    o_cmp, p_cmp = compressed_attention(q, compress_kv(full_c, *cmp_params(prm)), qpos)
    idx = select_blocks(p_cmp, qpos, -(-(past + T) // SEL_BLOCK))
    o_sel = selected_attention(q, idx, full_s, qpos)
    n_buf = win_buf.shape[1]
    win_rows = jnp.concatenate([win_buf.astype(x.dtype), kv_w], axis=1)
    kpos = past - n_buf + jnp.arange(n_buf + T)
    o_win = window_attention_rows(q, win_rows, qpos, kpos)
    attn = gate_branches(gates, o_cmp, o_sel, o_win)
    ssd_y, conv_new, ssm_new = mamba2_mixer(z, xbc, dt_raw, conv_st, ssm_st, prm)
    y = finish_layer(x, attn, ssd_y, prm)
    return y, (kv_c, kv_s, win_rows[:, -min(WINDOW, past + T):], conv_new,
               ssm_new.astype(ssm_st.dtype))


def setup_inputs(seed: int = 0) -> dict:
    key = jax.random.key(seed)
    ks = jax.random.split(key, 32)
    f32 = jnp.float32

    def nrm(k, shape, scale):
        return scale * jax.random.normal(k, shape, f32)

    n_pages = PAST_LEN // PAGE_SIZE
    n_pool = (DEC_BATCH * n_pages * 5) // 4
    w_buf = min(WINDOW, PAST_LEN)
    page_table = jax.random.permutation(ks[0], n_pool)[:DEC_BATCH * n_pages]
    page_table = page_table.reshape(DEC_BATCH, n_pages).astype(jnp.int32)
    dt0 = jnp.exp(jax.random.uniform(ks[1], (DEPTH, N_SSM_HEADS), f32,
                                     math.log(1e-3), math.log(1e-1)))
    return {
        'x_prompt': nrm(ks[2], (BATCH, SEQ, D_MODEL), 1.0),
        'x_sample': nrm(ks[3], (DEC_BATCH, DEC_SEQ, D_MODEL), 1.0),
        'cache_kv_cmp': nrm(ks[4], (DEPTH, n_pool, PAGE_SIZE, 2, N_KV, HEAD_DIM), 1.0),
        'cache_kv_sel': nrm(ks[5], (DEPTH, n_pool, PAGE_SIZE, 2, N_KV, HEAD_DIM), 1.0),
        'state_kv_win': nrm(ks[6], (DEPTH, DEC_BATCH, w_buf, 2, N_KV, HEAD_DIM), 1.0),
        'state_conv': nrm(ks[7], (DEPTH, DEC_BATCH, CONV_W - 1, CONV_DIM), 1.0),
        'state_ssm': nrm(ks[8], (DEPTH, DEC_BATCH, N_SSM_HEADS, SSM_HEAD_DIM, D_STATE), 0.1),
        'page_table': page_table,
        'ln_mix': 1.0 + nrm(ks[9], (DEPTH, D_MODEL), 0.02),
        'w_in': nrm(ks[10], (DEPTH, D_MODEL, IN_COLS), D_MODEL ** -0.5),
        'cmp_pe': nrm(ks[11], (DEPTH, 2, CMP_BLOCK, HEAD_DIM), 0.1),
        'cmp_w1': nrm(ks[12], (DEPTH, 2, CMP_BLOCK, HEAD_DIM, CMP_HIDDEN), (CMP_BLOCK * HEAD_DIM) ** -0.5),
        'cmp_b1': nrm(ks[13], (DEPTH, 2, CMP_HIDDEN), 0.02),
        'cmp_w2': nrm(ks[14], (DEPTH, 2, CMP_HIDDEN, HEAD_DIM), CMP_HIDDEN ** -0.5),
        'cmp_b2': nrm(ks[15], (DEPTH, 2, HEAD_DIM), 0.02),
        'conv_w': nrm(ks[16], (DEPTH, CONV_W, CONV_DIM), CONV_W ** -0.5),
        'conv_b': nrm(ks[17], (DEPTH, CONV_DIM), 0.02),
        'dt_bias': dt0 + jnp.log(-jnp.expm1(-dt0)),
        'a_log': jnp.log(jax.random.uniform(ks[18], (DEPTH, N_SSM_HEADS), f32, 1.0, 16.0)),
        'd_skip': 1.0 + nrm(ks[19], (DEPTH, N_SSM_HEADS), 0.1),
        'ssm_norm': 1.0 + nrm(ks[20], (DEPTH, D_INNER), 0.02),
        'w_out': nrm(ks[21], (DEPTH, MIX_W, D_MODEL), MIX_W ** -0.5),
        'ln_mlp': 1.0 + nrm(ks[22], (DEPTH, D_MODEL), 0.02),
        'w_up': nrm(ks[23], (DEPTH, D_MODEL, D_FF), D_MODEL ** -0.5),
        'w_down': nrm(ks[24], (DEPTH, D_FF, D_MODEL), D_FF ** -0.5),
        'ln_final': 1.0 + nrm(ks[25], (D_MODEL,), 0.02),
    }


def reference(x_prompt, x_sample, cache_kv_cmp, cache_kv_sel, state_kv_win, state_conv,
              state_ssm, page_table, ln_mix, w_in, cmp_pe, cmp_w1, cmp_b1, cmp_w2, cmp_b2,
              conv_w, conv_b, dt_bias, a_log, d_skip, ssm_norm, w_out, ln_mlp, w_up, w_down,
              ln_final):
    hp, hs = x_prompt, x_sample
    new_p, new_s = [], []
    for l in range(DEPTH):
        prm = {'ln_mix': ln_mix[l], 'w_in': w_in[l], 'cmp_pe': cmp_pe[l], 'cmp_w1': cmp_w1[l],
               'cmp_b1': cmp_b1[l], 'cmp_w2': cmp_w2[l], 'cmp_b2': cmp_b2[l],
               'conv_w': conv_w[l], 'conv_b': conv_b[l], 'dt_bias': dt_bias[l],
               'a_log': a_log[l], 'd_skip': d_skip[l], 'ssm_norm': ssm_norm[l],
               'w_out': w_out[l], 'ln_mlp': ln_mlp[l], 'w_up': w_up[l], 'w_down': w_down[l]}
        hp, st_p = prompt_layer(hp, prm)
        hs, st_s = sample_layer(hs, cache_kv_cmp[l], cache_kv_sel[l], state_kv_win[l],
                                state_conv[l], state_ssm[l], page_table, prm)
        new_p.append(st_p)
        new_s.append(st_s)
    y_prompt = rms_norm(hp, ln_final)
    y_sample = rms_norm(hs, ln_final)
    kv_cmp_p = jnp.stack([s[0] for s in new_p])
    kv_sel_p = jnp.stack([s[1] for s in new_p])
    kv_win_p = jnp.stack([s[2] for s in new_p])
    conv_p = jnp.stack([s[3] for s in new_p])
    ssm_p = jnp.stack([s[4] for s in new_p])
    kv_cmp_s = jnp.stack([s[0] for s in new_s])
    kv_sel_s = jnp.stack([s[1] for s in new_s])
    kv_win_s = jnp.stack([s[2] for s in new_s])
    conv_s = jnp.stack([s[3] for s in new_s])
    ssm_s = jnp.stack([s[4] for s in new_s])
    return (y_prompt, y_sample, kv_cmp_p, kv_sel_p, kv_win_p, conv_p, ssm_p,
            kv_cmp_s, kv_sel_s, kv_win_s, conv_s, ssm_s)
```

```python
import functools
import math

import jax
import jax.numpy as jnp
import numpy as np
from jax import lax
from jax.experimental import pallas as pl
from jax.experimental.pallas import tpu as pltpu

f32 = jnp.float32
bf16 = jnp.bfloat16

D_MODEL = 2048
BATCH = 2
SEQ = 4096
DEC_BATCH = 32
PAST_LEN = 8192
PAGE_SIZE = 128
HEAD_DIM = 64
N_HEADS = 16
N_KV = 4
Q_PER_KV = 4
ATTN_W = 1024
KV_W = 512
CMP_BLOCK = 32
CMP_STRIDE = 16
CMP_HIDDEN = 128
SEL_BLOCK = 64
SEL_TOPN = 16
WINDOW = 512
SELECT_FORCE = 1.0e4
ATTN_SCALE = HEAD_DIM ** -0.5
D_INNER = 1024
SSM_HEAD_DIM = 64
N_SSM_HEADS = 16
N_SSM_GROUPS = 2
D_STATE = 128
CONV_W = 4
CONV_DIM = D_INNER + 2 * N_SSM_GROUPS * D_STATE
D_FF = 4 * D_MODEL
IN_SPLITS = (ATTN_W, KV_W, KV_W, KV_W, 3 * N_HEADS, D_INNER, CONV_DIM, N_SSM_HEADS)
NORM_EPS = 1e-5

LANES = 128
GD_W = LANES
DT_LANE0 = 3 * N_HEADS
N_PAGES = PAST_LEN // PAGE_SIZE
SEG_PER_PAGE = PAGE_SIZE // CMP_STRIDE
SEG_W = CMP_STRIDE * KV_W
NEG = -1.0e30
SEL_NEG = -1.0e9
VMEM_LIMIT = 56 * 1024 * 1024


def _cparams(sem):
    return pltpu.CompilerParams(dimension_semantics=sem, vmem_limit_bytes=VMEM_LIMIT)


def _nt(a, b):
    return lax.dot_general(a, b, (((1,), (1,)), ((), ())), preferred_element_type=f32)


def _split2(x):
    hi = x.astype(bf16)
    lo = (x - hi.astype(f32)).astype(bf16)
    return hi, lo


def _split3(x):
    hi = x.astype(bf16)
    r = x - hi.astype(f32)
    mid = r.astype(bf16)
    lo = (r - mid.astype(f32)).astype(bf16)
    return hi, mid, lo


def _dot_exact(x, w):
    return sum(jnp.dot(t, w, preferred_element_type=f32) for t in _split3(x))


def _silu(x):
    return x * (1.0 / (1.0 + jnp.exp(-x)))


def _sigmoid(x):
    return 1.0 / (1.0 + jnp.exp(-x))


def _softplus(x):
    return jnp.maximum(x, 0.0) + jnp.log1p(jnp.exp(-jnp.abs(x)))


IN_OUT_WIDTHS = (ATTN_W, D_INNER, CONV_DIM, KV_W, KV_W, KV_W, GD_W)


def _inproj_kernel(x_ref, lnw_ref, w_ref, *out_refs):
    x = x_ref[...]
    h = x * lax.rsqrt(jnp.mean(x * x, axis=-1, keepdims=True) + NORM_EPS) * lnw_ref[...]
    h = h.astype(bf16)
    off = 0
    for ref in out_refs:
        n = ref.shape[-1]
        ref[...] = jnp.dot(h, w_ref[:, off:off + n], preferred_element_type=f32)
        off += n


def _inproj(x, lnw, w_perm, tm):
    m = x.shape[0]
    n_tot = sum(IN_OUT_WIDTHS)
    return pl.pallas_call(
        _inproj_kernel,
        grid=(m // tm,),
        in_specs=[pl.BlockSpec((tm, D_MODEL), lambda i: (i, 0)),
                  pl.BlockSpec((1, D_MODEL), lambda i: (0, 0)),
                  pl.BlockSpec((D_MODEL, n_tot), lambda i: (0, 0), pipeline_mode=pl.Buffered(1))],
        out_specs=[pl.BlockSpec((tm, n), lambda i: (i, 0)) for n in IN_OUT_WIDTHS],
        out_shape=[jax.ShapeDtypeStruct((m, n), f32) for n in IN_OUT_WIDTHS],
        compiler_params=_cparams(("parallel",)),
        name="inproj",
    )(x, lnw, w_perm)


def _prep_w_in(w_in):
    parts, off = [], 0
    for width in IN_SPLITS:
        parts.append(w_in[:, off:off + width])
        off += width
    q, kc, ks, kw, g, z, xbc, dt = parts
    gd = jnp.concatenate([g, dt, jnp.zeros((D_MODEL, GD_W - 4 * N_HEADS), w_in.dtype)], axis=1)
    return jnp.concatenate([q, z, xbc, kc, ks, kw, gd], axis=1).astype(bf16)


def _outproj_kernel(x_ref, oc_ref, os_ref, ow_ref, gd_ref, y_ref, eg_ref, w_ref, o_ref):
    gates = _sigmoid(gd_ref[...])
    hi, lo = _split2(gates)
    attn = None
    for c, br in enumerate((oc_ref, os_ref, ow_ref)):
        ge = (jnp.dot(hi, eg_ref[c], preferred_element_type=f32)
              + jnp.dot(lo, eg_ref[c], preferred_element_type=f32))
        term = ge * br[...]
        attn = term if attn is None else attn + term
    mix = (jnp.dot(attn.astype(bf16), w_ref[:ATTN_W, :], preferred_element_type=f32)
           + jnp.dot(y_ref[...], w_ref[ATTN_W:, :], preferred_element_type=f32))
    o_ref[...] = x_ref[...] + mix


def _outproj(x, ocmp, osel, owin, gd, ssd_y, eg, w_out, tm):
    m = x.shape[0]
    row = lambda w: pl.BlockSpec((tm, w), lambda i: (i, 0))
    return pl.pallas_call(
        _outproj_kernel,
        grid=(m // tm,),
        in_specs=[row(D_MODEL), row(ATTN_W), row(ATTN_W), row(ATTN_W), row(GD_W), row(D_INNER),
                  pl.BlockSpec((3, GD_W, ATTN_W), lambda i: (0, 0, 0)),
                  pl.BlockSpec((ATTN_W + D_INNER, D_MODEL), lambda i: (0, 0))],
        out_specs=row(D_MODEL),
        out_shape=jax.ShapeDtypeStruct((m, D_MODEL), f32),
        compiler_params=_cparams(("parallel",)),
        name="outproj",
    )(x, ocmp, osel, owin, gd, ssd_y, eg, w_out)


def _gate_expand():
    eg = np.zeros((3, GD_W, ATTN_W), np.float32)
    for h in range(N_HEADS):
        for c in range(3):
            eg[c, h * 3 + c, h * HEAD_DIM:(h + 1) * HEAD_DIM] = 1.0
    return jnp.asarray(eg, bf16)


def _mlp_kernel(x_ref, ln_ref, wu_ref, wd_ref, lnf_ref, o_ref, h_ref, acc_ref):
    k = pl.program_id(1)

    @pl.when(k == 0)
    def _():
        x = x_ref[...]
        h = x * lax.rsqrt(jnp.mean(x * x, axis=-1, keepdims=True) + NORM_EPS) * ln_ref[...]
        h_ref[...] = h.astype(bf16)
        acc_ref[...] = jnp.zeros_like(acc_ref)

    u = jnp.dot(h_ref[...], wu_ref[...], preferred_element_type=f32)
    u = jnp.square(jnp.maximum(u, 0.0)).astype(bf16)
    acc_ref[...] += jnp.dot(u, wd_ref[...], preferred_element_type=f32)

    @pl.when(k == pl.num_programs(1) - 1)
    def _():
        y = x_ref[...] + acc_ref[...]
        y = y * lax.rsqrt(jnp.mean(y * y, axis=-1, keepdims=True) + NORM_EPS) * lnf_ref[...]
        o_ref[...] = y


def _mlp(x, ln_mlp, w_up, w_down, ln_final, tm, tf):
    m = x.shape[0]
    return pl.pallas_call(
        _mlp_kernel,
        grid=(m // tm, D_FF // tf),
        in_specs=[pl.BlockSpec((tm, D_MODEL), lambda i, k: (i, 0)),
                  pl.BlockSpec((1, D_MODEL), lambda i, k: (0, 0)),
                  pl.BlockSpec((D_MODEL, tf), lambda i, k: (0, k)),
                  pl.BlockSpec((tf, D_MODEL), lambda i, k: (k, 0)),
                  pl.BlockSpec((1, D_MODEL), lambda i, k: (0, 0))],
        out_specs=pl.BlockSpec((tm, D_MODEL), lambda i, k: (i, 0)),
        out_shape=jax.ShapeDtypeStruct((m, D_MODEL), f32),
        scratch_shapes=[pltpu.VMEM((tm, D_MODEL), bf16), pltpu.VMEM((tm, D_MODEL), f32)],
        compiler_params=_cparams(("parallel", "arbitrary")),
        name="mlp",
    )(x, ln_mlp, w_up, w_down, ln_final)


N_SLAB_K = CMP_STRIDE * LANES


def _compress_weights(cmp_pe, cmp_w1, cmp_b1, cmp_w2, cmp_b2):
    eye2 = jnp.eye(2, dtype=f32)
    w1 = cmp_w1.reshape(2, 2, CMP_STRIDE, HEAD_DIM, CMP_HIDDEN)
    w1cat = jnp.einsum('crjdf,ab->cjadrbf', w1, eye2).reshape(2, N_SLAB_K, 4 * CMP_HIDDEN).astype(bf16)
    w1flat = cmp_w1.reshape(2, CMP_BLOCK * HEAD_DIM, CMP_HIDDEN).astype(bf16)
    pe8 = jnp.broadcast_to(cmp_pe.reshape(2, 1, CMP_BLOCK * HEAD_DIM), (2, 8, CMP_BLOCK * HEAD_DIM))
    b1 = cmp_b1.reshape(2, 1, CMP_HIDDEN)
    wk, wv = cmp_w2[0], cmp_w2[1]
    zk = jnp.zeros_like(wk)
    w2k = jnp.einsum('fd,ab,e->afbed', wk, eye2, jnp.ones((2,), f32)).reshape(2 * CMP_HIDDEN, 4 * HEAD_DIM)
    b2k = jnp.tile(cmp_b2[0], 4)[None]
    vl = jnp.einsum('fd,ab,e->afbed', wv, eye2, jnp.array([1.0, 0.0], f32)).reshape(2 * CMP_HIDDEN, 4 * HEAD_DIM)
    vh = jnp.einsum('fd,ab,e->afbed', wv, eye2, jnp.array([0.0, 1.0], f32)).reshape(2 * CMP_HIDDEN, 4 * HEAD_DIM)
    w2v = jnp.concatenate([vl, vh], axis=1)
    zb = jnp.zeros((HEAD_DIM,), f32)
    b2v = jnp.concatenate([cmp_b2[1], zb, cmp_b2[1], zb, zb, cmp_b2[1], zb, cmp_b2[1]])[None]
    del zk
    return pe8, w1flat, b1, w1cat, w2k.astype(bf16), b2k, w2v.astype(bf16), b2v


def _compress_slab(a, c, gp, prm_refs, out_refs):
    pe_ref, w1f_ref, b1_ref, w1_ref, w2k_ref, b2k_ref, w2v_ref, b2v_ref = prm_refs
    kd_ref, vl_ref, vh_ref = out_refs
    pet = jnp.dot(pe_ref[c].astype(bf16), w1f_ref[c], preferred_element_type=f32)[0:1] + b1_ref[c]
    bias = jnp.concatenate([pet, pet], axis=1)
    p = jnp.dot(a, w1_ref[c], preferred_element_type=f32)
    s = p.shape[0]
    hid = p[:, :2 * CMP_HIDDEN] + pltpu.roll(p[:, 2 * CMP_HIDDEN:], s - 1, axis=0)
    hid = _silu(hid + bias).astype(bf16)
    w = 4 * HEAD_DIM
    if c == 0:
        kd_ref[:, gp * w:(gp + 1) * w] = (jnp.dot(hid, w2k_ref[...], preferred_element_type=f32)
                                          + b2k_ref[...]).astype(bf16)
    else:
        o = jnp.dot(hid, w2v_ref[...], preferred_element_type=f32) + b2v_ref[...]
        vl_ref[:, gp * w:(gp + 1) * w] = o[:, :w].astype(bf16)
        vh_ref[:, gp * w:(gp + 1) * w] = o[:, w:].astype(bf16)


def _slab_lane0(j, c, gp):
    return j * KV_W + c * (KV_W // 2) + gp * LANES


def _compress_all_slabs(x_ref, prm_refs, out_refs):
    for c in range(2):
        for gp in range(2):
            a = jnp.concatenate(
                [x_ref[:, _slab_lane0(j, c, gp):_slab_lane0(j, c, gp) + LANES] for j in range(CMP_STRIDE)],
                axis=1).astype(bf16)
            _compress_slab(a, c, gp, prm_refs, out_refs)


def _compress_prompt_kernel(x_ref, *refs):
    _compress_all_slabs(x_ref, refs[:8], refs[8:11])


def _compress_sample_kernel(pt_ref, *refs):
    del pt_ref
    pages, prm_refs, out_refs, x_ref = refs[:N_PAGES], refs[N_PAGES:N_PAGES + 8], refs[N_PAGES + 8:N_PAGES + 11], refs[-1]
    rows = 2 * SEG_PER_PAGE
    for p in range(0, N_PAGES, 2):
        pair = jnp.concatenate([pages[p][...], pages[p + 1][...]], axis=0)
        x_ref[(p // 2) * rows:(p // 2 + 1) * rows, :] = pair.astype(bf16)
    _compress_all_slabs(x_ref, prm_refs, out_refs)


def _const_spec(a, n_grid, n_prefetch=0):
    zeros = (0,) * a.ndim
    if n_grid == 1:
        return pl.BlockSpec(a.shape, (lambda b, *_: zeros))
    return pl.BlockSpec(a.shape, (lambda b, c, *_: zeros))


def _compress_prompt(kc, cw):
    nseg = SEQ // CMP_STRIDE
    x = kc.reshape(BATCH, nseg, SEG_W)
    out = jax.ShapeDtypeStruct((BATCH, nseg, KV_W), bf16)
    return pl.pallas_call(
        _compress_prompt_kernel,
        grid=(BATCH,),
        in_specs=[pl.BlockSpec((None, nseg, SEG_W), lambda b: (b, 0, 0))] + [_const_spec(a, 1) for a in cw],
        out_specs=[pl.BlockSpec((None, nseg, KV_W), lambda b: (b, 0, 0))] * 3,
        out_shape=[out] * 3,
        compiler_params=_cparams(("parallel",)),
        name="compress_prompt",
    )(x, *cw)


def _compress_sample(cache, page_table, cw):
    nseg = PAST_LEN // CMP_STRIDE
    x = cache.reshape(-1, SEG_W)
    out = jax.ShapeDtypeStruct((DEC_BATCH, nseg, KV_W), bf16)
    page_specs = [pl.BlockSpec((SEG_PER_PAGE, SEG_W), functools.partial(lambda b, pt, p: (pt[b * N_PAGES + p], 0), p=p))
                  for p in range(N_PAGES)]
    grid_spec = pltpu.PrefetchScalarGridSpec(
        num_scalar_prefetch=1,
        grid=(DEC_BATCH,),
        in_specs=page_specs + [_const_spec(a, 1) for a in cw],
        out_specs=[pl.BlockSpec((None, nseg, KV_W), lambda b, pt: (b, 0, 0))] * 3,
        scratch_shapes=[pltpu.VMEM((nseg, SEG_W), bf16)])
    return pl.pallas_call(
        _compress_sample_kernel,
        grid_spec=grid_spec,
        out_shape=[out] * 3,
        compiler_params=_cparams(("arbitrary",)),
        name="compress_sample",
    )(page_table.reshape(-1), *([x] * N_PAGES), *cw)


TQ = 256
N_SEL = SEQ // SEL_BLOCK


def _half_mask(shape, hi):
    lane = lax.broadcasted_iota(jnp.int32, shape, 1)
    return (lane >= HEAD_DIM) if hi else (lane < HEAD_DIM)


def _cmp_attn_kernel(q_ref, kd_ref, vl_ref, vh_ref, mt_ref, o_ref, selm_ref):
    i = pl.program_id(2)
    t0 = i * TQ
    ncmp = kd_ref.shape[0]
    tpos = t0 + lax.broadcasted_iota(jnp.int32, (TQ, ncmp), 0)
    nidx = lax.broadcasted_iota(jnp.int32, (TQ, ncmp), 1)
    mask = nidx * CMP_STRIDE + (CMP_BLOCK - 1) <= tpos
    kd, vl, vh = kd_ref[...], vl_ref[...], vh_ref[...]
    psum = jnp.zeros((TQ, ncmp), f32)
    for pr in range(2):
        q2 = q_ref[:, pr * LANES:(pr + 1) * LANES] * ATTN_SCALE
        acc = jnp.zeros((TQ, LANES), f32)
        for e in range(2):
            qz = jnp.where(_half_mask((TQ, LANES), e), q2, 0.0).astype(bf16)
            s = jnp.where(mask, _nt(qz, kd), NEG)
            m = jnp.max(s, axis=1, keepdims=True)
            ex = jnp.where(mask, jnp.exp(s - m), 0.0)
            l = jnp.sum(ex, axis=1, keepdims=True)
            p = ex * (1.0 / jnp.maximum(l, 1e-30))
            psum = psum + p
            acc = acc + jnp.dot(p.astype(bf16), vh if e else vl, preferred_element_type=f32)
        o_ref[:, pr * LANES:(pr + 1) * LANES] = acc
    hi, lo = _split2(psum)
    imp = _nt(mt_ref[...], hi) + _nt(mt_ref[...], lo)
    imp = imp[:N_SEL]
    j = lax.broadcasted_iota(jnp.int32, (N_SEL, TQ), 0)
    jt = (t0 + lax.broadcasted_iota(jnp.int32, (N_SEL, TQ), 1)) // SEL_BLOCK
    imp = jnp.where((j == 0) | (j == jt) | (j == jt - 1), SELECT_FORCE, imp)
    imp = jnp.where(j > jt, -SELECT_FORCE, imp)
    cnt = jnp.zeros((N_SEL, TQ), f32)
    for jp in range(N_SEL):
        row = imp[jp:jp + 1, :]
        cnt = cnt + jnp.where(j > jp, jnp.where(row >= imp, 1.0, 0.0), jnp.where(row > imp, 1.0, 0.0))
    selm = jnp.where((cnt < SEL_TOPN) & (j <= jt), 0.0, SEL_NEG)
    selm_ref[...] = jnp.concatenate([selm, selm], axis=0).T


def _cmp_to_sel_matrix_t(n_cmp_pad, n_slc, n_cmp):
    ratio = SEL_BLOCK // CMP_STRIDE
    i = np.arange(n_cmp_pad)[None, :]
    jj = np.arange(n_slc)[:, None]
    diff = i - ratio * jj
    mat = np.zeros((n_slc, n_cmp_pad), np.float32)
    for n in range(CMP_BLOCK // CMP_STRIDE):
        mat += ((diff + n >= 0) & (diff + n < ratio)).astype(np.float32)
    mat[:, n_cmp:] = 0.0
    return mat


def _cmp_attn_prompt(q, kd, vl, vh):
    ncmp = SEQ // CMP_STRIDE
    mt = np.zeros((LANES, ncmp), np.float32)
    mt[:N_SEL] = _cmp_to_sel_matrix_t(ncmp, N_SEL, ncmp - 1)
    mt = jnp.asarray(mt, bf16)
    nq = SEQ // TQ
    kv_spec = pl.BlockSpec((None, ncmp, LANES), lambda b, g, i: (b, 0, g))
    return pl.pallas_call(
        _cmp_attn_kernel,
        grid=(BATCH, N_KV, nq),
        in_specs=[pl.BlockSpec((TQ, 2 * LANES), lambda b, g, i: (b * nq + i, g)), kv_spec, kv_spec, kv_spec,
                  pl.BlockSpec(mt.shape, lambda b, g, i: (0, 0))],
        out_specs=[pl.BlockSpec((TQ, 2 * LANES), lambda b, g, i: (b * nq + i, g)),
                   pl.BlockSpec((None, None, TQ, LANES), lambda b, g, i: (b, g, i, 0))],
        out_shape=[jax.ShapeDtypeStruct((BATCH * SEQ, ATTN_W), f32),
                   jax.ShapeDtypeStruct((BATCH, N_KV, SEQ, LANES), f32)],
        compiler_params=_cparams(("parallel", "parallel", "parallel")),
        name="cmp_attn_prompt",
    )(q, kd, vl, vh, mt)


def _flash_tile(qs, k, vl, vh, mask, m_ref, acc_ref):
    s = _nt(qs, k)
    if mask is not None:
        s = jnp.where(mask, s, NEG)
    m_prev = m_ref[...]
    m_next = jnp.maximum(m_prev, jnp.max(s, axis=1, keepdims=True))
    alpha = jnp.exp(m_prev - m_next)
    p = jnp.exp(s - jnp.concatenate([m_next] * (s.shape[1] // LANES), axis=1)).astype(bf16)
    m_ref[...] = m_next
    for r in range(Q_PER_KV):
        rows = slice(r * TQ, (r + 1) * TQ)
        acc_ref[rows, :] = alpha[rows] * acc_ref[rows, :] + jnp.dot(p[rows], vh if r % 2 else vl,
                                                                   preferred_element_type=f32)


def _flash_finish(acc_ref, o_ref):
    lo = _half_mask((TQ, LANES), 0)
    for pr in range(2):
        a_e = acc_ref[(2 * pr) * TQ:(2 * pr + 1) * TQ, :]
        a_o = acc_ref[(2 * pr + 1) * TQ:(2 * pr + 2) * TQ, :]
        num = jnp.where(lo, a_e, a_o)
        den = jnp.where(lo, pltpu.roll(a_e, HEAD_DIM, axis=1), pltpu.roll(a_o, HEAD_DIM, axis=1))
        o_ref[:, pr * LANES:(pr + 1) * LANES] = num * (1.0 / den)


def _stack_queries(q_ref, qs_ref):
    for r in range(Q_PER_KV):
        q2 = q_ref[:, (r // 2) * LANES:(r // 2 + 1) * LANES] * ATTN_SCALE
        qs_ref[r * TQ:(r + 1) * TQ, 0:LANES] = jnp.where(_half_mask((TQ, LANES), r % 2), q2, 0.0).astype(bf16)


def _tile_masks():
    row = lax.broadcasted_iota(jnp.int32, (Q_PER_KV * TQ, TQ), 0) & (TQ - 1)
    col = lax.broadcasted_iota(jnp.int32, (Q_PER_KV * TQ, TQ), 1)
    return row, col


def _sel_attn_kernel(q_ref, selm_ref, kp_ref, vl_ref, vh_ref, o_ref, qs_ref, m_ref, acc_ref):
    i = pl.program_id(2)
    _stack_queries(q_ref, qs_ref)
    selm = selm_ref[...].astype(bf16)
    for r in range(Q_PER_KV):
        qs_ref[r * TQ:(r + 1) * TQ, LANES:2 * LANES] = selm
    m_ref[...] = jnp.full(m_ref.shape, NEG, f32)
    acc_ref[...] = jnp.zeros(acc_ref.shape, f32)
    qs = qs_ref[...]
    row, col = _tile_masks()

    def tile(kt, mask):
        rows = pl.ds(pl.multiple_of(kt * TQ, TQ), TQ)
        _flash_tile(qs, kp_ref[rows, :], vl_ref[rows, :], vh_ref[rows, :], mask, m_ref, acc_ref)

    tile(i, col <= row)

    def body(kt, carry):
        tile(kt, None)
        return carry

    lax.fori_loop(0, i, body, 0)
    _flash_finish(acc_ref, o_ref)


def _win_attn_kernel(q_ref, kd_ref, vl_ref, vh_ref, o_ref, qs_ref, m_ref, acc_ref):
    i = pl.program_id(2)
    _stack_queries(q_ref, qs_ref)
    m_ref[...] = jnp.full(m_ref.shape, NEG, f32)
    acc_ref[...] = jnp.zeros(acc_ref.shape, f32)
    qs = qs_ref[...]
    row, col = _tile_masks()

    def tile(kt, mask):
        rows = pl.ds(pl.multiple_of(kt * TQ, TQ), TQ)
        _flash_tile(qs, kd_ref[rows, :], vl_ref[rows, :], vh_ref[rows, :], mask, m_ref, acc_ref)

    tile(i, col <= row)

    @pl.when(i >= 1)
    def _():
        tile(i - 1, None)

    @pl.when(i >= 2)
    def _():
        tile(i - 2, col > row)

    _flash_finish(acc_ref, o_ref)


def _split_kv(kv):
    kv5 = kv.reshape(BATCH, SEQ, 2, N_KV, HEAD_DIM)
    return kv5[:, :, 0].transpose(0, 2, 1, 3), kv5[:, :, 1].transpose(0, 2, 1, 3)


def _value_pairs(v):
    ones = jnp.ones_like(v)
    return (jnp.concatenate([v, ones], axis=-1).astype(bf16), jnp.concatenate([ones, v], axis=-1).astype(bf16))


def _sel_attn_prompt(q, selm, ks):
    assert WINDOW == 2 * TQ
    k, v = _split_kv(ks)
    blk = (jnp.arange(SEQ)[:, None] // SEL_BLOCK == jnp.arange(N_SEL)[None, :]).astype(f32)
    blk = jnp.broadcast_to(blk, (BATCH, N_KV, SEQ, N_SEL))
    kp = jnp.concatenate([k, k, blk, jnp.zeros_like(blk)], axis=-1).astype(bf16)
    vl, vh = _value_pairs(v)
    nq = SEQ // TQ
    q_spec = pl.BlockSpec((TQ, 2 * LANES), lambda b, g, i: (b * nq + i, g))
    kv_spec = lambda w: pl.BlockSpec((None, None, SEQ, w), lambda b, g, i: (b, g, 0, 0))
    return pl.pallas_call(
        _sel_attn_kernel,
        grid=(BATCH, N_KV, nq),
        in_specs=[q_spec, pl.BlockSpec((None, None, TQ, LANES), lambda b, g, i: (b, g, i, 0)),
                  kv_spec(2 * LANES), kv_spec(LANES), kv_spec(LANES)],
        out_specs=q_spec,
        out_shape=jax.ShapeDtypeStruct((BATCH * SEQ, ATTN_W), f32),
        scratch_shapes=[pltpu.VMEM((Q_PER_KV * TQ, 2 * LANES), bf16), pltpu.VMEM((Q_PER_KV * TQ, LANES), f32),
                        pltpu.VMEM((Q_PER_KV * TQ, LANES), f32)],
        compiler_params=_cparams(("parallel", "parallel", "parallel")),
        name="sel_attn_prompt",
    )(q, selm, kp, vl, vh)


def _win_attn_prompt(q, kw):
    k, v = _split_kv(kw)
    kd = jnp.concatenate([k, k], axis=-1).astype(bf16)
    vl, vh = _value_pairs(v)
    nq = SEQ // TQ
    q_spec = pl.BlockSpec((TQ, 2 * LANES), lambda b, g, i: (b * nq + i, g))
    kv_spec = pl.BlockSpec((None, None, SEQ, LANES), lambda b, g, i: (b, g, 0, 0))
    return pl.pallas_call(
        _win_attn_kernel,
        grid=(BATCH, N_KV, nq),
        in_specs=[q_spec, kv_spec, kv_spec, kv_spec],
        out_specs=q_spec,
        out_shape=jax.ShapeDtypeStruct((BATCH * SEQ, ATTN_W), f32),
        scratch_shapes=[pltpu.VMEM((Q_PER_KV * TQ, LANES), bf16), pltpu.VMEM((Q_PER_KV * TQ, LANES), f32),
                        pltpu.VMEM((Q_PER_KV * TQ, LANES), f32)],
        compiler_params=_cparams(("parallel", "parallel", "parallel")),
        name="win_attn_prompt",
    )(q, kd, vl, vh)


SSD_Q = 256
HALF_INNER = D_INNER // N_SSM_GROUPS
BC_W = N_SSM_GROUPS * D_STATE


def _mamba_consts(conv_w, conv_b, dt_bias, a_log, d_skip, ssm_norm):
    pad = lambda v: jnp.zeros((1, GD_W), f32).at[0, DT_LANE0:DT_LANE0 + N_SSM_HEADS].set(v)
    e16 = np.zeros((GD_W, D_INNER), np.float32)
    for h in range(N_SSM_HEADS):
        e16[DT_LANE0 + h, h * SSM_HEAD_DIM:(h + 1) * SSM_HEAD_DIM] = 1.0
    tri = np.tril(np.ones((SSD_Q, SSD_Q), np.float32))
    return (conv_w, conv_b[None], pad(dt_bias), pad(a_log), jnp.asarray(e16, bf16),
            jnp.repeat(d_skip, SSM_HEAD_DIM)[None], ssm_norm[None], jnp.asarray(tri, bf16))


def _dt_and_decay(gd, dtb_ref, alog_ref):
    lane = lax.broadcasted_iota(jnp.int32, gd.shape, 1)
    live = (lane >= DT_LANE0) & (lane < DT_LANE0 + N_SSM_HEADS)
    dt = jnp.where(live, _softplus(gd + dtb_ref[...]), 0.0)
    return dt, dt * (-jnp.exp(alog_ref[...]))


def _gated_group_norm(y, z, nw_ref):
    y = y * _silu(z)
    outs = []
    for g in range(N_SSM_GROUPS):
        yg = y[:, g * HALF_INNER:(g + 1) * HALF_INNER]
        outs.append(yg * lax.rsqrt(jnp.mean(yg * yg, axis=-1, keepdims=True) + NORM_EPS))
    return jnp.concatenate(outs, axis=1) * nw_ref[...]


def _mamba_prompt_kernel(z_ref, xbc_ref, gd_ref, cw_ref, cb_ref, dtb_ref, alog_ref, e16_ref, dsk_ref, nw_ref, tri_ref,
                         y_ref, st_ref, xpad_ref, state_ref):
    c = pl.program_id(1)
    nq = SSD_Q

    @pl.when(c == 0)
    def _():
        state_ref[...] = jnp.zeros(state_ref.shape, f32)
        xpad_ref[0:8, :] = jnp.zeros((8, CONV_DIM), f32)

    xpad_ref[8:8 + nq, :] = xbc_ref[...]
    conv = cb_ref[...]
    for w in range(CONV_W):
        conv = conv + xpad_ref[8 - (CONV_W - 1) + w:8 - (CONV_W - 1) + w + nq, :] * cw_ref[w:w + 1, :]
    xpad_ref[0:8, :] = xpad_ref[nq:nq + 8, :]
    act = _silu(conv)
    xs, bm, cm = act[:, :D_INNER], act[:, D_INNER:D_INNER + BC_W], act[:, D_INNER + BC_W:]

    dt, a = _dt_and_decay(gd_ref[...], dtb_ref, alog_ref)
    a_cs = sum(jnp.dot(tri_ref[...], t, preferred_element_type=f32) for t in _split3(a))
    ea = jnp.exp(a_cs)
    te = jnp.exp(a_cs[nq - 1:nq, :] - a_cs)
    e16 = e16_ref[...]
    dt_x, ea_x, te_x = _dot_exact(dt, e16), _dot_exact(ea, e16), _dot_exact(te, e16)
    xdt = xs * dt_x
    xw = (xdt * te_x).astype(bf16)
    a_cst = a_cs.T
    tril = lax.broadcasted_iota(jnp.int32, (nq, nq), 1) <= lax.broadcasted_iota(jnp.int32, (nq, nq), 0)

    ys = []
    for g in range(N_SSM_GROUPS):
        bg = bm[:, g * D_STATE:(g + 1) * D_STATE]
        cgb = cm[:, g * D_STATE:(g + 1) * D_STATE].astype(bf16)
        cb = _nt(cgb, bg.astype(bf16))
        st = state_ref[g]
        yoff = jnp.dot(cgb, st.astype(bf16), preferred_element_type=f32)
        gl = slice(g * HALF_INNER, (g + 1) * HALF_INNER)
        state_ref[g] = ea_x[nq - 1:nq, gl] * st + jnp.dot(bg.T.astype(bf16), xw[:, gl], preferred_element_type=f32)
        for hp in range(HALF_INNER // LANES):
            pl_ = slice(g * HALF_INNER + hp * LANES, g * HALF_INNER + (hp + 1) * LANES)
            xp = xdt[:, pl_]
            yp = ea_x[:, pl_] * yoff[:, hp * LANES:(hp + 1) * LANES] + dsk_ref[:, pl_] * xs[:, pl_]
            for e in range(2):
                lane_h = DT_LANE0 + g * (N_SSM_HEADS // N_SSM_GROUPS) + hp * 2 + e
                seg = a_cs[:, lane_h:lane_h + 1] - a_cst[lane_h:lane_h + 1, :]
                wgt = (cb * jnp.exp(jnp.where(tril, seg, NEG))).astype(bf16)
                xz = jnp.where(_half_mask((nq, LANES), e), xp, 0.0).astype(bf16)
                yp = yp + jnp.dot(wgt, xz, preferred_element_type=f32)
            ys.append(yp)
    y = jnp.concatenate(ys, axis=1)
    y_ref[...] = _gated_group_norm(y, z_ref[...], nw_ref).astype(bf16)

    @pl.when(c == pl.num_programs(1) - 1)
    def _():
        st_ref[...] = state_ref[...]


def _mamba_prompt(z, xbc, gd, mc):
    nc = SEQ // SSD_Q
    row = lambda w: pl.BlockSpec((SSD_Q, w), lambda b, c: (b * nc + c, 0))
    y, st = pl.pallas_call(
        _mamba_prompt_kernel,
        grid=(BATCH, nc),
        in_specs=[row(D_INNER), row(CONV_DIM), row(GD_W)] + [_const_spec(a, 2) for a in mc],
        out_specs=[row(D_INNER), pl.BlockSpec((None, N_SSM_GROUPS, D_STATE, HALF_INNER), lambda b, c: (b, 0, 0, 0))],
        out_shape=[jax.ShapeDtypeStruct((BATCH * SEQ, D_INNER), bf16),
                   jax.ShapeDtypeStruct((BATCH, N_SSM_GROUPS, D_STATE, HALF_INNER), f32)],
        scratch_shapes=[pltpu.VMEM((SSD_Q + 8, CONV_DIM), f32), pltpu.VMEM((N_SSM_GROUPS, D_STATE, HALF_INNER), f32)],
        compiler_params=_cparams(("parallel", "arbitrary")),
        name="mamba_prompt",
    )(z, xbc, gd, *mc)
    hpg = N_SSM_HEADS // N_SSM_GROUPS
    st = st.reshape(BATCH, N_SSM_GROUPS, D_STATE, hpg, SSM_HEAD_DIM).transpose(0, 1, 3, 4, 2)
    return y, st.reshape(BATCH, N_SSM_HEADS, SSM_HEAD_DIM, D_STATE)


def _mamba_sample_kernel(z_ref, xbc_ref, gd_ref, sc_ref, s_ref, cw_ref, cb_ref, dtb_ref, alog_ref, e16_ref, dsk_ref,
                         nw_ref, y_ref, so_ref):
    conv = cb_ref[...] + xbc_ref[...] * cw_ref[CONV_W - 1:CONV_W, :]
    for w in range(CONV_W - 1):
        conv = conv + sc_ref[w:w + 1, :] * cw_ref[w:w + 1, :]
    act = _silu(conv)
    xs, bm, cm = act[:, :D_INNER], act[:, D_INNER:D_INNER + BC_W], act[:, D_INNER + BC_W:]
    dt, a = _dt_and_decay(jnp.broadcast_to(gd_ref[...], (8, GD_W)), dtb_ref, alog_ref)
    e16 = e16_ref[...]
    dt_x = _dot_exact(dt, e16)[0:1]
    da_x = _dot_exact(jnp.exp(a), e16)[0:1]
    rows = N_SSM_HEADS * SSM_HEAD_DIM
    xcol = jnp.broadcast_to(xs * dt_x, (D_STATE, rows)).T
    acol = jnp.broadcast_to(da_x, (D_STATE, rows)).T
    rowi = lax.broadcasted_iota(jnp.int32, (rows, D_STATE), 0)
    bfull = jnp.where(rowi < HALF_INNER, bm[:, :D_STATE], bm[:, D_STATE:])
    snew = s_ref[...] * acol + xcol * bfull
    so_ref[...] = snew
    r8 = lax.broadcasted_iota(jnp.int32, (8, D_STATE), 0)
    c8 = jnp.where(r8 == 0, cm[:, :D_STATE], jnp.where(r8 == 1, cm[:, D_STATE:], 0.0)).astype(bf16)
    yy = _nt(c8, snew.astype(bf16))
    lane = lax.broadcasted_iota(jnp.int32, (1, rows), 1)
    y = jnp.where(lane < HALF_INNER, yy[0:1], yy[1:2]) + dsk_ref[...] * xs
    y_ref[...] = _gated_group_norm(y, z_ref[...], nw_ref).astype(bf16)


def _mamba_sample(z, xbc, gd, state_conv, state_ssm, mc):
    nb = z.shape[0]
    mc = mc[:7]
    rows = N_SSM_HEADS * SSM_HEAD_DIM
    one = lambda w: pl.BlockSpec((None, 1, w), lambda b: (b, 0, 0))
    s_spec = pl.BlockSpec((None, rows, D_STATE), lambda b: (b, 0, 0))
    y, so = pl.pallas_call(
        _mamba_sample_kernel,
        grid=(nb,),
        in_specs=[one(D_INNER), one(CONV_DIM), one(GD_W),
                  pl.BlockSpec((None, CONV_W - 1, CONV_DIM), lambda b: (b, 0, 0)), s_spec]
                 + [_const_spec(a, 1) for a in mc],
        out_specs=[one(D_INNER), s_spec],
        out_shape=[jax.ShapeDtypeStruct((nb, 1, D_INNER), bf16), jax.ShapeDtypeStruct((nb, rows, D_STATE), f32)],
        compiler_params=_cparams(("parallel",)),
        name="mamba_sample",
    )(z[:, None], xbc[:, None], gd[:, None], state_conv, state_ssm.reshape(nb, rows, D_STATE), *mc)
    return y[:, 0], so.reshape(nb, N_SSM_HEADS, SSM_HEAD_DIM, D_STATE)


N_CMP_S = PAST_LEN // CMP_STRIDE - 1
N_SLC_S = PAST_LEN // SEL_BLOCK + 1
N_SLC_PAD = 2 * LANES
HALF_PAGE = PAGE_SIZE // SEL_BLOCK
GONE = -3.0e38


def _cmp_attn_sample_kernel(qz_ref, kd_ref, vl_ref, vh_ref, mts_ref, o_ref, idx_ref):
    qz = (qz_ref[...] * ATTN_SCALE).astype(bf16)
    ncmp = kd_ref.shape[0]
    nidx = lax.broadcasted_iota(jnp.int32, (N_HEADS, ncmp), 1)
    rowh = lax.broadcasted_iota(jnp.int32, (N_HEADS, ncmp), 0)
    mask = (nidx < N_CMP_S) & (nidx * CMP_STRIDE + (CMP_BLOCK - 1) <= PAST_LEN)
    rowo = lax.broadcasted_iota(jnp.int32, (N_HEADS, LANES), 0)
    r8 = lax.broadcasted_iota(jnp.int32, (8, N_SLC_PAD), 0)
    o_acc = jnp.zeros((N_HEADS, LANES), f32)
    imp = jnp.zeros((8, N_SLC_PAD), f32)
    for g in range(N_KV):
        gl = slice(g * LANES, (g + 1) * LANES)
        s = jnp.where(mask, _nt(qz, kd_ref[:, gl]), NEG)
        m = jnp.max(s, axis=1, keepdims=True)
        ex = jnp.where(mask, jnp.exp(s - m), 0.0)
        p = ex * (1.0 / jnp.maximum(jnp.sum(ex, axis=1, keepdims=True), 1e-30))
        p = jnp.where(rowh // Q_PER_KV == g, p, 0.0)
        pb = p.astype(bf16)
        o_l = jnp.dot(pb, vl_ref[:, gl], preferred_element_type=f32)
        o_h = jnp.dot(pb, vh_ref[:, gl], preferred_element_type=f32)
        o_acc = o_acc + jnp.where(rowo % 2 == 0, o_l, o_h)
        hi, lo = _split2(p)
        imp_h = jnp.dot(hi, mts_ref[...], preferred_element_type=f32) + jnp.dot(lo, mts_ref[...], preferred_element_type=f32)
        imp = imp + jnp.where(r8 == g, jnp.sum(imp_h, axis=0, keepdims=True), 0.0)
    o_ref[...] = o_acc
    j = lax.broadcasted_iota(jnp.int32, (8, N_SLC_PAD), 1)
    jt = PAST_LEN // SEL_BLOCK
    imp = jnp.where((j == 0) | (j == jt) | (j == jt - 1), SELECT_FORCE, imp)
    imp = jnp.where(j > jt, -SELECT_FORCE, imp)
    imp = jnp.where(j >= N_SLC_S, NEG, imp)
    jf = j.astype(f32)
    lane = lax.broadcasted_iota(jnp.int32, (8, LANES), 1)
    picked = jnp.zeros((8, LANES), f32)
    for k in range(SEL_TOPN):
        m = jnp.max(imp, axis=1, keepdims=True)
        ix = jnp.min(jnp.where(imp == m, jf, float(N_SLC_PAD)), axis=1, keepdims=True)
        picked = jnp.where(lane == k, ix, picked)
        imp = jnp.where(jf == ix, GONE, imp)
    idx_ref[...] = picked.astype(jnp.int32)


def _cmp_attn_sample(qz, kd, vl, vh):
    nb, ncmp = kd.shape[0], kd.shape[1]
    mts = np.zeros((ncmp, N_SLC_PAD), np.float32)
    mts[:, :N_SLC_S] = _cmp_to_sel_matrix_t(ncmp, N_SLC_S, N_CMP_S).T
    mts = jnp.asarray(mts, bf16)
    kv_spec = pl.BlockSpec((None, ncmp, KV_W), lambda b: (b, 0, 0))
    return pl.pallas_call(
        _cmp_attn_sample_kernel,
        grid=(nb,),
        in_specs=[pl.BlockSpec((None, N_HEADS, LANES), lambda b: (b, 0, 0)), kv_spec, kv_spec, kv_spec,
                  pl.BlockSpec(mts.shape, lambda b: (0, 0))],
        out_specs=[pl.BlockSpec((None, N_HEADS, LANES), lambda b: (b, 0, 0)),
                   pl.BlockSpec((None, 8, LANES), lambda b: (b, 0, 0))],
        out_shape=[jax.ShapeDtypeStruct((nb, N_HEADS, LANES), f32), jax.ShapeDtypeStruct((nb, 8, LANES), jnp.int32)],
        compiler_params=_cparams(("parallel",)),
        name="cmp_attn_sample",
    )(qz, kd, vl, vh, mts)


def _one_query_softmax(s, s_new, v, v_new):
    m = jnp.maximum(jnp.max(s, axis=1, keepdims=True), s_new)
    p = jnp.exp(s - m)
    p_new = jnp.exp(s_new - m)
    den = jnp.sum(p, axis=1, keepdims=True) + p_new
    num = jnp.dot(p.astype(bf16), v, preferred_element_type=f32) + p_new * v_new
    return num * (1.0 / den)


def _selwin_sample_kernel(idx_ref, pt_ref, *refs):
    del pt_ref
    kb, vb = refs[:SEL_TOPN], refs[SEL_TOPN:2 * SEL_TOPN]
    qg_ref, ksn_ref, vsn_ref, wk_ref, wv_ref, kwn_ref, vwn_ref, os_ref, ow_ref = refs[2 * SEL_TOPN:]
    b, g = pl.program_id(0), pl.program_id(1)
    q8 = qg_ref[...] * ATTN_SCALE
    qb = q8.astype(bf16)
    kcat = jnp.concatenate([r[...] for r in kb], axis=0).astype(bf16)
    vcat = jnp.concatenate([r[...] for r in vb], axis=0).astype(bf16)
    n = SEL_TOPN * SEL_BLOCK
    slot = lax.broadcasted_iota(jnp.int32, (8, n), 1) // SEL_BLOCK
    blk = jnp.zeros((8, n), jnp.int32)
    for k in range(SEL_TOPN):
        blk = jnp.where(slot == k, idx_ref[(b * N_KV + g) * SEL_TOPN + k], blk)
    s = jnp.where(blk < N_SLC_S - 1, _nt(qb, kcat), NEG)
    s_new = jnp.sum(q8 * ksn_ref[...], axis=1, keepdims=True)
    os_ref[...] = _one_query_softmax(s, s_new, vcat, vsn_ref[...])
    nbuf = wk_ref.shape[0]
    i = lax.broadcasted_iota(jnp.int32, (8, nbuf), 1)
    s = jnp.where(nbuf - i < WINDOW, _nt(qb, wk_ref[...].astype(bf16)), NEG)
    s_new = jnp.sum(q8 * kwn_ref[...], axis=1, keepdims=True)
    ow_ref[...] = _one_query_softmax(s, s_new, wv_ref[...].astype(bf16), vwn_ref[...])


def _selwin_sample(idx, page_table, cache_s, qg8, ks_new, win_buf, kw_new):
    nb = qg8.shape[0]
    nbuf = win_buf.shape[1]
    blocks = cache_s.reshape(-1, SEL_BLOCK, KV_W)

    def blk_map(b, g, idx_ref, pt_ref, k, c):
        j = jnp.minimum(idx_ref[(b * N_KV + g) * SEL_TOPN + k], N_SLC_S - 2)
        return (pt_ref[b * N_PAGES + j // HALF_PAGE] * HALF_PAGE + j % HALF_PAGE, 0, 2 * c + g // 2)

    blk_specs = [pl.BlockSpec((None, SEL_BLOCK, LANES), functools.partial(blk_map, k=k, c=c))
                 for c in range(2) for k in range(SEL_TOPN)]
    new_spec = lambda c: pl.BlockSpec((None, 1, LANES), lambda b, g, *_: (b, 0, 2 * c + g // 2))
    win_spec = lambda c: pl.BlockSpec((None, nbuf, LANES), lambda b, g, *_: (b, 0, 2 * c + g // 2))
    o_spec = pl.BlockSpec((None, None, 8, LANES), lambda b, g, *_: (b, g, 0, 0))
    grid_spec = pltpu.PrefetchScalarGridSpec(
        num_scalar_prefetch=2,
        grid=(nb, N_KV),
        in_specs=blk_specs + [o_spec, new_spec(0), new_spec(1), win_spec(0), win_spec(1), new_spec(0), new_spec(1)],
        out_specs=[o_spec, o_spec])
    o_shape = jax.ShapeDtypeStruct((nb, N_KV, 8, LANES), f32)
    ks3, kw3 = ks_new[:, None], kw_new[:, None]
    wb = win_buf.reshape(nb, nbuf, KV_W)
    return pl.pallas_call(
        _selwin_sample_kernel,
        grid_spec=grid_spec,
        out_shape=[o_shape, o_shape],
        compiler_params=_cparams(("arbitrary", "arbitrary")),
        name="selwin_sample",
    )(idx.reshape(-1), page_table.reshape(-1), *([blocks] * (2 * SEL_TOPN)), qg8, ks3, ks3, wb, wb, kw3, kw3)


def _sample_query_layouts(q):
    nb = q.shape[0]
    q16 = q.reshape(nb, N_HEADS, HEAD_DIM)
    z = jnp.zeros_like(q16)
    lo, hi = jnp.concatenate([q16, z], axis=-1), jnp.concatenate([z, q16], axis=-1)
    h = jnp.arange(N_HEADS)[None, :, None]
    qz = jnp.where(h % 2 == 0, lo, hi)
    qg = jnp.where((h // Q_PER_KV) % 2 == 0, lo, hi).reshape(nb, N_KV, Q_PER_KV, LANES)
    qg8 = jnp.concatenate([qg, jnp.zeros_like(qg)], axis=2)
    return qz, qg8


def _unpad_heads_by_parity(o16):
    nb = o16.shape[0]
    o = o16.reshape(nb, N_HEADS // 2, 2, 2, HEAD_DIM)
    return jnp.stack([o[:, :, 0, 0], o[:, :, 1, 1]], axis=2).reshape(nb, ATTN_W)


def _unpad_heads_by_group(o48):
    nb = o48.shape[0]
    o = o48[:, :, :Q_PER_KV].reshape(nb, N_KV // 2, 2, Q_PER_KV, 2, HEAD_DIM)
    return jnp.stack([o[:, :, 0, :, 0], o[:, :, 1, :, 1]], axis=2).reshape(nb, ATTN_W)


def _finish(x, ocmp, osel, owin, gd, ssd_y, w_out, ln_mlp, w_up, w_down, ln_final, tm, tm_mlp, tf):
    x1 = _outproj(x, ocmp, osel, owin, gd, ssd_y, _gate_expand(), w_out, tm)
    return _mlp(x1, ln_mlp, w_up, w_down, ln_final, tm_mlp, tf)


def kernel(x_prompt, x_sample, cache_kv_cmp, cache_kv_sel, state_kv_win, state_conv, state_ssm, page_table, ln_mix, w_in, cmp_pe, cmp_w1, cmp_b1, cmp_w2, cmp_b2, conv_w, conv_b, dt_bias, a_log, d_skip, ssm_norm, w_out, ln_mlp, w_up, w_down, ln_final):
    nb = x_sample.shape[0]
    w_in_p = _prep_w_in(w_in[0])
    w_out_b, w_up_b, w_down_b = w_out[0].astype(bf16), w_up[0].astype(bf16), w_down[0].astype(bf16)
    lnw, lnm, lnf = ln_mix[0][None], ln_mlp[0][None], ln_final[None]
    cw = _compress_weights(cmp_pe[0], cmp_w1[0], cmp_b1[0], cmp_w2[0], cmp_b2[0])
    mc = _mamba_consts(conv_w[0], conv_b[0], dt_bias[0], a_log[0], d_skip[0], ssm_norm[0])
    kv_shape = (2, N_KV, HEAD_DIM)

    xp = x_prompt.reshape(BATCH * SEQ, D_MODEL)
    q, z, xbc, kc, ks, kw, gd = _inproj(xp, lnw, w_in_p, 256)
    kd, vl, vh = _compress_prompt(kc, cw)
    ocmp, selm = _cmp_attn_prompt(q, kd, vl, vh)
    osel = _sel_attn_prompt(q, selm, ks)
    owin = _win_attn_prompt(q, kw)
    ssd_y, ssm_p = _mamba_prompt(z, xbc, gd, mc)
    y_prompt = _finish(xp, ocmp, osel, owin, gd, ssd_y, w_out_b, lnm, w_up_b, w_down_b, lnf, 256, 512, 1024)

    xs = x_sample.reshape(nb, D_MODEL)
    q_s, z_s, xbc_s, kc_s, ks_s, kw_s, gd_s = _inproj(xs, lnw, w_in_p, nb)
    kd_s, vl_s, vh_s = _compress_sample(cache_kv_cmp[0], page_table, cw)
    qz, qg8 = _sample_query_layouts(q_s)
    ocmp16, idx8 = _cmp_attn_sample(qz, kd_s, vl_s, vh_s)
    idx = idx8[:, :N_KV, :SEL_TOPN]
    osel48, owin48 = _selwin_sample(idx, page_table, cache_kv_sel[0], qg8, ks_s, state_kv_win[0], kw_s)
    ssd_y_s, ssm_s = _mamba_sample(z_s, xbc_s, gd_s, state_conv[0], state_ssm[0], mc)
    y_sample = _finish(xs, _unpad_heads_by_parity(ocmp16), _unpad_heads_by_group(osel48),
                       _unpad_heads_by_group(owin48), gd_s, ssd_y_s, w_out_b, lnm, w_up_b, w_down_b, lnf, nb, nb, 1024)

    kv_win_p = kw.reshape(BATCH, SEQ, *kv_shape)[:, -min(WINDOW, SEQ):]
    conv_p = xbc.reshape(BATCH, SEQ, CONV_DIM)[:, -(CONV_W - 1):]
    kv_win_s = jnp.concatenate([state_kv_win[0], kw_s.reshape(nb, 1, *kv_shape)], axis=1)[:, -WINDOW:]
    conv_s = jnp.concatenate([state_conv[0], xbc_s[:, None]], axis=1)[:, -(CONV_W - 1):]
    return (y_prompt.reshape(BATCH, SEQ, D_MODEL), y_sample.reshape(nb, 1, D_MODEL),
            kc.reshape(1, BATCH, SEQ, *kv_shape), ks.reshape(1, BATCH, SEQ, *kv_shape), kv_win_p[None],
            conv_p[None], ssm_p[None],
            kc_s.reshape(1, nb, 1, *kv_shape), ks_s.reshape(1, nb, 1, *kv_shape), kv_win_s[None],
            conv_s[None], ssm_s[None])
```

```python
import functools
import math

import jax
import jax.numpy as jnp
import numpy as np
from jax import lax
from jax.experimental import pallas as pl
from jax.experimental.pallas import tpu as pltpu

f32 = jnp.float32
bf16 = jnp.bfloat16

D_MODEL = 2048
BATCH = 2
SEQ = 4096
DEC_BATCH = 32
PAST_LEN = 8192
PAGE_SIZE = 128
HEAD_DIM = 64
N_HEADS = 16
N_KV = 4
Q_PER_KV = 4
ATTN_W = 1024
KV_W = 512
CMP_BLOCK = 32
CMP_STRIDE = 16
CMP_HIDDEN = 128
SEL_BLOCK = 64
SEL_TOPN = 16
WINDOW = 512
SELECT_FORCE = 1.0e4
ATTN_SCALE = HEAD_DIM ** -0.5
D_INNER = 1024
SSM_HEAD_DIM = 64
N_SSM_HEADS = 16
N_SSM_GROUPS = 2
D_STATE = 128
CONV_W = 4
CONV_DIM = D_INNER + 2 * N_SSM_GROUPS * D_STATE
D_FF = 4 * D_MODEL
IN_SPLITS = (ATTN_W, KV_W, KV_W, KV_W, 3 * N_HEADS, D_INNER, CONV_DIM, N_SSM_HEADS)
NORM_EPS = 1e-5

LANES = 128
GD_W = LANES
DT_LANE0 = 3 * N_HEADS
N_PAGES = PAST_LEN // PAGE_SIZE
SEG_PER_PAGE = PAGE_SIZE // CMP_STRIDE
SEG_W = CMP_STRIDE * KV_W
NEG = -1.0e30
SEL_NEG = -1.0e9
VMEM_LIMIT = 56 * 1024 * 1024


def _cparams(sem):
    return pltpu.CompilerParams(dimension_semantics=sem, vmem_limit_bytes=VMEM_LIMIT)


def _nt(a, b):
    return lax.dot_general(a, b, (((1,), (1,)), ((), ())), preferred_element_type=f32)


def _split2(x):
    hi = x.astype(bf16)
    lo = (x - hi.astype(f32)).astype(bf16)
    return hi, lo


def _split3(x):
    hi = x.astype(bf16)
    r = x - hi.astype(f32)
    mid = r.astype(bf16)
    lo = (r - mid.astype(f32)).astype(bf16)
    return hi, mid, lo


def _dot_exact(x, w):
    return sum(jnp.dot(t, w, preferred_element_type=f32) for t in _split3(x))


def _silu(x):
    return x * (1.0 / (1.0 + jnp.exp(-x)))


def _sigmoid(x):
    return 1.0 / (1.0 + jnp.exp(-x))


def _softplus(x):
    return jnp.maximum(x, 0.0) + jnp.log1p(jnp.exp(-jnp.abs(x)))


IN_OUT_WIDTHS = (ATTN_W, D_INNER, CONV_DIM, KV_W, KV_W, KV_W, GD_W)


def _inproj_kernel(x_ref, lnw_ref, w_ref, *out_refs):
    x = x_ref[...]
    h = x * lax.rsqrt(jnp.mean(x * x, axis=-1, keepdims=True) + NORM_EPS) * lnw_ref[...]
    h = h.astype(bf16)
    off = 0
    for ref in out_refs:
        n = ref.shape[-1]
        ref[...] = jnp.dot(h, w_ref[:, off:off + n], preferred_element_type=f32)
        off += n


def _inproj(x, lnw, w_perm, tm):
    m = x.shape[0]
    n_tot = sum(IN_OUT_WIDTHS)
    return pl.pallas_call(
        _inproj_kernel,
        grid=(m // tm,),
        in_specs=[pl.BlockSpec((tm, D_MODEL), lambda i: (i, 0)),
                  pl.BlockSpec((1, D_MODEL), lambda i: (0, 0)),
                  pl.BlockSpec((D_MODEL, n_tot), lambda i: (0, 0), pipeline_mode=pl.Buffered(1))],
        out_specs=[pl.BlockSpec((tm, n), lambda i: (i, 0)) for n in IN_OUT_WIDTHS],
        out_shape=[jax.ShapeDtypeStruct((m, n), f32) for n in IN_OUT_WIDTHS],
        compiler_params=_cparams(("parallel",)),
        name="inproj",
    )(x, lnw, w_perm)


def _prep_w_in(w_in):
    parts, off = [], 0
    for width in IN_SPLITS:
        parts.append(w_in[:, off:off + width])
        off += width
    q, kc, ks, kw, g, z, xbc, dt = parts
    gd = jnp.concatenate([g, dt, jnp.zeros((D_MODEL, GD_W - 4 * N_HEADS), w_in.dtype)], axis=1)
    return jnp.concatenate([q, z, xbc, kc, ks, kw, gd], axis=1).astype(bf16)


def _outproj_kernel(x_ref, oc_ref, os_ref, ow_ref, gd_ref, y_ref, eg_ref, w_ref, o_ref):
    gates = _sigmoid(gd_ref[...])
    hi, lo = _split2(gates)
    attn = None
    for c, br in enumerate((oc_ref, os_ref, ow_ref)):
        ge = (jnp.dot(hi, eg_ref[c], preferred_element_type=f32)
              + jnp.dot(lo, eg_ref[c], preferred_element_type=f32))
        term = ge * br[...]
        attn = term if attn is None else attn + term
    mix = (jnp.dot(attn.astype(bf16), w_ref[:ATTN_W, :], preferred_element_type=f32)
           + jnp.dot(y_ref[...], w_ref[ATTN_W:, :], preferred_element_type=f32))
    o_ref[...] = x_ref[...] + mix


def _outproj(x, ocmp, osel, owin, gd, ssd_y, eg, w_out, tm):
    m = x.shape[0]
    row = lambda w: pl.BlockSpec((tm, w), lambda i: (i, 0))
    return pl.pallas_call(
        _outproj_kernel,
        grid=(m // tm,),
        in_specs=[row(D_MODEL), row(ATTN_W), row(ATTN_W), row(ATTN_W), row(GD_W), row(D_INNER),
                  pl.BlockSpec((3, GD_W, ATTN_W), lambda i: (0, 0, 0)),
                  pl.BlockSpec((ATTN_W + D_INNER, D_MODEL), lambda i: (0, 0))],
        out_specs=row(D_MODEL),
        out_shape=jax.ShapeDtypeStruct((m, D_MODEL), f32),
        compiler_params=_cparams(("parallel",)),
        name="outproj",
    )(x, ocmp, osel, owin, gd, ssd_y, eg, w_out)


def _gate_expand():
    eg = np.zeros((3, GD_W, ATTN_W), np.float32)
    for h in range(N_HEADS):
        for c in range(3):
            eg[c, h * 3 + c, h * HEAD_DIM:(h + 1) * HEAD_DIM] = 1.0
    return jnp.asarray(eg, bf16)


def _mlp_kernel(x_ref, ln_ref, wu_ref, wd_ref, lnf_ref, o_ref, h_ref, acc_ref):
    k = pl.program_id(1)

    @pl.when(k == 0)
    def _():
        x = x_ref[...]
        h = x * lax.rsqrt(jnp.mean(x * x, axis=-1, keepdims=True) + NORM_EPS) * ln_ref[...]
        h_ref[...] = h.astype(bf16)
        acc_ref[...] = jnp.zeros_like(acc_ref)

    u = jnp.dot(h_ref[...], wu_ref[...], preferred_element_type=f32)
    u = jnp.square(jnp.maximum(u, 0.0)).astype(bf16)
    acc_ref[...] += jnp.dot(u, wd_ref[...], preferred_element_type=f32)

    @pl.when(k == pl.num_programs(1) - 1)
    def _():
        y = x_ref[...] + acc_ref[...]
        y = y * lax.rsqrt(jnp.mean(y * y, axis=-1, keepdims=True) + NORM_EPS) * lnf_ref[...]
        o_ref[...] = y


def _mlp(x, ln_mlp, w_up, w_down, ln_final, tm, tf):
    m = x.shape[0]
    return pl.pallas_call(
        _mlp_kernel,
        grid=(m // tm, D_FF // tf),
        in_specs=[pl.BlockSpec((tm, D_MODEL), lambda i, k: (i, 0)),
                  pl.BlockSpec((1, D_MODEL), lambda i, k: (0, 0)),
                  pl.BlockSpec((D_MODEL, tf), lambda i, k: (0, k)),
                  pl.BlockSpec((tf, D_MODEL), lambda i, k: (k, 0)),
                  pl.BlockSpec((1, D_MODEL), lambda i, k: (0, 0))],
        out_specs=pl.BlockSpec((tm, D_MODEL), lambda i, k: (i, 0)),
        out_shape=jax.ShapeDtypeStruct((m, D_MODEL), f32),
        scratch_shapes=[pltpu.VMEM((tm, D_MODEL), bf16), pltpu.VMEM((tm, D_MODEL), f32)],
        compiler_params=_cparams(("parallel", "arbitrary")),
        name="mlp",
    )(x, ln_mlp, w_up, w_down, ln_final)


N_SLAB_K = CMP_STRIDE * LANES


def _compress_weights(cmp_pe, cmp_w1, cmp_b1, cmp_w2, cmp_b2):
    eye2 = jnp.eye(2, dtype=f32)
    w1 = cmp_w1.reshape(2, 2, CMP_STRIDE, HEAD_DIM, CMP_HIDDEN)
    w1cat = jnp.einsum('crjdf,ab->cjadrbf', w1, eye2).reshape(2, N_SLAB_K, 4 * CMP_HIDDEN).astype(bf16)
    w1flat = cmp_w1.reshape(2, CMP_BLOCK * HEAD_DIM, CMP_HIDDEN).astype(bf16)
    pe8 = jnp.broadcast_to(cmp_pe.reshape(2, 1, CMP_BLOCK * HEAD_DIM), (2, 8, CMP_BLOCK * HEAD_DIM))
    b1 = cmp_b1.reshape(2, 1, CMP_HIDDEN)
    wk, wv = cmp_w2[0], cmp_w2[1]
    zk = jnp.zeros_like(wk)
    w2k = jnp.einsum('fd,ab,e->afbed', wk, eye2, jnp.ones((2,), f32)).reshape(2 * CMP_HIDDEN, 4 * HEAD_DIM)
    b2k = jnp.tile(cmp_b2[0], 4)[None]
    vl = jnp.einsum('fd,ab,e->afbed', wv, eye2, jnp.array([1.0, 0.0], f32)).reshape(2 * CMP_HIDDEN, 4 * HEAD_DIM)
    vh = jnp.einsum('fd,ab,e->afbed', wv, eye2, jnp.array([0.0, 1.0], f32)).reshape(2 * CMP_HIDDEN, 4 * HEAD_DIM)
    w2v = jnp.concatenate([vl, vh], axis=1)
    zb = jnp.zeros((HEAD_DIM,), f32)
    b2v = jnp.concatenate([cmp_b2[1], zb, cmp_b2[1], zb, zb, cmp_b2[1], zb, cmp_b2[1]])[None]
    del zk
    return pe8, w1flat, b1, w1cat, w2k.astype(bf16), b2k, w2v.astype(bf16), b2v


def _compress_slab(a, c, gp, prm_refs, out_refs):
    pe_ref, w1f_ref, b1_ref, w1_ref, w2k_ref, b2k_ref, w2v_ref, b2v_ref = prm_refs
    kd_ref, vl_ref, vh_ref = out_refs
    pet = jnp.dot(pe_ref[c].astype(bf16), w1f_ref[c], preferred_element_type=f32)[0:1] + b1_ref[c]
    bias = jnp.concatenate([pet, pet], axis=1)
    p = jnp.dot(a, w1_ref[c], preferred_element_type=f32)
    s = p.shape[0]
    hid = p[:, :2 * CMP_HIDDEN] + pltpu.roll(p[:, 2 * CMP_HIDDEN:], s - 1, axis=0)
    hid = _silu(hid + bias).astype(bf16)
    w = 4 * HEAD_DIM
    if c == 0:
        kd_ref[:, gp * w:(gp + 1) * w] = (jnp.dot(hid, w2k_ref[...], preferred_element_type=f32)
                                          + b2k_ref[...]).astype(bf16)
    else:
        o = jnp.dot(hid, w2v_ref[...], preferred_element_type=f32) + b2v_ref[...]
        vl_ref[:, gp * w:(gp + 1) * w] = o[:, :w].astype(bf16)
        vh_ref[:, gp * w:(gp + 1) * w] = o[:, w:].astype(bf16)


def _slab_lane0(j, c, gp):
    return j * KV_W + c * (KV_W // 2) + gp * LANES


def _compress_all_slabs(x_ref, prm_refs, out_refs):
    for c in range(2):
        for gp in range(2):
            a = jnp.concatenate(
                [x_ref[:, _slab_lane0(j, c, gp):_slab_lane0(j, c, gp) + LANES] for j in range(CMP_STRIDE)],
                axis=1).astype(bf16)
            _compress_slab(a, c, gp, prm_refs, out_refs)


def _compress_prompt_kernel(x_ref, *refs):
    _compress_all_slabs(x_ref, refs[:8], refs[8:11])


def _seg_perm():
    m = np.zeros((2 * PAGE_SIZE, 2 * PAGE_SIZE), np.float32)
    for pp in range(2):
        for s in range(SEG_PER_PAGE):
            for j in range(CMP_STRIDE):
                m[j * 2 * SEG_PER_PAGE + pp * SEG_PER_PAGE + s, pp * PAGE_SIZE + s * CMP_STRIDE + j] = 1.0
    return jnp.asarray(m, bf16)


def _compress_sample_kernel(pt_ref, *refs):
    del pt_ref
    pages, perm_ref = refs[:N_PAGES], refs[N_PAGES]
    prm_refs, out_refs, o_ref = refs[N_PAGES + 1:N_PAGES + 9], refs[N_PAGES + 9:N_PAGES + 12], refs[-1]
    rows = 2 * SEG_PER_PAGE
    for k in range(N_PAGES // 2):
        two = jnp.concatenate([pages[2 * k][...], pages[2 * k + 1][...]], axis=1).astype(bf16)
        o_ref[k] = _nt(perm_ref[...], two).astype(bf16)
    nseg = N_PAGES * SEG_PER_PAGE
    for c in range(2):
        for gp in range(2):
            l0 = c * (KV_W // 2) + gp * LANES
            a = jnp.concatenate([o_ref[:, j * rows:(j + 1) * rows, l0:l0 + LANES].reshape(nseg, LANES)
                                 for j in range(CMP_STRIDE)], axis=1)
            _compress_slab(a, c, gp, prm_refs, out_refs)


def _const_spec(a, n_grid, n_prefetch=0):
    zeros = (0,) * a.ndim
    if n_grid == 1:
        return pl.BlockSpec(a.shape, (lambda b, *_: zeros))
    return pl.BlockSpec(a.shape, (lambda b, c, *_: zeros))


def _compress_prompt(kc, cw):
    nseg = SEQ // CMP_STRIDE
    x = kc.reshape(BATCH, nseg, SEG_W)
    out = jax.ShapeDtypeStruct((BATCH, nseg, KV_W), bf16)
    return pl.pallas_call(
        _compress_prompt_kernel,
        grid=(BATCH,),
        in_specs=[pl.BlockSpec((None, nseg, SEG_W), lambda b: (b, 0, 0))] + [_const_spec(a, 1) for a in cw],
        out_specs=[pl.BlockSpec((None, nseg, KV_W), lambda b: (b, 0, 0))] * 3,
        out_shape=[out] * 3,
        compiler_params=_cparams(("parallel",)),
        name="compress_prompt",
    )(x, *cw)


def _compress_sample(cache, page_table, cw):
    nseg = PAST_LEN // CMP_STRIDE
    x = _token_minor(cache)
    perm = _seg_perm()
    out = jax.ShapeDtypeStruct((DEC_BATCH, nseg, KV_W), bf16)
    page_specs = [pl.BlockSpec((None, KV_W, PAGE_SIZE), functools.partial(lambda b, pt, p: (pt[b * N_PAGES + p], 0, 0), p=p))
                  for p in range(N_PAGES)]
    grid_spec = pltpu.PrefetchScalarGridSpec(
        num_scalar_prefetch=1,
        grid=(DEC_BATCH,),
        in_specs=page_specs + [_const_spec(perm, 1)] + [_const_spec(a, 1) for a in cw],
        out_specs=[pl.BlockSpec((None, nseg, KV_W), lambda b, pt: (b, 0, 0))] * 3,
        scratch_shapes=[pltpu.VMEM((N_PAGES // 2, 2 * PAGE_SIZE, KV_W), bf16)])
    return pl.pallas_call(
        _compress_sample_kernel,
        grid_spec=grid_spec,
        out_shape=[out] * 3,
        compiler_params=_cparams(("arbitrary",)),
        name="compress_sample",
    )(page_table.reshape(-1), *([x] * N_PAGES), perm, *cw)


def _token_minor(kv):
    n, t = kv.shape[0], kv.shape[1]
    return kv.transpose(0, 2, 3, 4, 1).reshape(n, KV_W, t)


TQ = 256
N_SEL = SEQ // SEL_BLOCK


def _half_mask(shape, hi):
    lane = lax.broadcasted_iota(jnp.int32, shape, 1)
    return (lane >= HEAD_DIM) if hi else (lane < HEAD_DIM)


def _cmp_attn_kernel(q_ref, kd_ref, vl_ref, vh_ref, mt_ref, o_ref, selm_ref):
    i = pl.program_id(2)
    t0 = i * TQ
    ncmp = kd_ref.shape[0]
    tpos = t0 + lax.broadcasted_iota(jnp.int32, (TQ, ncmp), 0)
    nidx = lax.broadcasted_iota(jnp.int32, (TQ, ncmp), 1)
    mask = nidx * CMP_STRIDE + (CMP_BLOCK - 1) <= tpos
    kd, vl, vh = kd_ref[...], vl_ref[...], vh_ref[...]
    psum = jnp.zeros((TQ, ncmp), f32)
    for pr in range(2):
        q2 = q_ref[:, pr * LANES:(pr + 1) * LANES] * ATTN_SCALE
        acc = jnp.zeros((TQ, LANES), f32)
        for e in range(2):
            qz = jnp.where(_half_mask((TQ, LANES), e), q2, 0.0).astype(bf16)
            s = jnp.where(mask, _nt(qz, kd), NEG)
            m = jnp.max(s, axis=1, keepdims=True)
            ex = jnp.where(mask, jnp.exp(s - m), 0.0)
            l = jnp.sum(ex, axis=1, keepdims=True)
            p = ex * (1.0 / jnp.maximum(l, 1e-30))
            psum = psum + p
            acc = acc + jnp.dot(p.astype(bf16), vh if e else vl, preferred_element_type=f32)
        o_ref[:, pr * LANES:(pr + 1) * LANES] = acc
    hi, lo = _split2(psum)
    imp = _nt(mt_ref[...], hi) + _nt(mt_ref[...], lo)
    imp = imp[:N_SEL]
    j = lax.broadcasted_iota(jnp.int32, (N_SEL, TQ), 0)
    jt = (t0 + lax.broadcasted_iota(jnp.int32, (N_SEL, TQ), 1)) // SEL_BLOCK
    imp = jnp.where((j == 0) | (j == jt) | (j == jt - 1), SELECT_FORCE, imp)
    imp = jnp.where(j > jt, -SELECT_FORCE, imp)
    cnt = jnp.zeros((N_SEL, TQ), f32)
    for jp in range(N_SEL):
        row = imp[jp:jp + 1, :]
        cnt = cnt + jnp.where(j > jp, jnp.where(row >= imp, 1.0, 0.0), jnp.where(row > imp, 1.0, 0.0))
    selm = jnp.where((cnt < SEL_TOPN) & (j <= jt), 0.0, SEL_NEG)
    selm_ref[...] = jnp.concatenate([selm, selm], axis=0).T


def _cmp_to_sel_matrix_t(n_cmp_pad, n_slc, n_cmp):
    ratio = SEL_BLOCK // CMP_STRIDE
    i = np.arange(n_cmp_pad)[None, :]
    jj = np.arange(n_slc)[:, None]
    diff = i - ratio * jj
    mat = np.zeros((n_slc, n_cmp_pad), np.float32)
    for n in range(CMP_BLOCK // CMP_STRIDE):
        mat += ((diff + n >= 0) & (diff + n < ratio)).astype(np.float32)
    mat[:, n_cmp:] = 0.0
    return mat


def _cmp_attn_prompt(q, kd, vl, vh):
    ncmp = SEQ // CMP_STRIDE
    mt = np.zeros((LANES, ncmp), np.float32)
    mt[:N_SEL] = _cmp_to_sel_matrix_t(ncmp, N_SEL, ncmp - 1)
    mt = jnp.asarray(mt, bf16)
    nq = SEQ // TQ
    kv_spec = pl.BlockSpec((None, ncmp, LANES), lambda b, g, i: (b, 0, g))
    return pl.pallas_call(
        _cmp_attn_kernel,
        grid=(BATCH, N_KV, nq),
        in_specs=[pl.BlockSpec((TQ, 2 * LANES), lambda b, g, i: (b * nq + i, g)), kv_spec, kv_spec, kv_spec,
                  pl.BlockSpec(mt.shape, lambda b, g, i: (0, 0))],
        out_specs=[pl.BlockSpec((TQ, 2 * LANES), lambda b, g, i: (b * nq + i, g)),
                   pl.BlockSpec((None, None, TQ, LANES), lambda b, g, i: (b, g, i, 0))],
        out_shape=[jax.ShapeDtypeStruct((BATCH * SEQ, ATTN_W), f32),
                   jax.ShapeDtypeStruct((BATCH, N_KV, SEQ, LANES), f32)],
        compiler_params=_cparams(("parallel", "parallel", "parallel")),
        name="cmp_attn_prompt",
    )(q, kd, vl, vh, mt)


def _flash_tile(qs, k, vl, vh, mask, m_ref, acc_ref):
    s = _nt(qs, k)
    if mask is not None:
        s = jnp.where(mask, s, NEG)
    m_prev = m_ref[...]
    m_next = jnp.maximum(m_prev, jnp.max(s, axis=1, keepdims=True))
    alpha = jnp.exp(m_prev - m_next)
    p = jnp.exp(s - jnp.concatenate([m_next] * (s.shape[1] // LANES), axis=1)).astype(bf16)
    m_ref[...] = m_next
    for r in range(Q_PER_KV):
        rows = slice(r * TQ, (r + 1) * TQ)
        acc_ref[rows, :] = alpha[rows] * acc_ref[rows, :] + jnp.dot(p[rows], vh if r % 2 else vl,
                                                                   preferred_element_type=f32)


def _flash_finish(acc_ref, o_ref):
    lo = _half_mask((TQ, LANES), 0)
    for pr in range(2):
        a_e = acc_ref[(2 * pr) * TQ:(2 * pr + 1) * TQ, :]
        a_o = acc_ref[(2 * pr + 1) * TQ:(2 * pr + 2) * TQ, :]
        num = jnp.where(lo, a_e, a_o)
        den = jnp.where(lo, pltpu.roll(a_e, HEAD_DIM, axis=1), pltpu.roll(a_o, HEAD_DIM, axis=1))
        o_ref[:, pr * LANES:(pr + 1) * LANES] = num * (1.0 / den)


def _stack_queries(q_ref, qs_ref):
    for r in range(Q_PER_KV):
        q2 = q_ref[:, (r // 2) * LANES:(r // 2 + 1) * LANES] * ATTN_SCALE
        qs_ref[r * TQ:(r + 1) * TQ, 0:LANES] = jnp.where(_half_mask((TQ, LANES), r % 2), q2, 0.0).astype(bf16)


def _tile_masks():
    row = lax.broadcasted_iota(jnp.int32, (Q_PER_KV * TQ, TQ), 0) & (TQ - 1)
    col = lax.broadcasted_iota(jnp.int32, (Q_PER_KV * TQ, TQ), 1)
    return row, col


def _sel_attn_kernel(q_ref, selm_ref, kp_ref, vl_ref, vh_ref, o_ref, qs_ref, m_ref, acc_ref):
    i = pl.program_id(2)
    _stack_queries(q_ref, qs_ref)
    selm = selm_ref[...].astype(bf16)
    for r in range(Q_PER_KV):
        qs_ref[r * TQ:(r + 1) * TQ, LANES:2 * LANES] = selm
    m_ref[...] = jnp.full(m_ref.shape, NEG, f32)
    acc_ref[...] = jnp.zeros(acc_ref.shape, f32)
    qs = qs_ref[...]
    row, col = _tile_masks()

    def tile(kt, mask):
        rows = pl.ds(pl.multiple_of(kt * TQ, TQ), TQ)
        _flash_tile(qs, kp_ref[rows, :], vl_ref[rows, :], vh_ref[rows, :], mask, m_ref, acc_ref)

    tile(i, col <= row)

    def body(kt, carry):
        tile(kt, None)
        return carry

    lax.fori_loop(0, i, body, 0)
    _flash_finish(acc_ref, o_ref)


def _win_attn_kernel(q_ref, kd_ref, vl_ref, vh_ref, o_ref, qs_ref, m_ref, acc_ref):
    i = pl.program_id(2)
    _stack_queries(q_ref, qs_ref)
    m_ref[...] = jnp.full(m_ref.shape, NEG, f32)
    acc_ref[...] = jnp.zeros(acc_ref.shape, f32)
    qs = qs_ref[...]
    row, col = _tile_masks()

    def tile(kt, mask):
        rows = pl.ds(pl.multiple_of(kt * TQ, TQ), TQ)
        _flash_tile(qs, kd_ref[rows, :], vl_ref[rows, :], vh_ref[rows, :], mask, m_ref, acc_ref)

    tile(i, col <= row)

    @pl.when(i >= 1)
    def _():
        tile(i - 1, None)

    @pl.when(i >= 2)
    def _():
        tile(i - 2, col > row)

    _flash_finish(acc_ref, o_ref)


def _split_kv(kv):
    kv5 = kv.reshape(BATCH, SEQ, 2, N_KV, HEAD_DIM)
    return kv5[:, :, 0].transpose(0, 2, 1, 3), kv5[:, :, 1].transpose(0, 2, 1, 3)


def _value_pairs(v):
    ones = jnp.ones_like(v)
    return (jnp.concatenate([v, ones], axis=-1).astype(bf16), jnp.concatenate([ones, v], axis=-1).astype(bf16))


def _sel_attn_prompt(q, selm, ks):
    assert WINDOW == 2 * TQ
    k, v = _split_kv(ks)
    blk = (jnp.arange(SEQ)[:, None] // SEL_BLOCK == jnp.arange(N_SEL)[None, :]).astype(f32)
    blk = jnp.broadcast_to(blk, (BATCH, N_KV, SEQ, N_SEL))
    kp = jnp.concatenate([k, k, blk, jnp.zeros_like(blk)], axis=-1).astype(bf16)
    vl, vh = _value_pairs(v)
    nq = SEQ // TQ
    q_spec = pl.BlockSpec((TQ, 2 * LANES), lambda b, g, i: (b * nq + i, g))
    kv_spec = lambda w: pl.BlockSpec((None, None, SEQ, w), lambda b, g, i: (b, g, 0, 0))
    return pl.pallas_call(
        _sel_attn_kernel,
        grid=(BATCH, N_KV, nq),
        in_specs=[q_spec, pl.BlockSpec((None, None, TQ, LANES), lambda b, g, i: (b, g, i, 0)),
                  kv_spec(2 * LANES), kv_spec(LANES), kv_spec(LANES)],
        out_specs=q_spec,
        out_shape=jax.ShapeDtypeStruct((BATCH * SEQ, ATTN_W), f32),
        scratch_shapes=[pltpu.VMEM((Q_PER_KV * TQ, 2 * LANES), bf16), pltpu.VMEM((Q_PER_KV * TQ, LANES), f32),
                        pltpu.VMEM((Q_PER_KV * TQ, LANES), f32)],
        compiler_params=_cparams(("parallel", "parallel", "parallel")),
        name="sel_attn_prompt",
    )(q, selm, kp, vl, vh)


def _win_attn_prompt(q, kw):
    k, v = _split_kv(kw)
    kd = jnp.concatenate([k, k], axis=-1).astype(bf16)
    vl, vh = _value_pairs(v)
    nq = SEQ // TQ
    q_spec = pl.BlockSpec((TQ, 2 * LANES), lambda b, g, i: (b * nq + i, g))
    kv_spec = pl.BlockSpec((None, None, SEQ, LANES), lambda b, g, i: (b, g, 0, 0))
    return pl.pallas_call(
        _win_attn_kernel,
        grid=(BATCH, N_KV, nq),
        in_specs=[q_spec, kv_spec, kv_spec, kv_spec],
        out_specs=q_spec,
        out_shape=jax.ShapeDtypeStruct((BATCH * SEQ, ATTN_W), f32),
        scratch_shapes=[pltpu.VMEM((Q_PER_KV * TQ, LANES), bf16), pltpu.VMEM((Q_PER_KV * TQ, LANES), f32),
                        pltpu.VMEM((Q_PER_KV * TQ, LANES), f32)],
        compiler_params=_cparams(("parallel", "parallel", "parallel")),
        name="win_attn_prompt",
    )(q, kd, vl, vh)


SSD_Q = 256
HALF_INNER = D_INNER // N_SSM_GROUPS
BC_W = N_SSM_GROUPS * D_STATE


def _mamba_consts(conv_w, conv_b, dt_bias, a_log, d_skip, ssm_norm):
    pad = lambda v: jnp.zeros((1, GD_W), f32).at[0, DT_LANE0:DT_LANE0 + N_SSM_HEADS].set(v)
    e16 = np.zeros((GD_W, D_INNER), np.float32)
    for h in range(N_SSM_HEADS):
        e16[DT_LANE0 + h, h * SSM_HEAD_DIM:(h + 1) * SSM_HEAD_DIM] = 1.0
    tri = np.tril(np.ones((SSD_Q, SSD_Q), np.float32))
    return (conv_w, conv_b[None], pad(dt_bias), pad(a_log), jnp.asarray(e16, bf16),
            jnp.repeat(d_skip, SSM_HEAD_DIM)[None], ssm_norm[None], jnp.asarray(tri, bf16))


def _dt_and_decay(gd, dtb_ref, alog_ref):
    lane = lax.broadcasted_iota(jnp.int32, gd.shape, 1)
    live = (lane >= DT_LANE0) & (lane < DT_LANE0 + N_SSM_HEADS)
    dt = jnp.where(live, _softplus(gd + dtb_ref[...]), 0.0)
    return dt, dt * (-jnp.exp(alog_ref[...]))


def _gated_group_norm(y, z, nw_ref):
    y = y * _silu(z)
    outs = []
    for g in range(N_SSM_GROUPS):
        yg = y[:, g * HALF_INNER:(g + 1) * HALF_INNER]
        outs.append(yg * lax.rsqrt(jnp.mean(yg * yg, axis=-1, keepdims=True) + NORM_EPS))
    return jnp.concatenate(outs, axis=1) * nw_ref[...]


def _mamba_prompt_kernel(z_ref, xbc_ref, gd_ref, cw_ref, cb_ref, dtb_ref, alog_ref, e16_ref, dsk_ref, nw_ref, tri_ref,
                         y_ref, st_ref, xpad_ref, state_ref):
    c = pl.program_id(1)
    nq = SSD_Q

    @pl.when(c == 0)
    def _():
        state_ref[...] = jnp.zeros(state_ref.shape, f32)
        xpad_ref[0:8, :] = jnp.zeros((8, CONV_DIM), f32)

    xpad_ref[8:8 + nq, :] = xbc_ref[...]
    conv = cb_ref[...]
    for w in range(CONV_W):
        conv = conv + xpad_ref[8 - (CONV_W - 1) + w:8 - (CONV_W - 1) + w + nq, :] * cw_ref[w:w + 1, :]
    xpad_ref[0:8, :] = xpad_ref[nq:nq + 8, :]
    act = _silu(conv)
    xs, bm, cm = act[:, :D_INNER], act[:, D_INNER:D_INNER + BC_W], act[:, D_INNER + BC_W:]

    dt, a = _dt_and_decay(gd_ref[...], dtb_ref, alog_ref)
    a_cs = sum(jnp.dot(tri_ref[...], t, preferred_element_type=f32) for t in _split3(a))
    ea = jnp.exp(a_cs)
    te = jnp.exp(a_cs[nq - 1:nq, :] - a_cs)
    e16 = e16_ref[...]
    dt_x, ea_x, te_x = _dot_exact(dt, e16), _dot_exact(ea, e16), _dot_exact(te, e16)
    xdt = xs * dt_x
    xw = (xdt * te_x).astype(bf16)
    a_cst = a_cs.T
    tril = lax.broadcasted_iota(jnp.int32, (nq, nq), 1) <= lax.broadcasted_iota(jnp.int32, (nq, nq), 0)

    ys = []
    for g in range(N_SSM_GROUPS):
        bg = bm[:, g * D_STATE:(g + 1) * D_STATE]
        cgb = cm[:, g * D_STATE:(g + 1) * D_STATE].astype(bf16)
        cb = _nt(cgb, bg.astype(bf16))
        st = state_ref[g]
        yoff = jnp.dot(cgb, st.astype(bf16), preferred_element_type=f32)
        gl = slice(g * HALF_INNER, (g + 1) * HALF_INNER)
        state_ref[g] = ea_x[nq - 1:nq, gl] * st + jnp.dot(bg.T.astype(bf16), xw[:, gl], preferred_element_type=f32)
        for hp in range(HALF_INNER // LANES):
            pl_ = slice(g * HALF_INNER + hp * LANES, g * HALF_INNER + (hp + 1) * LANES)
            xp = xdt[:, pl_]
            yp = ea_x[:, pl_] * yoff[:, hp * LANES:(hp + 1) * LANES] + dsk_ref[:, pl_] * xs[:, pl_]
            for e in range(2):
                lane_h = DT_LANE0 + g * (N_SSM_HEADS // N_SSM_GROUPS) + hp * 2 + e
                seg = a_cs[:, lane_h:lane_h + 1] - a_cst[lane_h:lane_h + 1, :]
                wgt = (cb * jnp.exp(jnp.where(tril, seg, NEG))).astype(bf16)
                xz = jnp.where(_half_mask((nq, LANES), e), xp, 0.0).astype(bf16)
                yp = yp + jnp.dot(wgt, xz, preferred_element_type=f32)
            ys.append(yp)
    y = jnp.concatenate(ys, axis=1)
    y_ref[...] = _gated_group_norm(y, z_ref[...], nw_ref).astype(bf16)

    @pl.when(c == pl.num_programs(1) - 1)
    def _():
        st_ref[...] = state_ref[...]


def _mamba_prompt(z, xbc, gd, mc):
    nc = SEQ // SSD_Q
    row = lambda w: pl.BlockSpec((SSD_Q, w), lambda b, c: (b * nc + c, 0))
    y, st = pl.pallas_call(
        _mamba_prompt_kernel,
        grid=(BATCH, nc),
        in_specs=[row(D_INNER), row(CONV_DIM), row(GD_W)] + [_const_spec(a, 2) for a in mc],
        out_specs=[row(D_INNER), pl.BlockSpec((None, N_SSM_GROUPS, D_STATE, HALF_INNER), lambda b, c: (b, 0, 0, 0))],
        out_shape=[jax.ShapeDtypeStruct((BATCH * SEQ, D_INNER), bf16),
                   jax.ShapeDtypeStruct((BATCH, N_SSM_GROUPS, D_STATE, HALF_INNER), f32)],
        scratch_shapes=[pltpu.VMEM((SSD_Q + 8, CONV_DIM), f32), pltpu.VMEM((N_SSM_GROUPS, D_STATE, HALF_INNER), f32)],
        compiler_params=_cparams(("parallel", "arbitrary")),
        name="mamba_prompt",
    )(z, xbc, gd, *mc)
    hpg = N_SSM_HEADS // N_SSM_GROUPS
    st = st.reshape(BATCH, N_SSM_GROUPS, D_STATE, hpg, SSM_HEAD_DIM).transpose(0, 1, 3, 4, 2)
    return y, st.reshape(BATCH, N_SSM_HEADS, SSM_HEAD_DIM, D_STATE)


def _mamba_sample_kernel(z_ref, xbc_ref, gd_ref, sc_ref, s_ref, cw_ref, cb_ref, dtb_ref, alog_ref, e16_ref, dsk_ref,
                         nw_ref, y_ref, so_ref):
    conv = cb_ref[...] + xbc_ref[...] * cw_ref[CONV_W - 1:CONV_W, :]
    for w in range(CONV_W - 1):
        conv = conv + sc_ref[w:w + 1, :] * cw_ref[w:w + 1, :]
    act = _silu(conv)
    xs, bm, cm = act[:, :D_INNER], act[:, D_INNER:D_INNER + BC_W], act[:, D_INNER + BC_W:]
    dt, a = _dt_and_decay(jnp.broadcast_to(gd_ref[...], (8, GD_W)), dtb_ref, alog_ref)
    e16 = e16_ref[...]
    dt_x = _dot_exact(dt, e16)[0:1]
    da_x = _dot_exact(jnp.exp(a), e16)[0:1]
    rows = N_SSM_HEADS * SSM_HEAD_DIM
    xcol = jnp.broadcast_to(xs * dt_x, (D_STATE, rows)).T
    acol = jnp.broadcast_to(da_x, (D_STATE, rows)).T
    rowi = lax.broadcasted_iota(jnp.int32, (rows, D_STATE), 0)
    bfull = jnp.where(rowi < HALF_INNER, bm[:, :D_STATE], bm[:, D_STATE:])
    snew = s_ref[...] * acol + xcol * bfull
    so_ref[...] = snew
    r8 = lax.broadcasted_iota(jnp.int32, (8, D_STATE), 0)
    c8 = jnp.where(r8 == 0, cm[:, :D_STATE], jnp.where(r8 == 1, cm[:, D_STATE:], 0.0)).astype(bf16)
    yy = _nt(c8, snew.astype(bf16))
    lane = lax.broadcasted_iota(jnp.int32, (1, rows), 1)
    y = jnp.where(lane < HALF_INNER, yy[0:1], yy[1:2]) + dsk_ref[...] * xs
    y_ref[...] = _gated_group_norm(y, z_ref[...], nw_ref).astype(bf16)


def _mamba_sample(z, xbc, gd, state_conv, state_ssm, mc):
    nb = z.shape[0]
    mc = mc[:7]
    rows = N_SSM_HEADS * SSM_HEAD_DIM
    one = lambda w: pl.BlockSpec((None, 1, w), lambda b: (b, 0, 0))
    s_spec = pl.BlockSpec((None, rows, D_STATE), lambda b: (b, 0, 0))
    y, so = pl.pallas_call(
        _mamba_sample_kernel,
        grid=(nb,),
        in_specs=[one(D_INNER), one(CONV_DIM), one(GD_W),
                  pl.BlockSpec((None, CONV_W - 1, CONV_DIM), lambda b: (b, 0, 0)), s_spec]
                 + [_const_spec(a, 1) for a in mc],
        out_specs=[one(D_INNER), s_spec],
        out_shape=[jax.ShapeDtypeStruct((nb, 1, D_INNER), bf16), jax.ShapeDtypeStruct((nb, rows, D_STATE), f32)],
        compiler_params=_cparams(("parallel",)),
        name="mamba_sample",
    )(z[:, None], xbc[:, None], gd[:, None], state_conv, state_ssm.reshape(nb, rows, D_STATE), *mc)
    return y[:, 0], so.reshape(nb, N_SSM_HEADS, SSM_HEAD_DIM, D_STATE)


N_CMP_S = PAST_LEN // CMP_STRIDE - 1
N_SLC_S = PAST_LEN // SEL_BLOCK + 1
N_SLC_PAD = 2 * LANES
HALF_PAGE = PAGE_SIZE // SEL_BLOCK
GONE = -3.0e38


def _cmp_attn_sample_kernel(qz_ref, kd_ref, vl_ref, vh_ref, mts_ref, o_ref, idx_ref):
    qz = (qz_ref[...] * ATTN_SCALE).astype(bf16)
    ncmp = kd_ref.shape[0]
    nidx = lax.broadcasted_iota(jnp.int32, (N_HEADS, ncmp), 1)
    rowh = lax.broadcasted_iota(jnp.int32, (N_HEADS, ncmp), 0)
    mask = (nidx < N_CMP_S) & (nidx * CMP_STRIDE + (CMP_BLOCK - 1) <= PAST_LEN)
    rowo = lax.broadcasted_iota(jnp.int32, (N_HEADS, LANES), 0)
    r8 = lax.broadcasted_iota(jnp.int32, (8, N_SLC_PAD), 0)
    o_acc = jnp.zeros((N_HEADS, LANES), f32)
    imp = jnp.zeros((8, N_SLC_PAD), f32)
    for g in range(N_KV):
        gl = slice(g * LANES, (g + 1) * LANES)
        s = jnp.where(mask, _nt(qz, kd_ref[:, gl]), NEG)
        m = jnp.max(s, axis=1, keepdims=True)
        ex = jnp.where(mask, jnp.exp(s - m), 0.0)
        p = ex * (1.0 / jnp.maximum(jnp.sum(ex, axis=1, keepdims=True), 1e-30))
        p = jnp.where(rowh // Q_PER_KV == g, p, 0.0)
        pb = p.astype(bf16)
        o_l = jnp.dot(pb, vl_ref[:, gl], preferred_element_type=f32)
        o_h = jnp.dot(pb, vh_ref[:, gl], preferred_element_type=f32)
        o_acc = o_acc + jnp.where(rowo % 2 == 0, o_l, o_h)
        hi, lo = _split2(p)
        imp_h = jnp.dot(hi, mts_ref[...], preferred_element_type=f32) + jnp.dot(lo, mts_ref[...], preferred_element_type=f32)
        imp = imp + jnp.where(r8 == g, jnp.sum(imp_h, axis=0, keepdims=True), 0.0)
    o_ref[...] = o_acc
    j = lax.broadcasted_iota(jnp.int32, (8, N_SLC_PAD), 1)
    jt = PAST_LEN // SEL_BLOCK
    imp = jnp.where((j == 0) | (j == jt) | (j == jt - 1), SELECT_FORCE, imp)
    imp = jnp.where(j > jt, -SELECT_FORCE, imp)
    imp = jnp.where(j >= N_SLC_S, NEG, imp)
    jf = j.astype(f32)
    lane = lax.broadcasted_iota(jnp.int32, (8, LANES), 1)
    picked = jnp.zeros((8, LANES), f32)
    for k in range(SEL_TOPN):
        m = jnp.max(imp, axis=1, keepdims=True)
        ix = jnp.min(jnp.where(imp == m, jf, float(N_SLC_PAD)), axis=1, keepdims=True)
        picked = jnp.where(lane == k, ix, picked)
        imp = jnp.where(jf == ix, GONE, imp)
    idx_ref[...] = picked.astype(jnp.int32)


def _cmp_attn_sample(qz, kd, vl, vh):
    nb, ncmp = kd.shape[0], kd.shape[1]
    mts = np.zeros((ncmp, N_SLC_PAD), np.float32)
    mts[:, :N_SLC_S] = _cmp_to_sel_matrix_t(ncmp, N_SLC_S, N_CMP_S).T
    mts = jnp.asarray(mts, bf16)
    kv_spec = pl.BlockSpec((None, ncmp, KV_W), lambda b: (b, 0, 0))
    return pl.pallas_call(
        _cmp_attn_sample_kernel,
        grid=(nb,),
        in_specs=[pl.BlockSpec((None, N_HEADS, LANES), lambda b: (b, 0, 0)), kv_spec, kv_spec, kv_spec,
                  pl.BlockSpec(mts.shape, lambda b: (0, 0))],
        out_specs=[pl.BlockSpec((None, N_HEADS, LANES), lambda b: (b, 0, 0)),
                   pl.BlockSpec((None, 8, LANES), lambda b: (b, 0, 0))],
        out_shape=[jax.ShapeDtypeStruct((nb, N_HEADS, LANES), f32), jax.ShapeDtypeStruct((nb, 8, LANES), jnp.int32)],
        compiler_params=_cparams(("parallel",)),
        name="cmp_attn_sample",
    )(qz, kd, vl, vh, mts)


def _one_query_softmax(s, s_new, vt, v_new):
    m = jnp.maximum(jnp.max(s, axis=1, keepdims=True), s_new)
    p = jnp.exp(s - m)
    p_new = jnp.exp(s_new - m)
    den = jnp.sum(p, axis=1, keepdims=True) + p_new
    num = _nt(p.astype(bf16), vt) + p_new * v_new
    return num * (1.0 / den)


def _selwin_sample_kernel(idx_ref, pt_ref, *refs):
    del pt_ref
    kb, vb = refs[:SEL_TOPN], refs[SEL_TOPN:2 * SEL_TOPN]
    q_ref, ksn_ref, wk_ref, wv_ref, kwn_ref, os_ref, ow_ref = refs[2 * SEL_TOPN:]
    b, g = pl.program_id(0), pl.program_id(1)
    q8 = q_ref[...] * ATTN_SCALE
    qb = q8.astype(bf16)
    kcat = jnp.concatenate([r[...] for r in kb], axis=1).astype(bf16)
    vcat = jnp.concatenate([r[...] for r in vb], axis=1).astype(bf16)
    n = SEL_TOPN * PAGE_SIZE
    lane = lax.broadcasted_iota(jnp.int32, (8, n), 1)
    slot = lane // PAGE_SIZE
    blk = jnp.zeros((8, n), jnp.int32)
    for k in range(SEL_TOPN):
        blk = jnp.where(slot == k, idx_ref[(b * N_KV + g) * SEL_TOPN + k], blk)
    live = (blk < N_SLC_S - 1) & ((lane % PAGE_SIZE) // SEL_BLOCK == blk % HALF_PAGE)
    s = jnp.where(live, jnp.dot(qb, kcat, preferred_element_type=f32), NEG)
    k_new, v_new = ksn_ref[pl.ds(g, 1), :], ksn_ref[pl.ds(N_KV + g, 1), :]
    s_new = jnp.sum(q8 * k_new, axis=1, keepdims=True)
    os_ref[...] = _one_query_softmax(s, s_new, vcat, v_new)
    nbuf = wk_ref.shape[1]
    i = lax.broadcasted_iota(jnp.int32, (8, nbuf), 1)
    s = jnp.where(nbuf - i < WINDOW, jnp.dot(qb, wk_ref[...].astype(bf16), preferred_element_type=f32), NEG)
    k_new, v_new = kwn_ref[pl.ds(g, 1), :], kwn_ref[pl.ds(N_KV + g, 1), :]
    s_new = jnp.sum(q8 * k_new, axis=1, keepdims=True)
    ow_ref[...] = _one_query_softmax(s, s_new, wv_ref[...].astype(bf16), v_new)


def _selwin_sample(idx, page_table, cache_s, q8, ks_new, win_buf, kw_new):
    nb = q8.shape[0]
    nbuf = win_buf.shape[1]
    pages = _token_minor(cache_s)
    wb = _token_minor(win_buf)

    def blk_map(b, g, idx_ref, pt_ref, k, c):
        j = jnp.minimum(idx_ref[(b * N_KV + g) * SEL_TOPN + k], N_SLC_S - 2)
        return (pt_ref[b * N_PAGES + j // HALF_PAGE], c * N_KV + g, 0)

    blk_specs = [pl.BlockSpec((None, HEAD_DIM, PAGE_SIZE), functools.partial(blk_map, k=k, c=c))
                 for c in range(2) for k in range(SEL_TOPN)]
    new_spec = pl.BlockSpec((None, 2 * N_KV, HEAD_DIM), lambda b, g, *_: (b, 0, 0))
    win_spec = lambda c: pl.BlockSpec((None, HEAD_DIM, nbuf), lambda b, g, *_: (b, c * N_KV + g, 0))
    o_spec = pl.BlockSpec((None, None, 8, HEAD_DIM), lambda b, g, *_: (b, g, 0, 0))
    grid_spec = pltpu.PrefetchScalarGridSpec(
        num_scalar_prefetch=2,
        grid=(nb, N_KV),
        in_specs=blk_specs + [o_spec, new_spec, win_spec(0), win_spec(1), new_spec],
        out_specs=[o_spec, o_spec])
    o_shape = jax.ShapeDtypeStruct((nb, N_KV, 8, HEAD_DIM), f32)
    ks3, kw3 = ks_new.reshape(nb, 2 * N_KV, HEAD_DIM), kw_new.reshape(nb, 2 * N_KV, HEAD_DIM)
    os8, ow8 = pl.pallas_call(
        _selwin_sample_kernel,
        grid_spec=grid_spec,
        out_shape=[o_shape, o_shape],
        compiler_params=_cparams(("arbitrary", "arbitrary")),
        name="selwin_sample",
    )(idx.reshape(-1), page_table.reshape(-1), *([pages] * (2 * SEL_TOPN)), q8, ks3, wb, wb, kw3)
    return os8[:, :, :Q_PER_KV].reshape(nb, ATTN_W), ow8[:, :, :Q_PER_KV].reshape(nb, ATTN_W)


def _sample_query_layouts(q):
    nb = q.shape[0]
    q16 = q.reshape(nb, N_HEADS, HEAD_DIM)
    z = jnp.zeros_like(q16)
    lo, hi = jnp.concatenate([q16, z], axis=-1), jnp.concatenate([z, q16], axis=-1)
    h = jnp.arange(N_HEADS)[None, :, None]
    qz = jnp.where(h % 2 == 0, lo, hi)
    qg = q16.reshape(nb, N_KV, Q_PER_KV, HEAD_DIM)
    q8 = jnp.concatenate([qg, jnp.zeros_like(qg)], axis=2)
    return qz, q8


def _unpad_heads_by_parity(o16):
    nb = o16.shape[0]
    o = o16.reshape(nb, N_HEADS // 2, 2, 2, HEAD_DIM)
    return jnp.stack([o[:, :, 0, 0], o[:, :, 1, 1]], axis=2).reshape(nb, ATTN_W)


def _finish(x, ocmp, osel, owin, gd, ssd_y, w_out, ln_mlp, w_up, w_down, ln_final, tm, tm_mlp, tf):
    x1 = _outproj(x, ocmp, osel, owin, gd, ssd_y, _gate_expand(), w_out, tm)
    return _mlp(x1, ln_mlp, w_up, w_down, ln_final, tm_mlp, tf)


def kernel(x_prompt, x_sample, cache_kv_cmp, cache_kv_sel, state_kv_win, state_conv, state_ssm, page_table, ln_mix, w_in, cmp_pe, cmp_w1, cmp_b1, cmp_w2, cmp_b2, conv_w, conv_b, dt_bias, a_log, d_skip, ssm_norm, w_out, ln_mlp, w_up, w_down, ln_final):
    nb = x_sample.shape[0]
    w_in_p = _prep_w_in(w_in[0])
    w_out_b, w_up_b, w_down_b = w_out[0].astype(bf16), w_up[0].astype(bf16), w_down[0].astype(bf16)
    lnw, lnm, lnf = ln_mix[0][None], ln_mlp[0][None], ln_final[None]
    cw = _compress_weights(cmp_pe[0], cmp_w1[0], cmp_b1[0], cmp_w2[0], cmp_b2[0])
    mc = _mamba_consts(conv_w[0], conv_b[0], dt_bias[0], a_log[0], d_skip[0], ssm_norm[0])
    kv_shape = (2, N_KV, HEAD_DIM)

    xp = x_prompt.reshape(BATCH * SEQ, D_MODEL)
    q, z, xbc, kc, ks, kw, gd = _inproj(xp, lnw, w_in_p, 256)
    kd, vl, vh = _compress_prompt(kc, cw)
    ocmp, selm = _cmp_attn_prompt(q, kd, vl, vh)
    osel = _sel_attn_prompt(q, selm, ks)
    owin = _win_attn_prompt(q, kw)
    ssd_y, ssm_p = _mamba_prompt(z, xbc, gd, mc)
    y_prompt = _finish(xp, ocmp, osel, owin, gd, ssd_y, w_out_b, lnm, w_up_b, w_down_b, lnf, 256, 512, 1024)

    xs = x_sample.reshape(nb, D_MODEL)
    q_s, z_s, xbc_s, kc_s, ks_s, kw_s, gd_s = _inproj(xs, lnw, w_in_p, nb)
    kd_s, vl_s, vh_s = _compress_sample(cache_kv_cmp[0], page_table, cw)
    qz, q8 = _sample_query_layouts(q_s)
    ocmp16, idx8 = _cmp_attn_sample(qz, kd_s, vl_s, vh_s)
    idx = idx8[:, :N_KV, :SEL_TOPN]
    osel_s, owin_s = _selwin_sample(idx, page_table, cache_kv_sel[0], q8, ks_s, state_kv_win[0], kw_s)
    ssd_y_s, ssm_s = _mamba_sample(z_s, xbc_s, gd_s, state_conv[0], state_ssm[0], mc)
    y_sample = _finish(xs, _unpad_heads_by_parity(ocmp16), osel_s, owin_s, gd_s, ssd_y_s,
                       w_out_b, lnm, w_up_b, w_down_b, lnf, nb, nb, 1024)

    kv_win_p = kw.reshape(BATCH, SEQ, *kv_shape)[:, -min(WINDOW, SEQ):]
    conv_p = xbc.reshape(BATCH, SEQ, CONV_DIM)[:, -(CONV_W - 1):]
    kv_win_s = jnp.concatenate([state_kv_win[0], kw_s.reshape(nb, 1, *kv_shape)], axis=1)[:, -WINDOW:]
    conv_s = jnp.concatenate([state_conv[0], xbc_s[:, None]], axis=1)[:, -(CONV_W - 1):]
    return (y_prompt.reshape(BATCH, SEQ, D_MODEL), y_sample.reshape(nb, 1, D_MODEL),
            kc.reshape(1, BATCH, SEQ, *kv_shape), ks.reshape(1, BATCH, SEQ, *kv_shape), kv_win_p[None],
            conv_p[None], ssm_p[None],
            kc_s.reshape(1, nb, 1, *kv_shape), ks_s.reshape(1, nb, 1, *kv_shape), kv_win_s[None],
            conv_s[None], ssm_s[None])
```

```python
import functools
import math

import jax
import jax.numpy as jnp
import numpy as np
from jax import lax
from jax.experimental import pallas as pl
from jax.experimental.pallas import tpu as pltpu

f32 = jnp.float32
bf16 = jnp.bfloat16

D_MODEL = 2048
BATCH = 2
SEQ = 4096
DEC_BATCH = 32
PAST_LEN = 8192
PAGE_SIZE = 128
HEAD_DIM = 64
N_HEADS = 16
N_KV = 4
Q_PER_KV = 4
ATTN_W = 1024
KV_W = 512
CMP_BLOCK = 32
CMP_STRIDE = 16
CMP_HIDDEN = 128
SEL_BLOCK = 64
SEL_TOPN = 16
WINDOW = 512
SELECT_FORCE = 1.0e4
ATTN_SCALE = HEAD_DIM ** -0.5
D_INNER = 1024
SSM_HEAD_DIM = 64
N_SSM_HEADS = 16
N_SSM_GROUPS = 2
D_STATE = 128
CONV_W = 4
CONV_DIM = D_INNER + 2 * N_SSM_GROUPS * D_STATE
D_FF = 4 * D_MODEL
IN_SPLITS = (ATTN_W, KV_W, KV_W, KV_W, 3 * N_HEADS, D_INNER, CONV_DIM, N_SSM_HEADS)
NORM_EPS = 1e-5

LANES = 128
GD_W = LANES
DT_LANE0 = 3 * N_HEADS
N_PAGES = PAST_LEN // PAGE_SIZE
SEG_PER_PAGE = PAGE_SIZE // CMP_STRIDE
SEG_W = CMP_STRIDE * KV_W
NEG = -1.0e30
SEL_NEG = -1.0e9
VMEM_LIMIT = 56 * 1024 * 1024


def _cparams(sem):
    return pltpu.CompilerParams(dimension_semantics=sem, vmem_limit_bytes=VMEM_LIMIT)


def _nt(a, b):
    return lax.dot_general(a, b, (((1,), (1,)), ((), ())), preferred_element_type=f32)


def _split2(x):
    hi = x.astype(bf16)
    lo = (x - hi.astype(f32)).astype(bf16)
    return hi, lo


def _split3(x):
    hi = x.astype(bf16)
    r = x - hi.astype(f32)
    mid = r.astype(bf16)
    lo = (r - mid.astype(f32)).astype(bf16)
    return hi, mid, lo


def _dot_exact(x, w):
    return sum(jnp.dot(t, w, preferred_element_type=f32) for t in _split3(x))


def _silu(x):
    return x * (1.0 / (1.0 + jnp.exp(-x)))


def _sigmoid(x):
    return 1.0 / (1.0 + jnp.exp(-x))


def _softplus(x):
    return jnp.maximum(x, 0.0) + jnp.log1p(jnp.exp(-jnp.abs(x)))


IN_OUT_WIDTHS = (ATTN_W, D_INNER, CONV_DIM, GD_W, KV_W, KV_W, KV_W)
KD_W = N_KV * LANES
IN_COLS_ROWMAJOR = sum(IN_OUT_WIDTHS)


def _normed(x_ref, lnw_ref):
    x = x_ref[...]
    return (x * lax.rsqrt(jnp.mean(x * x, axis=-1, keepdims=True) + NORM_EPS) * lnw_ref[...]).astype(bf16)


def _inproj_kernel(x_ref, lnw_ref, w_ref, *out_refs):
    h = _normed(x_ref, lnw_ref)
    off = 0
    for ref in out_refs:
        n = ref.shape[-1]
        ref[...] = jnp.dot(h, w_ref[:, off:off + n], preferred_element_type=f32)
        off += n


def _inproj_prompt_kernel(x_ref, lnw_ref, w_ref, q_ref, z_ref, xbc_ref, gd_ref, kc_ref,
                          kct_ref, kst_ref, kwt_ref, kds_ref, kdw_ref, t_ref):
    h = _normed(x_ref, lnw_ref)
    off = 0

    def proj(n):
        nonlocal off
        y = jnp.dot(h, w_ref[:, off:off + n], preferred_element_type=f32)
        off += n
        return y

    for ref in (q_ref, z_ref, xbc_ref, gd_ref):
        ref[...] = proj(ref.shape[-1])
    kc_ref[...] = proj(KV_W)
    kct_ref[...] = kc_ref[...].T
    for t_out in (kst_ref, kwt_ref):
        t_ref[...] = proj(KV_W)
        t_out[...] = t_ref[...].T
    kds_ref[...] = proj(KD_W).astype(bf16)
    kdw_ref[...] = proj(KD_W).astype(bf16)


def _inproj(x, lnw, w_perm, tm):
    m = x.shape[0]
    return pl.pallas_call(
        _inproj_kernel,
        grid=(m // tm,),
        in_specs=[pl.BlockSpec((tm, D_MODEL), lambda i: (i, 0)),
                  pl.BlockSpec((1, D_MODEL), lambda i: (0, 0)),
                  pl.BlockSpec((D_MODEL, IN_COLS_ROWMAJOR), lambda i: (0, 0), pipeline_mode=pl.Buffered(1))],
        out_specs=[pl.BlockSpec((tm, n), lambda i: (i, 0)) for n in IN_OUT_WIDTHS],
        out_shape=[jax.ShapeDtypeStruct((m, n), f32) for n in IN_OUT_WIDTHS],
        compiler_params=_cparams(("parallel",)),
        name="inproj",
    )(x, lnw, w_perm)


def _inproj_prompt(x, lnw, w_perm, tm):
    nq = SEQ // tm
    row = lambda n: pl.BlockSpec((tm, n), lambda i: (i, 0))
    tmin = pl.BlockSpec((None, KV_W, tm), lambda i: (i // nq, 0, i % nq))
    rm = lambda n, dt=f32: jax.ShapeDtypeStruct((BATCH * SEQ, n), dt)
    tshape = jax.ShapeDtypeStruct((BATCH, KV_W, SEQ), f32)
    return pl.pallas_call(
        _inproj_prompt_kernel,
        grid=(BATCH * nq,),
        in_specs=[pl.BlockSpec((tm, D_MODEL), lambda i: (i, 0)),
                  pl.BlockSpec((1, D_MODEL), lambda i: (0, 0)),
                  pl.BlockSpec(w_perm.shape, lambda i: (0, 0), pipeline_mode=pl.Buffered(1))],
        out_specs=[row(ATTN_W), row(D_INNER), row(CONV_DIM), row(GD_W), row(KV_W), tmin, tmin, tmin, row(KD_W), row(KD_W)],
        out_shape=[rm(ATTN_W), rm(D_INNER), rm(CONV_DIM), rm(GD_W), rm(KV_W), tshape, tshape, tshape,
                   rm(KD_W, bf16), rm(KD_W, bf16)],
        scratch_shapes=[pltpu.VMEM((tm, KV_W), f32)],
        compiler_params=_cparams(("parallel",)),
        name="inproj_prompt",
    )(x, lnw, w_perm)


def _prep_w_in(w_in):
    parts, off = [], 0
    for width in IN_SPLITS:
        parts.append(w_in[:, off:off + width])
        off += width
    q, kc, ks, kw, g, z, xbc, dt = parts
    gd = jnp.concatenate([g, dt, jnp.zeros((D_MODEL, GD_W - 4 * N_HEADS), w_in.dtype)], axis=1)

    def dup_keys(kv):
        k = kv[:, :KV_W // 2].reshape(D_MODEL, N_KV, 1, HEAD_DIM)
        return jnp.broadcast_to(k, (D_MODEL, N_KV, 2, HEAD_DIM)).reshape(D_MODEL, KD_W)

    return jnp.concatenate([q, z, xbc, gd, kc, ks, kw, dup_keys(ks), dup_keys(kw)], axis=1).astype(bf16)


def _outproj_kernel(x_ref, oc_ref, os_ref, ow_ref, gd_ref, y_ref, eg_ref, w_ref, o_ref):
    gates = _sigmoid(gd_ref[...])
    hi, lo = _split2(gates)
    attn = None
    for c, br in enumerate((oc_ref, os_ref, ow_ref)):
        ge = (jnp.dot(hi, eg_ref[c], preferred_element_type=f32)
              + jnp.dot(lo, eg_ref[c], preferred_element_type=f32))
        term = ge * br[...]
        attn = term if attn is None else attn + term
    mix = (jnp.dot(attn.astype(bf16), w_ref[:ATTN_W, :], preferred_element_type=f32)
           + jnp.dot(y_ref[...], w_ref[ATTN_W:, :], preferred_element_type=f32))
    o_ref[...] = x_ref[...] + mix


def _outproj(x, ocmp, osel, owin, gd, ssd_y, eg, w_out, tm):
    m = x.shape[0]
    row = lambda w: pl.BlockSpec((tm, w), lambda i: (i, 0))
    return pl.pallas_call(
        _outproj_kernel,
        grid=(m // tm,),
        in_specs=[row(D_MODEL), row(ATTN_W), row(ATTN_W), row(ATTN_W), row(GD_W), row(D_INNER),
                  pl.BlockSpec((3, GD_W, ATTN_W), lambda i: (0, 0, 0)),
                  pl.BlockSpec((ATTN_W + D_INNER, D_MODEL), lambda i: (0, 0))],
        out_specs=row(D_MODEL),
        out_shape=jax.ShapeDtypeStruct((m, D_MODEL), f32),
        compiler_params=_cparams(("parallel",)),
        name="outproj",
    )(x, ocmp, osel, owin, gd, ssd_y, eg, w_out)


def _gate_expand():
    eg = np.zeros((3, GD_W, ATTN_W), np.float32)
    for h in range(N_HEADS):
        for c in range(3):
            eg[c, h * 3 + c, h * HEAD_DIM:(h + 1) * HEAD_DIM] = 1.0
    return jnp.asarray(eg, bf16)


def _mlp_kernel(x_ref, ln_ref, wu_ref, wd_ref, lnf_ref, o_ref, h_ref, acc_ref):
    k = pl.program_id(1)

    @pl.when(k == 0)
    def _():
        x = x_ref[...]
        h = x * lax.rsqrt(jnp.mean(x * x, axis=-1, keepdims=True) + NORM_EPS) * ln_ref[...]
        h_ref[...] = h.astype(bf16)
        acc_ref[...] = jnp.zeros_like(acc_ref)

    u = jnp.dot(h_ref[...], wu_ref[...], preferred_element_type=f32)
    u = jnp.square(jnp.maximum(u, 0.0)).astype(bf16)
    acc_ref[...] += jnp.dot(u, wd_ref[...], preferred_element_type=f32)

    @pl.when(k == pl.num_programs(1) - 1)
    def _():
        y = x_ref[...] + acc_ref[...]
        y = y * lax.rsqrt(jnp.mean(y * y, axis=-1, keepdims=True) + NORM_EPS) * lnf_ref[...]
        o_ref[...] = y


def _mlp(x, ln_mlp, w_up, w_down, ln_final, tm, tf):
    m = x.shape[0]
    return pl.pallas_call(
        _mlp_kernel,
        grid=(m // tm, D_FF // tf),
        in_specs=[pl.BlockSpec((tm, D_MODEL), lambda i, k: (i, 0)),
                  pl.BlockSpec((1, D_MODEL), lambda i, k: (0, 0)),
                  pl.BlockSpec((D_MODEL, tf), lambda i, k: (0, k)),
                  pl.BlockSpec((tf, D_MODEL), lambda i, k: (k, 0)),
                  pl.BlockSpec((1, D_MODEL), lambda i, k: (0, 0))],
        out_specs=pl.BlockSpec((tm, D_MODEL), lambda i, k: (i, 0)),
        out_shape=jax.ShapeDtypeStruct((m, D_MODEL), f32),
        scratch_shapes=[pltpu.VMEM((tm, D_MODEL), bf16), pltpu.VMEM((tm, D_MODEL), f32)],
        compiler_params=_cparams(("parallel", "arbitrary")),
        name="mlp",
    )(x, ln_mlp, w_up, w_down, ln_final)


N_SLAB_K = CMP_STRIDE * LANES


def _compress_weights(cmp_pe, cmp_w1, cmp_b1, cmp_w2, cmp_b2):
    eye2 = jnp.eye(2, dtype=f32)
    w1 = cmp_w1.reshape(2, 2, CMP_STRIDE, HEAD_DIM, CMP_HIDDEN)
    w1cat = jnp.einsum('crjdf,ab->cjadrbf', w1, eye2).reshape(2, N_SLAB_K, 4 * CMP_HIDDEN).astype(bf16)
    w1flat = cmp_w1.reshape(2, CMP_BLOCK * HEAD_DIM, CMP_HIDDEN).astype(bf16)
    pe8 = jnp.broadcast_to(cmp_pe.reshape(2, 1, CMP_BLOCK * HEAD_DIM), (2, 8, CMP_BLOCK * HEAD_DIM))
    b1 = cmp_b1.reshape(2, 1, CMP_HIDDEN)
    wk, wv = cmp_w2[0], cmp_w2[1]
    zk = jnp.zeros_like(wk)
    w2k = jnp.einsum('fd,ab,e->afbed', wk, eye2, jnp.ones((2,), f32)).reshape(2 * CMP_HIDDEN, 4 * HEAD_DIM)
    b2k = jnp.tile(cmp_b2[0], 4)[None]
    vl = jnp.einsum('fd,ab,e->afbed', wv, eye2, jnp.array([1.0, 0.0], f32)).reshape(2 * CMP_HIDDEN, 4 * HEAD_DIM)
    vh = jnp.einsum('fd,ab,e->afbed', wv, eye2, jnp.array([0.0, 1.0], f32)).reshape(2 * CMP_HIDDEN, 4 * HEAD_DIM)
    w2v = jnp.concatenate([vl, vh], axis=1)
    zb = jnp.zeros((HEAD_DIM,), f32)
    b2v = jnp.concatenate([cmp_b2[1], zb, cmp_b2[1], zb, zb, cmp_b2[1], zb, cmp_b2[1]])[None]
    del zk
    return pe8, w1flat, b1, w1cat, w2k.astype(bf16), b2k, w2v.astype(bf16), b2v


def _compress_slab(a, c, gp, prm_refs, out_refs):
    pe_ref, w1f_ref, b1_ref, w1_ref, w2k_ref, b2k_ref, w2v_ref, b2v_ref = prm_refs
    kd_ref, vl_ref, vh_ref = out_refs
    pet = jnp.dot(pe_ref[c].astype(bf16), w1f_ref[c], preferred_element_type=f32)[0:1] + b1_ref[c]
    bias = jnp.concatenate([pet, pet], axis=1)
    p = jnp.dot(a, w1_ref[c], preferred_element_type=f32)
    s = p.shape[0]
    hid = p[:, :2 * CMP_HIDDEN] + pltpu.roll(p[:, 2 * CMP_HIDDEN:], s - 1, axis=0)
    hid = _silu(hid + bias).astype(bf16)
    w = 4 * HEAD_DIM
    if c == 0:
        kd_ref[:, gp * w:(gp + 1) * w] = (jnp.dot(hid, w2k_ref[...], preferred_element_type=f32)
                                          + b2k_ref[...]).astype(bf16)
    else:
        o = jnp.dot(hid, w2v_ref[...], preferred_element_type=f32) + b2v_ref[...]
        vl_ref[:, gp * w:(gp + 1) * w] = o[:, :w].astype(bf16)
        vh_ref[:, gp * w:(gp + 1) * w] = o[:, w:].astype(bf16)


def _slab_lane0(j, c, gp):
    return j * KV_W + c * (KV_W // 2) + gp * LANES


def _compress_all_slabs(x_ref, prm_refs, out_refs):
    for c in range(2):
        for gp in range(2):
            a = jnp.concatenate(
                [x_ref[:, _slab_lane0(j, c, gp):_slab_lane0(j, c, gp) + LANES] for j in range(CMP_STRIDE)],
                axis=1).astype(bf16)
            _compress_slab(a, c, gp, prm_refs, out_refs)


def _compress_prompt_kernel(x_ref, *refs):
    _compress_all_slabs(x_ref, refs[:8], refs[8:11])


def _seg_perm():
    m = np.zeros((2 * PAGE_SIZE, 2 * PAGE_SIZE), np.float32)
    for pp in range(2):
        for s in range(SEG_PER_PAGE):
            for j in range(CMP_STRIDE):
                m[j * 2 * SEG_PER_PAGE + pp * SEG_PER_PAGE + s, pp * PAGE_SIZE + s * CMP_STRIDE + j] = 1.0
    return jnp.asarray(m, bf16)


def _compress_sample_kernel(pt_ref, *refs):
    del pt_ref
    pages, perm_ref = refs[:N_PAGES], refs[N_PAGES]
    prm_refs, out_refs, o_ref = refs[N_PAGES + 1:N_PAGES + 9], refs[N_PAGES + 9:N_PAGES + 12], refs[-1]
    rows = 2 * SEG_PER_PAGE
    for k in range(N_PAGES // 2):
        two = jnp.concatenate([pages[2 * k][...], pages[2 * k + 1][...]], axis=1).astype(bf16)
        o_ref[k] = _nt(perm_ref[...], two).astype(bf16)
    nseg = N_PAGES * SEG_PER_PAGE
    for c in range(2):
        for gp in range(2):
            l0 = c * (KV_W // 2) + gp * LANES
            a = jnp.concatenate([o_ref[:, j * rows:(j + 1) * rows, l0:l0 + LANES].reshape(nseg, LANES)
                                 for j in range(CMP_STRIDE)], axis=1)
            _compress_slab(a, c, gp, prm_refs, out_refs)


def _const_spec(a, n_grid, n_prefetch=0):
    zeros = (0,) * a.ndim
    if n_grid == 1:
        return pl.BlockSpec(a.shape, (lambda b, *_: zeros))
    return pl.BlockSpec(a.shape, (lambda b, c, *_: zeros))


def _compress_prompt(kc, cw):
    nseg = SEQ // CMP_STRIDE
    x = kc.reshape(BATCH, nseg, SEG_W)
    out = jax.ShapeDtypeStruct((BATCH, nseg, KV_W), bf16)
    return pl.pallas_call(
        _compress_prompt_kernel,
        grid=(BATCH,),
        in_specs=[pl.BlockSpec((None, nseg, SEG_W), lambda b: (b, 0, 0))] + [_const_spec(a, 1) for a in cw],
        out_specs=[pl.BlockSpec((None, nseg, KV_W), lambda b: (b, 0, 0))] * 3,
        out_shape=[out] * 3,
        compiler_params=_cparams(("parallel",)),
        name="compress_prompt",
    )(x, *cw)


def _compress_sample(cache, page_table, cw):
    nseg = PAST_LEN // CMP_STRIDE
    x = _token_minor(cache)
    perm = _seg_perm()
    out = jax.ShapeDtypeStruct((DEC_BATCH, nseg, KV_W), bf16)
    page_specs = [pl.BlockSpec((None, KV_W, PAGE_SIZE), functools.partial(lambda b, pt, p: (pt[b * N_PAGES + p], 0, 0), p=p))
                  for p in range(N_PAGES)]
    grid_spec = pltpu.PrefetchScalarGridSpec(
        num_scalar_prefetch=1,
        grid=(DEC_BATCH,),
        in_specs=page_specs + [_const_spec(perm, 1)] + [_const_spec(a, 1) for a in cw],
        out_specs=[pl.BlockSpec((None, nseg, KV_W), lambda b, pt: (b, 0, 0))] * 3,
        scratch_shapes=[pltpu.VMEM((N_PAGES // 2, 2 * PAGE_SIZE, KV_W), bf16)])
    return pl.pallas_call(
        _compress_sample_kernel,
        grid_spec=grid_spec,
        out_shape=[out] * 3,
        compiler_params=_cparams(("arbitrary",)),
        name="compress_sample",
    )(page_table.reshape(-1), *([x] * N_PAGES), perm, *cw)


def _token_minor(kv):
    n, t = kv.shape[0], kv.shape[1]
    return kv.transpose(0, 2, 3, 4, 1).reshape(n, KV_W, t)


TQ = 256
N_SEL = SEQ // SEL_BLOCK


def _half_mask(shape, hi):
    lane = lax.broadcasted_iota(jnp.int32, shape, 1)
    return (lane >= HEAD_DIM) if hi else (lane < HEAD_DIM)


def _cmp_attn_kernel(q_ref, kd_ref, vl_ref, vh_ref, mt_ref, o_ref, selm_ref):
    i = pl.program_id(2)
    t0 = i * TQ
    ncmp = kd_ref.shape[0]
    tpos = t0 + lax.broadcasted_iota(jnp.int32, (TQ, ncmp), 0)
    nidx = lax.broadcasted_iota(jnp.int32, (TQ, ncmp), 1)
    mask = nidx * CMP_STRIDE + (CMP_BLOCK - 1) <= tpos
    kd, vl, vh = kd_ref[...], vl_ref[...], vh_ref[...]
    psum = jnp.zeros((TQ, ncmp), f32)
    for pr in range(2):
        q2 = q_ref[:, pr * LANES:(pr + 1) * LANES] * ATTN_SCALE
        acc = jnp.zeros((TQ, LANES), f32)
        for e in range(2):
            qz = jnp.where(_half_mask((TQ, LANES), e), q2, 0.0).astype(bf16)
            s = jnp.where(mask, _nt(qz, kd), NEG)
            m = jnp.max(s, axis=1, keepdims=True)
            ex = jnp.where(mask, jnp.exp(s - m), 0.0)
            l = jnp.sum(ex, axis=1, keepdims=True)
            p = ex * (1.0 / jnp.maximum(l, 1e-30))
            psum = psum + p
            acc = acc + jnp.dot(p.astype(bf16), vh if e else vl, preferred_element_type=f32)
        o_ref[:, pr * LANES:(pr + 1) * LANES] = acc
    hi, lo = _split2(psum)
    imp = _nt(mt_ref[...], hi) + _nt(mt_ref[...], lo)
    imp = imp[:N_SEL]
    j = lax.broadcasted_iota(jnp.int32, (N_SEL, TQ), 0)
    jt = (t0 + lax.broadcasted_iota(jnp.int32, (N_SEL, TQ), 1)) // SEL_BLOCK
    imp = jnp.where((j == 0) | (j == jt) | (j == jt - 1), SELECT_FORCE, imp)
    imp = jnp.where(j > jt, -SELECT_FORCE, imp)
    cnt = jnp.zeros((N_SEL, TQ), f32)
    for jp in range(N_SEL):
        row = imp[jp:jp + 1, :]
        cnt = cnt + jnp.where(j > jp, jnp.where(row >= imp, 1.0, 0.0), jnp.where(row > imp, 1.0, 0.0))
    selm = jnp.where((cnt < SEL_TOPN) & (j <= jt), 0.0, SEL_NEG)
    selm_ref[...] = jnp.concatenate([selm, selm], axis=0).T


def _cmp_to_sel_matrix_t(n_cmp_pad, n_slc, n_cmp):
    ratio = SEL_BLOCK // CMP_STRIDE
    i = np.arange(n_cmp_pad)[None, :]
    jj = np.arange(n_slc)[:, None]
    diff = i - ratio * jj
    mat = np.zeros((n_slc, n_cmp_pad), np.float32)
    for n in range(CMP_BLOCK // CMP_STRIDE):
        mat += ((diff + n >= 0) & (diff + n < ratio)).astype(np.float32)
    mat[:, n_cmp:] = 0.0
    return mat


def _cmp_attn_prompt(q, kd, vl, vh):
    ncmp = SEQ // CMP_STRIDE
    mt = np.zeros((LANES, ncmp), np.float32)
    mt[:N_SEL] = _cmp_to_sel_matrix_t(ncmp, N_SEL, ncmp - 1)
    mt = jnp.asarray(mt, bf16)
    nq = SEQ // TQ
    kv_spec = pl.BlockSpec((None, ncmp, LANES), lambda b, g, i: (b, 0, g))
    return pl.pallas_call(
        _cmp_attn_kernel,
        grid=(BATCH, N_KV, nq),
        in_specs=[pl.BlockSpec((TQ, 2 * LANES), lambda b, g, i: (b * nq + i, g)), kv_spec, kv_spec, kv_spec,
                  pl.BlockSpec(mt.shape, lambda b, g, i: (0, 0))],
        out_specs=[pl.BlockSpec((TQ, 2 * LANES), lambda b, g, i: (b * nq + i, g)),
                   pl.BlockSpec((None, None, TQ, LANES), lambda b, g, i: (b, g, i, 0))],
        out_shape=[jax.ShapeDtypeStruct((BATCH * SEQ, ATTN_W), f32),
                   jax.ShapeDtypeStruct((BATCH, N_KV, SEQ, LANES), f32)],
        compiler_params=_cparams(("parallel", "parallel", "parallel")),
        name="cmp_attn_prompt",
    )(q, kd, vl, vh, mt)


VT_ROWS = HEAD_DIM + 16
SEL_KT = 2


def _flash_tile(qs, k, vt, mask, m, acc):
    s = _nt(k, qs)
    if mask is not None:
        s = jnp.where(mask, s, NEG)
    m_next = jnp.maximum(m, jnp.max(s, axis=0, keepdims=True))
    alpha = jnp.exp2(m - m_next)
    p = jnp.exp2(s - m_next).astype(bf16)
    pv = jnp.concatenate([jnp.dot(vt, p[:, r * TQ:(r + 1) * TQ], preferred_element_type=f32)
                          for r in range(Q_PER_KV)], axis=1)
    return m_next, alpha * acc + pv


def _flash_finish(acc, o_ref):
    o_t = acc[:HEAD_DIM] * (1.0 / acc[HEAD_DIM:HEAD_DIM + 1])
    heads = jnp.concatenate([o_t[:, r * TQ:(r + 1) * TQ] for r in range(Q_PER_KV)], axis=0)
    o_ref[...] = heads.T


LOG2E = math.log2(math.e)


def _stack_queries(q_ref, qs_ref):
    for r in range(Q_PER_KV):
        q2 = q_ref[:, (r // 2) * LANES:(r // 2 + 1) * LANES] * (ATTN_SCALE * LOG2E)
        qs_ref[r * TQ:(r + 1) * TQ, 0:LANES] = jnp.where(_half_mask((TQ, LANES), r % 2), q2, 0.0).astype(bf16)


def _tile_masks():
    key = lax.broadcasted_iota(jnp.int32, (TQ, Q_PER_KV * TQ), 0)
    qry = lax.broadcasted_iota(jnp.int32, (TQ, Q_PER_KV * TQ), 1) & (TQ - 1)
    return key, qry


def _flash_init():
    return jnp.full((1, Q_PER_KV * TQ), NEG, f32), jnp.zeros((VT_ROWS, Q_PER_KV * TQ), f32)


def _sel_attn_kernel(q_ref, selm_ref, kd_ref, oh_ref, vt_ref, o_ref, qs_ref, m_ref, acc_ref):
    i = pl.program_id(2)
    _stack_queries(q_ref, qs_ref)
    selm = selm_ref[...].astype(bf16)
    for r in range(Q_PER_KV):
        qs_ref[r * TQ:(r + 1) * TQ, LANES:2 * LANES] = selm
    qs = qs_ref[...]
    key, qry = _tile_masks()

    def tile(kt, n, mask, m, acc):
        rows = pl.ds(pl.multiple_of(kt * TQ, TQ), n * TQ)
        k = jnp.concatenate([kd_ref[rows, :], oh_ref[rows, :]], axis=1)
        return _flash_tile(qs, k, _with_ones(vt_ref[:, rows]), mask, m, acc)

    m, acc = tile(i, 1, key <= qry, *_flash_init())

    def body(t, carry):
        return tile(SEL_KT * t, SEL_KT, None, *carry)

    m, acc = lax.fori_loop(0, i // SEL_KT, body, (m, acc))
    m_ref[...] = m
    acc_ref[...] = acc

    @pl.when(i % SEL_KT == 1)
    def _():
        _, acc2 = tile(i - 1, 1, None, m_ref[...], acc_ref[...])
        acc_ref[...] = acc2

    _flash_finish(acc_ref[...], o_ref)


def _win_attn_kernel(q_ref, kd_ref, vt_ref, o_ref, qs_ref, acc_ref):
    i = pl.program_id(2)
    _stack_queries(q_ref, qs_ref)
    qs = qs_ref[...]

    def span(first, n):
        rows = pl.ds(pl.multiple_of(first * TQ, TQ), n * TQ)
        key = lax.broadcasted_iota(jnp.int32, (n * TQ, Q_PER_KV * TQ), 0)
        qry = (lax.broadcasted_iota(jnp.int32, (n * TQ, Q_PER_KV * TQ), 1) & (TQ - 1)) + (n - 1) * TQ
        dist = qry - key
        _, acc = _flash_tile(qs, kd_ref[rows, :], _with_ones(vt_ref[:, rows]), (dist >= 0) & (dist < WINDOW),
                             *_flash_init())
        acc_ref[...] = acc

    n_back = WINDOW // TQ

    @pl.when(i >= n_back)
    def _():
        span(i - n_back, n_back + 1)

    for early in range(n_back):
        @pl.when(i == early)
        def _(early=early):
            span(0, early + 1)

    _flash_finish(acc_ref[...], o_ref)


def _with_ones(vt):
    return jnp.concatenate([vt.astype(bf16), jnp.ones((VT_ROWS - HEAD_DIM, vt.shape[1]), bf16)], axis=0)


def _prompt_attn_specs():
    nq = SEQ // TQ
    q_spec = pl.BlockSpec((TQ, 2 * LANES), lambda b, g, i: (b * nq + i, g))
    kd_spec = pl.BlockSpec((SEQ, LANES), lambda b, g, i: (b, g))
    vt_spec = pl.BlockSpec((None, HEAD_DIM, SEQ), lambda b, g, i: (b, N_KV + g, 0))
    return nq, q_spec, kd_spec, vt_spec


def _sel_attn_prompt(q, selm, kd, kv_t):
    nq, q_spec, kd_spec, vt_spec = _prompt_attn_specs()
    blk = np.arange(SEQ)[:, None] // SEL_BLOCK == np.arange(LANES)[None, :]
    oh = jnp.asarray(blk.astype(np.float32), bf16)
    return pl.pallas_call(
        _sel_attn_kernel,
        grid=(BATCH, N_KV, nq),
        in_specs=[q_spec, pl.BlockSpec((None, None, TQ, LANES), lambda b, g, i: (b, g, i, 0)),
                  kd_spec, pl.BlockSpec((SEQ, LANES), lambda b, g, i: (0, 0)), vt_spec],
        out_specs=q_spec,
        out_shape=jax.ShapeDtypeStruct((BATCH * SEQ, ATTN_W), f32),
        scratch_shapes=[pltpu.VMEM((Q_PER_KV * TQ, 2 * LANES), bf16), pltpu.VMEM((1, Q_PER_KV * TQ), f32),
                        pltpu.VMEM((VT_ROWS, Q_PER_KV * TQ), f32)],
        compiler_params=_cparams(("parallel", "parallel", "parallel")),
        name="sel_attn_prompt",
    )(q, selm, kd, oh, kv_t)


def _win_attn_prompt(q, kd, kv_t):
    nq, q_spec, kd_spec, vt_spec = _prompt_attn_specs()
    return pl.pallas_call(
        _win_attn_kernel,
        grid=(BATCH, N_KV, nq),
        in_specs=[q_spec, kd_spec, vt_spec],
        out_specs=q_spec,
        out_shape=jax.ShapeDtypeStruct((BATCH * SEQ, ATTN_W), f32),
        scratch_shapes=[pltpu.VMEM((Q_PER_KV * TQ, LANES), bf16), pltpu.VMEM((VT_ROWS, Q_PER_KV * TQ), f32)],
        compiler_params=_cparams(("parallel", "parallel", "parallel")),
        name="win_attn_prompt",
    )(q, kd, kv_t)


SSD_Q = 256
HALF_INNER = D_INNER // N_SSM_GROUPS
BC_W = N_SSM_GROUPS * D_STATE


def _mamba_consts(conv_w, conv_b, dt_bias, a_log, d_skip, ssm_norm):
    pad = lambda v: jnp.zeros((1, GD_W), f32).at[0, DT_LANE0:DT_LANE0 + N_SSM_HEADS].set(v)
    e16 = np.zeros((GD_W, D_INNER), np.float32)
    for h in range(N_SSM_HEADS):
        e16[DT_LANE0 + h, h * SSM_HEAD_DIM:(h + 1) * SSM_HEAD_DIM] = 1.0
    tri = np.tril(np.ones((SSD_Q, SSD_Q), np.float32))
    return (conv_w, conv_b[None], pad(dt_bias), pad(a_log), jnp.asarray(e16, bf16),
            jnp.repeat(d_skip, SSM_HEAD_DIM)[None], ssm_norm[None], jnp.asarray(tri, bf16))


def _dt_and_decay(gd, dtb_ref, alog_ref):
    lane = lax.broadcasted_iota(jnp.int32, gd.shape, 1)
    live = (lane >= DT_LANE0) & (lane < DT_LANE0 + N_SSM_HEADS)
    dt = jnp.where(live, _softplus(gd + dtb_ref[...]), 0.0)
    return dt, dt * (-jnp.exp(alog_ref[...]))


def _gated_group_norm(y, z, nw_ref):
    y = y * _silu(z)
    outs = []
    for g in range(N_SSM_GROUPS):
        yg = y[:, g * HALF_INNER:(g + 1) * HALF_INNER]
        outs.append(yg * lax.rsqrt(jnp.mean(yg * yg, axis=-1, keepdims=True) + NORM_EPS))
    return jnp.concatenate(outs, axis=1) * nw_ref[...]


def _mamba_prompt_kernel(z_ref, xbc_ref, gd_ref, cw_ref, cb_ref, dtb_ref, alog_ref, e16_ref, dsk_ref, nw_ref, tri_ref,
                         y_ref, st_ref, xpad_ref, state_ref):
    c = pl.program_id(1)
    nq = SSD_Q

    @pl.when(c == 0)
    def _():
        state_ref[...] = jnp.zeros(state_ref.shape, f32)
        xpad_ref[0:8, :] = jnp.zeros((8, CONV_DIM), f32)

    xpad_ref[8:8 + nq, :] = xbc_ref[...]
    conv = cb_ref[...]
    for w in range(CONV_W):
        conv = conv + xpad_ref[8 - (CONV_W - 1) + w:8 - (CONV_W - 1) + w + nq, :] * cw_ref[w:w + 1, :]
    xpad_ref[0:8, :] = xpad_ref[nq:nq + 8, :]
    act = _silu(conv)
    xs, bm, cm = act[:, :D_INNER], act[:, D_INNER:D_INNER + BC_W], act[:, D_INNER + BC_W:]

    dt, a = _dt_and_decay(gd_ref[...], dtb_ref, alog_ref)
    a_cs = sum(jnp.dot(tri_ref[...], t, preferred_element_type=f32) for t in _split3(a))
    ea = jnp.exp(a_cs)
    te = jnp.exp(a_cs[nq - 1:nq, :] - a_cs)
    e16 = e16_ref[...]
    dt_x, ea_x, te_x = _dot_exact(dt, e16), _dot_exact(ea, e16), _dot_exact(te, e16)
    xdt = xs * dt_x
    xw = (xdt * te_x).astype(bf16)
    a_cst = a_cs.T
    tril = lax.broadcasted_iota(jnp.int32, (nq, nq), 1) <= lax.broadcasted_iota(jnp.int32, (nq, nq), 0)

    ys = []
    for g in range(N_SSM_GROUPS):
        bg = bm[:, g * D_STATE:(g + 1) * D_STATE]
        cgb = cm[:, g * D_STATE:(g + 1) * D_STATE].astype(bf16)
        cb = _nt(cgb, bg.astype(bf16))
        st = state_ref[g]
        yoff = jnp.dot(cgb, st.astype(bf16), preferred_element_type=f32)
        gl = slice(g * HALF_INNER, (g + 1) * HALF_INNER)
        state_ref[g] = ea_x[nq - 1:nq, gl] * st + jnp.dot(bg.T.astype(bf16), xw[:, gl], preferred_element_type=f32)
        for hp in range(HALF_INNER // LANES):
            pl_ = slice(g * HALF_INNER + hp * LANES, g * HALF_INNER + (hp + 1) * LANES)
            xp = xdt[:, pl_]
            yp = ea_x[:, pl_] * yoff[:, hp * LANES:(hp + 1) * LANES] + dsk_ref[:, pl_] * xs[:, pl_]
            for e in range(2):
                lane_h = DT_LANE0 + g * (N_SSM_HEADS // N_SSM_GROUPS) + hp * 2 + e
                seg = a_cs[:, lane_h:lane_h + 1] - a_cst[lane_h:lane_h + 1, :]
                wgt = (cb * jnp.exp(jnp.where(tril, seg, NEG))).astype(bf16)
                xz = jnp.where(_half_mask((nq, LANES), e), xp, 0.0).astype(bf16)
                yp = yp + jnp.dot(wgt, xz, preferred_element_type=f32)
            ys.append(yp)
    y = jnp.concatenate(ys, axis=1)
    y_ref[...] = _gated_group_norm(y, z_ref[...], nw_ref).astype(bf16)

    @pl.when(c == pl.num_programs(1) - 1)
    def _():
        st_ref[...] = state_ref[...]


def _mamba_prompt(z, xbc, gd, mc):
    nc = SEQ // SSD_Q
    row = lambda w: pl.BlockSpec((SSD_Q, w), lambda b, c: (b * nc + c, 0))
    y, st = pl.pallas_call(
        _mamba_prompt_kernel,
        grid=(BATCH, nc),
        in_specs=[row(D_INNER), row(CONV_DIM), row(GD_W)] + [_const_spec(a, 2) for a in mc],
        out_specs=[row(D_INNER), pl.BlockSpec((None, N_SSM_GROUPS, D_STATE, HALF_INNER), lambda b, c: (b, 0, 0, 0))],
        out_shape=[jax.ShapeDtypeStruct((BATCH * SEQ, D_INNER), bf16),
                   jax.ShapeDtypeStruct((BATCH, N_SSM_GROUPS, D_STATE, HALF_INNER), f32)],
        scratch_shapes=[pltpu.VMEM((SSD_Q + 8, CONV_DIM), f32), pltpu.VMEM((N_SSM_GROUPS, D_STATE, HALF_INNER), f32)],
        compiler_params=_cparams(("parallel", "arbitrary")),
        name="mamba_prompt",
    )(z, xbc, gd, *mc)
    hpg = N_SSM_HEADS // N_SSM_GROUPS
    st = st.reshape(BATCH, N_SSM_GROUPS, D_STATE, hpg, SSM_HEAD_DIM).transpose(0, 1, 3, 4, 2)
    return y, st.reshape(BATCH, N_SSM_HEADS, SSM_HEAD_DIM, D_STATE)


def _mamba_sample_kernel(z_ref, xbc_ref, gd_ref, sc_ref, s_ref, cw_ref, cb_ref, dtb_ref, alog_ref, e16_ref, dsk_ref,
                         nw_ref, y_ref, so_ref):
    conv = cb_ref[...] + xbc_ref[...] * cw_ref[CONV_W - 1:CONV_W, :]
    for w in range(CONV_W - 1):
        conv = conv + sc_ref[w:w + 1, :] * cw_ref[w:w + 1, :]
    act = _silu(conv)
    xs, bm, cm = act[:, :D_INNER], act[:, D_INNER:D_INNER + BC_W], act[:, D_INNER + BC_W:]
    dt, a = _dt_and_decay(jnp.broadcast_to(gd_ref[...], (8, GD_W)), dtb_ref, alog_ref)
    e16 = e16_ref[...]
    dt_x = _dot_exact(dt, e16)[0:1]
    da_x = _dot_exact(jnp.exp(a), e16)[0:1]
    rows = N_SSM_HEADS * SSM_HEAD_DIM
    xcol = jnp.broadcast_to(xs * dt_x, (D_STATE, rows)).T
    acol = jnp.broadcast_to(da_x, (D_STATE, rows)).T
    rowi = lax.broadcasted_iota(jnp.int32, (rows, D_STATE), 0)
    bfull = jnp.where(rowi < HALF_INNER, bm[:, :D_STATE], bm[:, D_STATE:])
    snew = s_ref[...] * acol + xcol * bfull
    so_ref[...] = snew
    r8 = lax.broadcasted_iota(jnp.int32, (8, D_STATE), 0)
    c8 = jnp.where(r8 == 0, cm[:, :D_STATE], jnp.where(r8 == 1, cm[:, D_STATE:], 0.0)).astype(bf16)
    yy = _nt(c8, snew.astype(bf16))
    lane = lax.broadcasted_iota(jnp.int32, (1, rows), 1)
    y = jnp.where(lane < HALF_INNER, yy[0:1], yy[1:2]) + dsk_ref[...] * xs
    y_ref[...] = _gated_group_norm(y, z_ref[...], nw_ref).astype(bf16)


def _mamba_sample(z, xbc, gd, state_conv, state_ssm, mc):
    nb = z.shape[0]
    mc = mc[:7]
    rows = N_SSM_HEADS * SSM_HEAD_DIM
    one = lambda w: pl.BlockSpec((None, 1, w), lambda b: (b, 0, 0))
    s_spec = pl.BlockSpec((None, rows, D_STATE), lambda b: (b, 0, 0))
    y, so = pl.pallas_call(
        _mamba_sample_kernel,
        grid=(nb,),
        in_specs=[one(D_INNER), one(CONV_DIM), one(GD_W),
                  pl.BlockSpec((None, CONV_W - 1, CONV_DIM), lambda b: (b, 0, 0)), s_spec]
                 + [_const_spec(a, 1) for a in mc],
        out_specs=[one(D_INNER), s_spec],
        out_shape=[jax.ShapeDtypeStruct((nb, 1, D_INNER), bf16), jax.ShapeDtypeStruct((nb, rows, D_STATE), f32)],
        compiler_params=_cparams(("parallel",)),
        name="mamba_sample",
    )(z[:, None], xbc[:, None], gd[:, None], state_conv, state_ssm.reshape(nb, rows, D_STATE), *mc)
    return y[:, 0], so.reshape(nb, N_SSM_HEADS, SSM_HEAD_DIM, D_STATE)


N_CMP_S = PAST_LEN // CMP_STRIDE - 1
N_SLC_S = PAST_LEN // SEL_BLOCK + 1
N_SLC_PAD = 2 * LANES
HALF_PAGE = PAGE_SIZE // SEL_BLOCK
GONE = -3.0e38


def _cmp_attn_sample_kernel(qz_ref, kd_ref, vl_ref, vh_ref, mts_ref, o_ref, imp_ref):
    qz = (qz_ref[...] * ATTN_SCALE).astype(bf16)
    ncmp = kd_ref.shape[0]
    nidx = lax.broadcasted_iota(jnp.int32, (N_HEADS, ncmp), 1)
    rowh = lax.broadcasted_iota(jnp.int32, (N_HEADS, ncmp), 0)
    mask = (nidx < N_CMP_S) & (nidx * CMP_STRIDE + (CMP_BLOCK - 1) <= PAST_LEN)
    rowo = lax.broadcasted_iota(jnp.int32, (N_HEADS, LANES), 0)
    r8 = lax.broadcasted_iota(jnp.int32, (8, N_SLC_PAD), 0)
    o_acc = jnp.zeros((N_HEADS, LANES), f32)
    imp = jnp.zeros((8, N_SLC_PAD), f32)
    for g in range(N_KV):
        gl = slice(g * LANES, (g + 1) * LANES)
        s = jnp.where(mask, _nt(qz, kd_ref[:, gl]), NEG)
        m = jnp.max(s, axis=1, keepdims=True)
        ex = jnp.where(mask, jnp.exp(s - m), 0.0)
        p = ex * (1.0 / jnp.maximum(jnp.sum(ex, axis=1, keepdims=True), 1e-30))
        p = jnp.where(rowh // Q_PER_KV == g, p, 0.0)
        pb = p.astype(bf16)
        o_l = jnp.dot(pb, vl_ref[:, gl], preferred_element_type=f32)
        o_h = jnp.dot(pb, vh_ref[:, gl], preferred_element_type=f32)
        o_acc = o_acc + jnp.where(rowo % 2 == 0, o_l, o_h)
        hi, lo = _split2(p)
        imp_h = jnp.dot(hi, mts_ref[...], preferred_element_type=f32) + jnp.dot(lo, mts_ref[...], preferred_element_type=f32)
        imp = imp + jnp.where(r8 == g, jnp.sum(imp_h, axis=0, keepdims=True), 0.0)
    o_ref[...] = o_acc
    imp_ref[...] = imp


def _topk_sample_kernel(imp_ref, idx_ref):
    rows = imp_ref.shape[0]
    j = lax.broadcasted_iota(jnp.int32, (rows, N_SLC_PAD), 1)
    jt = PAST_LEN // SEL_BLOCK
    imp = imp_ref[...]
    imp = jnp.where((j == 0) | (j == jt) | (j == jt - 1), SELECT_FORCE, imp)
    imp = jnp.where(j > jt, -SELECT_FORCE, imp)
    imp = jnp.where(j >= N_SLC_S, NEG, imp)
    jf = j.astype(f32)
    lane = lax.broadcasted_iota(jnp.int32, (rows, LANES), 1)
    picked = jnp.zeros((rows, LANES), f32)
    for k in range(SEL_TOPN):
        m = jnp.max(imp, axis=1, keepdims=True)
        ix = jnp.min(jnp.where(imp == m, jf, float(N_SLC_PAD)), axis=1, keepdims=True)
        picked = jnp.where(lane == k, ix, picked)
        imp = jnp.where(jf == ix, GONE, imp)
    idx_ref[...] = picked.astype(jnp.int32)


def _cmp_attn_sample(qz, kd, vl, vh):
    nb, ncmp = kd.shape[0], kd.shape[1]
    mts = np.zeros((ncmp, N_SLC_PAD), np.float32)
    mts[:, :N_SLC_S] = _cmp_to_sel_matrix_t(ncmp, N_SLC_S, N_CMP_S).T
    mts = jnp.asarray(mts, bf16)
    kv_spec = pl.BlockSpec((None, ncmp, KV_W), lambda b: (b, 0, 0))
    o16, imp = pl.pallas_call(
        _cmp_attn_sample_kernel,
        grid=(nb,),
        in_specs=[pl.BlockSpec((None, N_HEADS, LANES), lambda b: (b, 0, 0)), kv_spec, kv_spec, kv_spec,
                  pl.BlockSpec(mts.shape, lambda b: (0, 0))],
        out_specs=[pl.BlockSpec((None, N_HEADS, LANES), lambda b: (b, 0, 0)),
                   pl.BlockSpec((None, 8, N_SLC_PAD), lambda b: (b, 0, 0))],
        out_shape=[jax.ShapeDtypeStruct((nb, N_HEADS, LANES), f32), jax.ShapeDtypeStruct((nb, 8, N_SLC_PAD), f32)],
        compiler_params=_cparams(("parallel",)),
        name="cmp_attn_sample",
    )(qz, kd, vl, vh, mts)
    idx = pl.pallas_call(
        _topk_sample_kernel,
        out_shape=jax.ShapeDtypeStruct((nb * 8, LANES), jnp.int32),
        name="topk_sample",
    )(imp.reshape(nb * 8, N_SLC_PAD))
    return o16, idx.reshape(nb, 8, LANES)


def _one_query_softmax(s, s_new, vt, v_new):
    m = jnp.maximum(jnp.max(s, axis=1, keepdims=True), s_new)
    p = jnp.exp(s - m)
    p_new = jnp.exp(s_new - m)
    den = jnp.sum(p, axis=1, keepdims=True) + p_new
    num = _nt(p.astype(bf16), vt) + p_new * v_new
    return num * (1.0 / den)


def _selwin_sample_kernel(idx_ref, pt_ref, *refs):
    del pt_ref
    kb, vb = refs[:SEL_TOPN], refs[SEL_TOPN:2 * SEL_TOPN]
    q_ref, ksn_ref, wk_ref, wv_ref, kwn_ref, os_ref, ow_ref = refs[2 * SEL_TOPN:]
    b, g = pl.program_id(0), pl.program_id(1)
    q8 = q_ref[...] * ATTN_SCALE
    qb = q8.astype(bf16)
    kcat = jnp.concatenate([r[...] for r in kb], axis=1).astype(bf16)
    vcat = jnp.concatenate([r[...] for r in vb], axis=1).astype(bf16)
    n = SEL_TOPN * PAGE_SIZE
    lane = lax.broadcasted_iota(jnp.int32, (8, n), 1)
    slot = lane // PAGE_SIZE
    blk = jnp.zeros((8, n), jnp.int32)
    for k in range(SEL_TOPN):
        blk = jnp.where(slot == k, idx_ref[(b * N_KV + g) * SEL_TOPN + k], blk)
    live = (blk < N_SLC_S - 1) & ((lane % PAGE_SIZE) // SEL_BLOCK == blk % HALF_PAGE)
    s = jnp.where(live, jnp.dot(qb, kcat, preferred_element_type=f32), NEG)
    k_new, v_new = ksn_ref[pl.ds(g, 1), :], ksn_ref[pl.ds(N_KV + g, 1), :]
    s_new = jnp.sum(q8 * k_new, axis=1, keepdims=True)
    os_ref[...] = _one_query_softmax(s, s_new, vcat, v_new)
    nbuf = wk_ref.shape[1]
    i = lax.broadcasted_iota(jnp.int32, (8, nbuf), 1)
    s = jnp.where(nbuf - i < WINDOW, jnp.dot(qb, wk_ref[...].astype(bf16), preferred_element_type=f32), NEG)
    k_new, v_new = kwn_ref[pl.ds(g, 1), :], kwn_ref[pl.ds(N_KV + g, 1), :]
    s_new = jnp.sum(q8 * k_new, axis=1, keepdims=True)
    ow_ref[...] = _one_query_softmax(s, s_new, wv_ref[...].astype(bf16), v_new)


def _selwin_sample(idx, page_table, cache_s, q8, ks_new, win_buf, kw_new):
    nb = q8.shape[0]
    nbuf = win_buf.shape[1]
    pages = _token_minor(cache_s)
    wb = _token_minor(win_buf)

    def blk_map(b, g, idx_ref, pt_ref, k, c):
        j = jnp.minimum(idx_ref[(b * N_KV + g) * SEL_TOPN + k], N_SLC_S - 2)
        return (pt_ref[b * N_PAGES + j // HALF_PAGE], c * N_KV + g, 0)

    blk_specs = [pl.BlockSpec((None, HEAD_DIM, PAGE_SIZE), functools.partial(blk_map, k=k, c=c))
                 for c in range(2) for k in range(SEL_TOPN)]
    new_spec = pl.BlockSpec((None, 2 * N_KV, HEAD_DIM), lambda b, g, *_: (b, 0, 0))
    win_spec = lambda c: pl.BlockSpec((None, HEAD_DIM, nbuf), lambda b, g, *_: (b, c * N_KV + g, 0))
    o_spec = pl.BlockSpec((None, None, 8, HEAD_DIM), lambda b, g, *_: (b, g, 0, 0))
    grid_spec = pltpu.PrefetchScalarGridSpec(
        num_scalar_prefetch=2,
        grid=(nb, N_KV),
        in_specs=blk_specs + [o_spec, new_spec, win_spec(0), win_spec(1), new_spec],
        out_specs=[o_spec, o_spec])
    o_shape = jax.ShapeDtypeStruct((nb, N_KV, 8, HEAD_DIM), f32)
    ks3, kw3 = ks_new.reshape(nb, 2 * N_KV, HEAD_DIM), kw_new.reshape(nb, 2 * N_KV, HEAD_DIM)
    os8, ow8 = pl.pallas_call(
        _selwin_sample_kernel,
        grid_spec=grid_spec,
        out_shape=[o_shape, o_shape],
        compiler_params=_cparams(("arbitrary", "arbitrary")),
        name="selwin_sample",
    )(idx.reshape(-1), page_table.reshape(-1), *([pages] * (2 * SEL_TOPN)), q8, ks3, wb, wb, kw3)
    return os8[:, :, :Q_PER_KV].reshape(nb, ATTN_W), ow8[:, :, :Q_PER_KV].reshape(nb, ATTN_W)


def _sample_query_layouts(q):
    nb = q.shape[0]
    q16 = q.reshape(nb, N_HEADS, HEAD_DIM)
    z = jnp.zeros_like(q16)
    lo, hi = jnp.concatenate([q16, z], axis=-1), jnp.concatenate([z, q16], axis=-1)
    h = jnp.arange(N_HEADS)[None, :, None]
    qz = jnp.where(h % 2 == 0, lo, hi)
    qg = q16.reshape(nb, N_KV, Q_PER_KV, HEAD_DIM)
    q8 = jnp.concatenate([qg, jnp.zeros_like(qg)], axis=2)
    return qz, q8


def _unpad_heads_by_parity(o16):
    nb = o16.shape[0]
    o = o16.reshape(nb, N_HEADS // 2, 2, 2, HEAD_DIM)
    return jnp.stack([o[:, :, 0, 0], o[:, :, 1, 1]], axis=2).reshape(nb, ATTN_W)


def _finish(x, ocmp, osel, owin, gd, ssd_y, w_out, ln_mlp, w_up, w_down, ln_final, tm, tm_mlp, tf):
    x1 = _outproj(x, ocmp, osel, owin, gd, ssd_y, _gate_expand(), w_out, tm)
    return _mlp(x1, ln_mlp, w_up, w_down, ln_final, tm_mlp, tf)


def kernel(x_prompt, x_sample, cache_kv_cmp, cache_kv_sel, state_kv_win, state_conv, state_ssm, page_table, ln_mix, w_in, cmp_pe, cmp_w1, cmp_b1, cmp_w2, cmp_b2, conv_w, conv_b, dt_bias, a_log, d_skip, ssm_norm, w_out, ln_mlp, w_up, w_down, ln_final):
    nb = x_sample.shape[0]
    w_in_p = _prep_w_in(w_in[0])
    w_out_b, w_up_b, w_down_b = w_out[0].astype(bf16), w_up[0].astype(bf16), w_down[0].astype(bf16)
    lnw, lnm, lnf = ln_mix[0][None], ln_mlp[0][None], ln_final[None]
    cw = _compress_weights(cmp_pe[0], cmp_w1[0], cmp_b1[0], cmp_w2[0], cmp_b2[0])
    mc = _mamba_consts(conv_w[0], conv_b[0], dt_bias[0], a_log[0], d_skip[0], ssm_norm[0])
    kv_shape = (2, N_KV, HEAD_DIM)

    xp = x_prompt.reshape(BATCH * SEQ, D_MODEL)
    q, z, xbc, gd, kc, kct, kst, kwt, kds, kdw = _inproj_prompt(xp, lnw, w_in_p, 256)
    kd, vl, vh = _compress_prompt(kc, cw)
    ocmp, selm = _cmp_attn_prompt(q, kd, vl, vh)
    osel = _sel_attn_prompt(q, selm, kds, kst)
    owin = _win_attn_prompt(q, kdw, kwt)
    ssd_y, ssm_p = _mamba_prompt(z, xbc, gd, mc)
    y_prompt = _finish(xp, ocmp, osel, owin, gd, ssd_y, w_out_b, lnm, w_up_b, w_down_b, lnf, 256, 512, 1024)

    xs = x_sample.reshape(nb, D_MODEL)
    q_s, z_s, xbc_s, gd_s, kc_s, ks_s, kw_s = _inproj(xs, lnw, w_in_p, nb)
    kd_s, vl_s, vh_s = _compress_sample(cache_kv_cmp[0], page_table, cw)
    qz, q8 = _sample_query_layouts(q_s)
    ocmp16, idx8 = _cmp_attn_sample(qz, kd_s, vl_s, vh_s)
    idx = idx8[:, :N_KV, :SEL_TOPN]
    osel_s, owin_s = _selwin_sample(idx, page_table, cache_kv_sel[0], q8, ks_s, state_kv_win[0], kw_s)
    ssd_y_s, ssm_s = _mamba_sample(z_s, xbc_s, gd_s, state_conv[0], state_ssm[0], mc)
    y_sample = _finish(xs, _unpad_heads_by_parity(ocmp16), osel_s, owin_s, gd_s, ssd_y_s,
                       w_out_b, lnm, w_up_b, w_down_b, lnf, nb, nb, 1024)

    def token_major(t):
        return t.reshape(BATCH, *kv_shape, t.shape[-1]).transpose(0, 4, 1, 2, 3)[None]

    conv_p = xbc.reshape(BATCH, SEQ, CONV_DIM)[:, -(CONV_W - 1):]
    kv_win_s = jnp.concatenate([state_kv_win[0], kw_s.reshape(nb, 1, *kv_shape)], axis=1)[:, -WINDOW:]
    conv_s = jnp.concatenate([state_conv[0], xbc_s[:, None]], axis=1)[:, -(CONV_W - 1):]
    return (y_prompt.reshape(BATCH, SEQ, D_MODEL), y_sample.reshape(nb, 1, D_MODEL),
            token_major(kct), token_major(kst), token_major(kwt[:, :, -min(WINDOW, SEQ):]),
            conv_p[None], ssm_p[None],
            kc_s.reshape(1, nb, 1, *kv_shape), ks_s.reshape(1, nb, 1, *kv_shape), kv_win_s[None],
            conv_s[None], ssm_s[None])
```

```python
import functools
import math

import jax
import jax.numpy as jnp
import numpy as np
from jax import lax
from jax.experimental import pallas as pl
from jax.experimental.pallas import tpu as pltpu

f32 = jnp.float32
bf16 = jnp.bfloat16

D_MODEL = 2048
BATCH = 2
SEQ = 4096
DEC_BATCH = 32
PAST_LEN = 8192
PAGE_SIZE = 128
HEAD_DIM = 64
N_HEADS = 16
N_KV = 4
Q_PER_KV = 4
ATTN_W = 1024
KV_W = 512
CMP_BLOCK = 32
CMP_STRIDE = 16
CMP_HIDDEN = 128
SEL_BLOCK = 64
SEL_TOPN = 16
WINDOW = 512
SELECT_FORCE = 1.0e4
ATTN_SCALE = HEAD_DIM ** -0.5
D_INNER = 1024
SSM_HEAD_DIM = 64
N_SSM_HEADS = 16
N_SSM_GROUPS = 2
D_STATE = 128
CONV_W = 4
CONV_DIM = D_INNER + 2 * N_SSM_GROUPS * D_STATE
D_FF = 4 * D_MODEL
IN_SPLITS = (ATTN_W, KV_W, KV_W, KV_W, 3 * N_HEADS, D_INNER, CONV_DIM, N_SSM_HEADS)
NORM_EPS = 1e-5

LANES = 128
GD_W = LANES
DT_LANE0 = 3 * N_HEADS
N_PAGES = PAST_LEN // PAGE_SIZE
SEG_PER_PAGE = PAGE_SIZE // CMP_STRIDE
SEG_W = CMP_STRIDE * KV_W
NEG = -1.0e30
SEL_NEG = -1.0e9
VMEM_LIMIT = 56 * 1024 * 1024


def _cparams(sem):
    return pltpu.CompilerParams(dimension_semantics=sem, vmem_limit_bytes=VMEM_LIMIT)


def _nt(a, b):
    return lax.dot_general(a, b, (((1,), (1,)), ((), ())), preferred_element_type=f32)


def _split2(x):
    hi = x.astype(bf16)
    lo = (x - hi.astype(f32)).astype(bf16)
    return hi, lo


def _split3(x):
    hi = x.astype(bf16)
    r = x - hi.astype(f32)
    mid = r.astype(bf16)
    lo = (r - mid.astype(f32)).astype(bf16)
    return hi, mid, lo


def _dot_exact(x, w):
    return sum(jnp.dot(t, w, preferred_element_type=f32) for t in _split3(x))


def _silu(x):
    return x * (1.0 / (1.0 + jnp.exp(-x)))


def _sigmoid(x):
    return 1.0 / (1.0 + jnp.exp(-x))


def _softplus(x):
    return jnp.maximum(x, 0.0) + jnp.log1p(jnp.exp(-jnp.abs(x)))


IN_OUT_WIDTHS = (ATTN_W, D_INNER, CONV_DIM, GD_W, KV_W, KV_W, KV_W)
KD_W = N_KV * LANES
IN_COLS_ROWMAJOR = sum(IN_OUT_WIDTHS)


def _normed(x_ref, lnw_ref):
    x = x_ref[...]
    return (x * lax.rsqrt(jnp.mean(x * x, axis=-1, keepdims=True) + NORM_EPS) * lnw_ref[...]).astype(bf16)


def _inproj_kernel(x_ref, lnw_ref, w_ref, *out_refs):
    h = _normed(x_ref, lnw_ref)
    off = 0
    for ref in out_refs:
        n = ref.shape[-1]
        ref[...] = jnp.dot(h, w_ref[:, off:off + n], preferred_element_type=f32)
        off += n


def _inproj_prompt_kernel(x_ref, lnw_ref, w_ref, q_ref, z_ref, xbc_ref, gd_ref, kc_ref,
                          kct_ref, kst_ref, kwt_ref, kds_ref, kdw_ref, t_ref):
    h = _normed(x_ref, lnw_ref)
    off = 0

    def proj(n):
        nonlocal off
        y = jnp.dot(h, w_ref[:, off:off + n], preferred_element_type=f32)
        off += n
        return y

    for ref in (q_ref, z_ref, xbc_ref, gd_ref):
        ref[...] = proj(ref.shape[-1])
    kc_ref[...] = proj(KV_W)
    kct_ref[...] = kc_ref[...].T
    for t_out in (kst_ref, kwt_ref):
        t_ref[...] = proj(KV_W)
        t_out[...] = t_ref[...].T
    kds_ref[...] = proj(KD_W).astype(bf16)
    kdw_ref[...] = proj(KD_W).astype(bf16)


def _inproj(x, lnw, w_perm, tm):
    m = x.shape[0]
    return pl.pallas_call(
        _inproj_kernel,
        grid=(m // tm,),
        in_specs=[pl.BlockSpec((tm, D_MODEL), lambda i: (i, 0)),
                  pl.BlockSpec((1, D_MODEL), lambda i: (0, 0)),
                  pl.BlockSpec((D_MODEL, IN_COLS_ROWMAJOR), lambda i: (0, 0), pipeline_mode=pl.Buffered(1))],
        out_specs=[pl.BlockSpec((tm, n), lambda i: (i, 0)) for n in IN_OUT_WIDTHS],
        out_shape=[jax.ShapeDtypeStruct((m, n), f32) for n in IN_OUT_WIDTHS],
        compiler_params=_cparams(("parallel",)),
        name="inproj",
    )(x, lnw, w_perm)


def _inproj_prompt(x, lnw, w_perm, tm):
    nq = SEQ // tm
    row = lambda n: pl.BlockSpec((tm, n), lambda i: (i, 0))
    tmin = pl.BlockSpec((None, KV_W, tm), lambda i: (i // nq, 0, i % nq))
    rm = lambda n, dt=f32: jax.ShapeDtypeStruct((BATCH * SEQ, n), dt)
    tshape = jax.ShapeDtypeStruct((BATCH, KV_W, SEQ), f32)
    return pl.pallas_call(
        _inproj_prompt_kernel,
        grid=(BATCH * nq,),
        in_specs=[pl.BlockSpec((tm, D_MODEL), lambda i: (i, 0)),
                  pl.BlockSpec((1, D_MODEL), lambda i: (0, 0)),
                  pl.BlockSpec(w_perm.shape, lambda i: (0, 0), pipeline_mode=pl.Buffered(1))],
        out_specs=[row(ATTN_W), row(D_INNER), row(CONV_DIM), row(GD_W), row(KV_W), tmin, tmin, tmin, row(KD_W), row(KD_W)],
        out_shape=[rm(ATTN_W), rm(D_INNER), rm(CONV_DIM), rm(GD_W), rm(KV_W), tshape, tshape, tshape,
                   rm(KD_W, bf16), rm(KD_W, bf16)],
        scratch_shapes=[pltpu.VMEM((tm, KV_W), f32)],
        compiler_params=_cparams(("parallel",)),
        name="inproj_prompt",
    )(x, lnw, w_perm)


def _prep_w_in(w_in):
    parts, off = [], 0
    for width in IN_SPLITS:
        parts.append(w_in[:, off:off + width])
        off += width
    q, kc, ks, kw, g, z, xbc, dt = parts
    gd = jnp.concatenate([g, dt, jnp.zeros((D_MODEL, GD_W - 4 * N_HEADS), w_in.dtype)], axis=1)

    def dup_keys(kv):
        k = kv[:, :KV_W // 2].reshape(D_MODEL, N_KV, 1, HEAD_DIM)
        return jnp.broadcast_to(k, (D_MODEL, N_KV, 2, HEAD_DIM)).reshape(D_MODEL, KD_W)

    return jnp.concatenate([q, z, xbc, gd, kc, ks, kw, dup_keys(ks), dup_keys(kw)], axis=1).astype(bf16)


def _outproj_kernel(x_ref, oc_ref, os_ref, ow_ref, gd_ref, y_ref, eg_ref, w_ref, o_ref):
    gates = _sigmoid(gd_ref[...])
    hi, lo = _split2(gates)
    attn = None
    for c, br in enumerate((oc_ref, os_ref, ow_ref)):
        ge = (jnp.dot(hi, eg_ref[c], preferred_element_type=f32)
              + jnp.dot(lo, eg_ref[c], preferred_element_type=f32))
        term = ge * br[...]
        attn = term if attn is None else attn + term
    mix = (jnp.dot(attn.astype(bf16), w_ref[:ATTN_W, :], preferred_element_type=f32)
           + jnp.dot(y_ref[...], w_ref[ATTN_W:, :], preferred_element_type=f32))
    o_ref[...] = x_ref[...] + mix


def _outproj(x, ocmp, osel, owin, gd, ssd_y, eg, w_out, tm):
    m = x.shape[0]
    row = lambda w: pl.BlockSpec((tm, w), lambda i: (i, 0))
    return pl.pallas_call(
        _outproj_kernel,
        grid=(m // tm,),
        in_specs=[row(D_MODEL), row(ATTN_W), row(ATTN_W), row(ATTN_W), row(GD_W), row(D_INNER),
                  pl.BlockSpec((3, GD_W, ATTN_W), lambda i: (0, 0, 0)),
                  pl.BlockSpec((ATTN_W + D_INNER, D_MODEL), lambda i: (0, 0))],
        out_specs=row(D_MODEL),
        out_shape=jax.ShapeDtypeStruct((m, D_MODEL), f32),
        compiler_params=_cparams(("parallel",)),
        name="outproj",
    )(x, ocmp, osel, owin, gd, ssd_y, eg, w_out)


def _gate_expand():
    eg = np.zeros((3, GD_W, ATTN_W), np.float32)
    for h in range(N_HEADS):
        for c in range(3):
            eg[c, h * 3 + c, h * HEAD_DIM:(h + 1) * HEAD_DIM] = 1.0
    return jnp.asarray(eg, bf16)


def _mlp_kernel(x_ref, ln_ref, wu_ref, wd_ref, lnf_ref, o_ref, h_ref, acc_ref):
    k = pl.program_id(1)

    @pl.when(k == 0)
    def _():
        x = x_ref[...]
        h = x * lax.rsqrt(jnp.mean(x * x, axis=-1, keepdims=True) + NORM_EPS) * ln_ref[...]
        h_ref[...] = h.astype(bf16)
        acc_ref[...] = jnp.zeros_like(acc_ref)

    u = jnp.dot(h_ref[...], wu_ref[...], preferred_element_type=f32)
    u = jnp.square(jnp.maximum(u, 0.0)).astype(bf16)
    acc_ref[...] += jnp.dot(u, wd_ref[...], preferred_element_type=f32)

    @pl.when(k == pl.num_programs(1) - 1)
    def _():
        y = x_ref[...] + acc_ref[...]
        y = y * lax.rsqrt(jnp.mean(y * y, axis=-1, keepdims=True) + NORM_EPS) * lnf_ref[...]
        o_ref[...] = y


def _mlp(x, ln_mlp, w_up, w_down, ln_final, tm, tf):
    m = x.shape[0]
    return pl.pallas_call(
        _mlp_kernel,
        grid=(m // tm, D_FF // tf),
        in_specs=[pl.BlockSpec((tm, D_MODEL), lambda i, k: (i, 0)),
                  pl.BlockSpec((1, D_MODEL), lambda i, k: (0, 0)),
                  pl.BlockSpec((D_MODEL, tf), lambda i, k: (0, k)),
                  pl.BlockSpec((tf, D_MODEL), lambda i, k: (k, 0)),
                  pl.BlockSpec((1, D_MODEL), lambda i, k: (0, 0))],
        out_specs=pl.BlockSpec((tm, D_MODEL), lambda i, k: (i, 0)),
        out_shape=jax.ShapeDtypeStruct((m, D_MODEL), f32),
        scratch_shapes=[pltpu.VMEM((tm, D_MODEL), bf16), pltpu.VMEM((tm, D_MODEL), f32)],
        compiler_params=_cparams(("parallel", "arbitrary")),
        name="mlp",
    )(x, ln_mlp, w_up, w_down, ln_final)


N_SLAB_K = CMP_STRIDE * LANES


def _compress_weights(cmp_pe, cmp_w1, cmp_b1, cmp_w2, cmp_b2):
    eye2 = jnp.eye(2, dtype=f32)
    w1 = cmp_w1.reshape(2, 2, CMP_STRIDE, HEAD_DIM, CMP_HIDDEN)
    w1cat = jnp.einsum('crjdf,ab->cjadrbf', w1, eye2).reshape(2, N_SLAB_K, 4 * CMP_HIDDEN).astype(bf16)
    w1flat = cmp_w1.reshape(2, CMP_BLOCK * HEAD_DIM, CMP_HIDDEN).astype(bf16)
    pe8 = jnp.broadcast_to(cmp_pe.reshape(2, 1, CMP_BLOCK * HEAD_DIM), (2, 8, CMP_BLOCK * HEAD_DIM))
    b1 = cmp_b1.reshape(2, 1, CMP_HIDDEN)
    wk, wv = cmp_w2[0], cmp_w2[1]
    zk = jnp.zeros_like(wk)
    w2k = jnp.einsum('fd,ab,e->afbed', wk, eye2, jnp.ones((2,), f32)).reshape(2 * CMP_HIDDEN, 4 * HEAD_DIM)
    b2k = jnp.tile(cmp_b2[0], 4)[None]
    vl = jnp.einsum('fd,ab,e->afbed', wv, eye2, jnp.array([1.0, 0.0], f32)).reshape(2 * CMP_HIDDEN, 4 * HEAD_DIM)
    vh = jnp.einsum('fd,ab,e->afbed', wv, eye2, jnp.array([0.0, 1.0], f32)).reshape(2 * CMP_HIDDEN, 4 * HEAD_DIM)
    w2v = jnp.concatenate([vl, vh], axis=1)
    zb = jnp.zeros((HEAD_DIM,), f32)
    b2v = jnp.concatenate([cmp_b2[1], zb, cmp_b2[1], zb, zb, cmp_b2[1], zb, cmp_b2[1]])[None]
    del zk
    return pe8, w1flat, b1, w1cat, w2k.astype(bf16), b2k, w2v.astype(bf16), b2v


def _compress_slab(a, c, gp, prm_refs, out_refs):
    pe_ref, w1f_ref, b1_ref, w1_ref, w2k_ref, b2k_ref, w2v_ref, b2v_ref = prm_refs
    kd_ref, vl_ref, vh_ref = out_refs
    pet = jnp.dot(pe_ref[c].astype(bf16), w1f_ref[c], preferred_element_type=f32)[0:1] + b1_ref[c]
    bias = jnp.concatenate([pet, pet], axis=1)
    p = jnp.dot(a, w1_ref[c], preferred_element_type=f32)
    s = p.shape[0]
    hid = p[:, :2 * CMP_HIDDEN] + pltpu.roll(p[:, 2 * CMP_HIDDEN:], s - 1, axis=0)
    hid = _silu(hid + bias).astype(bf16)
    w = 4 * HEAD_DIM
    if c == 0:
        kd_ref[:, gp * w:(gp + 1) * w] = (jnp.dot(hid, w2k_ref[...], preferred_element_type=f32)
                                          + b2k_ref[...]).astype(bf16)
    else:
        o = jnp.dot(hid, w2v_ref[...], preferred_element_type=f32) + b2v_ref[...]
        vl_ref[:, gp * w:(gp + 1) * w] = o[:, :w].astype(bf16)
        vh_ref[:, gp * w:(gp + 1) * w] = o[:, w:].astype(bf16)


def _slab_lane0(j, c, gp):
    return j * KV_W + c * (KV_W // 2) + gp * LANES


def _compress_all_slabs(x_ref, prm_refs, out_refs):
    for c in range(2):
        for gp in range(2):
            a = jnp.concatenate(
                [x_ref[:, _slab_lane0(j, c, gp):_slab_lane0(j, c, gp) + LANES] for j in range(CMP_STRIDE)],
                axis=1).astype(bf16)
            _compress_slab(a, c, gp, prm_refs, out_refs)


def _compress_prompt_kernel(x_ref, *refs):
    _compress_all_slabs(x_ref, refs[:8], refs[8:11])


def _seg_perm():
    m = np.zeros((2 * PAGE_SIZE, 2 * PAGE_SIZE), np.float32)
    for pp in range(2):
        for s in range(SEG_PER_PAGE):
            for j in range(CMP_STRIDE):
                m[j * 2 * SEG_PER_PAGE + pp * SEG_PER_PAGE + s, pp * PAGE_SIZE + s * CMP_STRIDE + j] = 1.0
    return jnp.asarray(m, bf16)


def _compress_sample_kernel(pt_ref, *refs):
    del pt_ref
    pages, perm_ref = refs[:N_PAGES], refs[N_PAGES]
    prm_refs, out_refs, o_ref = refs[N_PAGES + 1:N_PAGES + 9], refs[N_PAGES + 9:N_PAGES + 12], refs[-1]
    rows = 2 * SEG_PER_PAGE
    for k in range(N_PAGES // 2):
        two = jnp.concatenate([pages[2 * k][...], pages[2 * k + 1][...]], axis=1).astype(bf16)
        o_ref[k] = _nt(perm_ref[...], two).astype(bf16)
    nseg = N_PAGES * SEG_PER_PAGE
    for c in range(2):
        for gp in range(2):
            l0 = c * (KV_W // 2) + gp * LANES
            a = jnp.concatenate([o_ref[:, j * rows:(j + 1) * rows, l0:l0 + LANES].reshape(nseg, LANES)
                                 for j in range(CMP_STRIDE)], axis=1)
            _compress_slab(a, c, gp, prm_refs, out_refs)


def _const_spec(a, n_grid, n_prefetch=0):
    zeros = (0,) * a.ndim
    if n_grid == 1:
        return pl.BlockSpec(a.shape, (lambda b, *_: zeros))
    return pl.BlockSpec(a.shape, (lambda b, c, *_: zeros))


def _compress_prompt(kc, cw):
    nseg = SEQ // CMP_STRIDE
    x = kc.reshape(BATCH, nseg, SEG_W)
    out = jax.ShapeDtypeStruct((BATCH, nseg, KV_W), bf16)
    return pl.pallas_call(
        _compress_prompt_kernel,
        grid=(BATCH,),
        in_specs=[pl.BlockSpec((None, nseg, SEG_W), lambda b: (b, 0, 0))] + [_const_spec(a, 1) for a in cw],
        out_specs=[pl.BlockSpec((None, nseg, KV_W), lambda b: (b, 0, 0))] * 3,
        out_shape=[out] * 3,
        compiler_params=_cparams(("parallel",)),
        name="compress_prompt",
    )(x, *cw)


def _compress_sample(cache, page_table, cw):
    nseg = PAST_LEN // CMP_STRIDE
    x = _token_minor(cache)
    perm = _seg_perm()
    out = jax.ShapeDtypeStruct((DEC_BATCH, nseg, KV_W), bf16)
    page_specs = [pl.BlockSpec((None, KV_W, PAGE_SIZE), functools.partial(lambda b, pt, p: (pt[b * N_PAGES + p], 0, 0), p=p))
                  for p in range(N_PAGES)]
    grid_spec = pltpu.PrefetchScalarGridSpec(
        num_scalar_prefetch=1,
        grid=(DEC_BATCH,),
        in_specs=page_specs + [_const_spec(perm, 1)] + [_const_spec(a, 1) for a in cw],
        out_specs=[pl.BlockSpec((None, nseg, KV_W), lambda b, pt: (b, 0, 0))] * 3,
        scratch_shapes=[pltpu.VMEM((N_PAGES // 2, 2 * PAGE_SIZE, KV_W), bf16)])
    return pl.pallas_call(
        _compress_sample_kernel,
        grid_spec=grid_spec,
        out_shape=[out] * 3,
        compiler_params=_cparams(("arbitrary",)),
        name="compress_sample",
    )(page_table.reshape(-1), *([x] * N_PAGES), perm, *cw)


def _token_minor(kv):
    n, t = kv.shape[0], kv.shape[1]
    return kv.transpose(0, 2, 3, 4, 1).reshape(n, KV_W, t)


TQ = 256
N_SEL = SEQ // SEL_BLOCK


def _half_mask(shape, hi):
    lane = lax.broadcasted_iota(jnp.int32, shape, 1)
    return (lane >= HEAD_DIM) if hi else (lane < HEAD_DIM)


def _cmp_attn_kernel(q_ref, kd_ref, vl_ref, vh_ref, mt_ref, o_ref, selm_ref):
    i = pl.program_id(2)
    t0 = i * TQ
    ncmp = kd_ref.shape[0]
    tpos = t0 + lax.broadcasted_iota(jnp.int32, (TQ, ncmp), 0)
    nidx = lax.broadcasted_iota(jnp.int32, (TQ, ncmp), 1)
    mask = nidx * CMP_STRIDE + (CMP_BLOCK - 1) <= tpos
    kd, vl, vh = kd_ref[...], vl_ref[...], vh_ref[...]
    psum = jnp.zeros((TQ, ncmp), f32)
    for pr in range(2):
        q2 = q_ref[:, pr * LANES:(pr + 1) * LANES] * ATTN_SCALE
        acc = jnp.zeros((TQ, LANES), f32)
        for e in range(2):
            qz = jnp.where(_half_mask((TQ, LANES), e), q2, 0.0).astype(bf16)
            s = jnp.where(mask, _nt(qz, kd), NEG)
            m = jnp.max(s, axis=1, keepdims=True)
            ex = jnp.where(mask, jnp.exp(s - m), 0.0)
            l = jnp.sum(ex, axis=1, keepdims=True)
            p = ex * (1.0 / jnp.maximum(l, 1e-30))
            psum = psum + p
            acc = acc + jnp.dot(p.astype(bf16), vh if e else vl, preferred_element_type=f32)
        o_ref[:, pr * LANES:(pr + 1) * LANES] = acc
    hi, lo = _split2(psum)
    imp = _nt(mt_ref[...], hi) + _nt(mt_ref[...], lo)
    imp = imp[:N_SEL]
    j = lax.broadcasted_iota(jnp.int32, (N_SEL, TQ), 0)
    jt = (t0 + lax.broadcasted_iota(jnp.int32, (N_SEL, TQ), 1)) // SEL_BLOCK
    imp = jnp.where((j == 0) | (j == jt) | (j == jt - 1), SELECT_FORCE, imp)
    imp = jnp.where(j > jt, -SELECT_FORCE, imp)
    cnt = jnp.zeros((N_SEL, TQ), f32)
    for jp in range(N_SEL):
        row = imp[jp:jp + 1, :]
        cnt = cnt + jnp.where(j > jp, jnp.where(row >= imp, 1.0, 0.0), jnp.where(row > imp, 1.0, 0.0))
    selm = jnp.where((cnt < SEL_TOPN) & (j <= jt), 0.0, SEL_NEG)
    selm_ref[...] = jnp.concatenate([selm, selm], axis=0).T


def _cmp_to_sel_matrix_t(n_cmp_pad, n_slc, n_cmp):
    ratio = SEL_BLOCK // CMP_STRIDE
    i = np.arange(n_cmp_pad)[None, :]
    jj = np.arange(n_slc)[:, None]
    diff = i - ratio * jj
    mat = np.zeros((n_slc, n_cmp_pad), np.float32)
    for n in range(CMP_BLOCK // CMP_STRIDE):
        mat += ((diff + n >= 0) & (diff + n < ratio)).astype(np.float32)
    mat[:, n_cmp:] = 0.0
    return mat


def _cmp_attn_prompt(q, kd, vl, vh):
    ncmp = SEQ // CMP_STRIDE
    mt = np.zeros((LANES, ncmp), np.float32)
    mt[:N_SEL] = _cmp_to_sel_matrix_t(ncmp, N_SEL, ncmp - 1)
    mt = jnp.asarray(mt, bf16)
    nq = SEQ // TQ
    kv_spec = pl.BlockSpec((None, ncmp, LANES), lambda b, g, i: (b, 0, g))
    return pl.pallas_call(
        _cmp_attn_kernel,
        grid=(BATCH, N_KV, nq),
        in_specs=[pl.BlockSpec((TQ, 2 * LANES), lambda b, g, i: (b * nq + i, g)), kv_spec, kv_spec, kv_spec,
                  pl.BlockSpec(mt.shape, lambda b, g, i: (0, 0))],
        out_specs=[pl.BlockSpec((TQ, 2 * LANES), lambda b, g, i: (b * nq + i, g)),
                   pl.BlockSpec((None, None, TQ, LANES), lambda b, g, i: (b, g, i, 0))],
        out_shape=[jax.ShapeDtypeStruct((BATCH * SEQ, ATTN_W), f32),
                   jax.ShapeDtypeStruct((BATCH, N_KV, SEQ, LANES), f32)],
        compiler_params=_cparams(("parallel", "parallel", "parallel")),
        name="cmp_attn_prompt",
    )(q, kd, vl, vh, mt)


VT_ROWS = HEAD_DIM + 16
SEL_KT = 4

def _flash_tile(qs, k, vt, mask, m, acc):
    s = _nt(k, qs)
    if mask is not None:
        s = jnp.where(mask, s, NEG)
    m_next = jnp.maximum(m, jnp.max(s, axis=0, keepdims=True))
    alpha = jnp.exp2(m - m_next)
    p = jnp.exp2(s - m_next).astype(bf16)
    pv = jnp.concatenate([jnp.dot(vt, p[:, r * TQ:(r + 1) * TQ], preferred_element_type=f32)
                          for r in range(Q_PER_KV)], axis=1)
    return m_next, alpha * acc + pv


def _flash_finish(acc, o_ref):
    o_t = acc[:HEAD_DIM] * (1.0 / acc[HEAD_DIM:HEAD_DIM + 1])
    heads = jnp.concatenate([o_t[:, r * TQ:(r + 1) * TQ] for r in range(Q_PER_KV)], axis=0)
    o_ref[...] = heads.T


LOG2E = math.log2(math.e)


def _stack_queries(q_ref, qs_ref):
    for r in range(Q_PER_KV):
        q2 = q_ref[:, (r // 2) * LANES:(r // 2 + 1) * LANES] * (ATTN_SCALE * LOG2E)
        qs_ref[r * TQ:(r + 1) * TQ, 0:LANES] = jnp.where(_half_mask((TQ, LANES), r % 2), q2, 0.0).astype(bf16)


def _tile_masks():
    key = lax.broadcasted_iota(jnp.int32, (TQ, Q_PER_KV * TQ), 0)
    qry = lax.broadcasted_iota(jnp.int32, (TQ, Q_PER_KV * TQ), 1) & (TQ - 1)
    return key, qry


def _flash_init():
    return jnp.full((1, Q_PER_KV * TQ), NEG, f32), jnp.zeros((VT_ROWS, Q_PER_KV * TQ), f32)


def _sel_attn_kernel(q_ref, selm_ref, kd_ref, oh_ref, vt_ref, o_ref, qs_ref):
    i = pl.program_id(2)
    _stack_queries(q_ref, qs_ref)
    selm = selm_ref[...].astype(bf16)
    for r in range(Q_PER_KV):
        qs_ref[r * TQ:(r + 1) * TQ, LANES:2 * LANES] = selm
    qs = qs_ref[...]
    key, qry = _tile_masks()

    def tile(kt, n, mask, m, acc):
        rows = pl.ds(pl.multiple_of(kt * TQ, TQ), n * TQ)
        k = jnp.concatenate([kd_ref[rows, :], oh_ref[rows, :]], axis=1)
        return _flash_tile(qs, k, _with_ones(vt_ref[:, rows]), mask, m, acc)

    m, acc = tile(i, 1, key <= qry, *_flash_init())

    def body(t, carry):
        return tile(SEL_KT * t, SEL_KT, None, *carry)

    m, acc = lax.fori_loop(0, i // SEL_KT, body, (m, acc))
    done = (i // SEL_KT) * SEL_KT
    _, acc = lax.fori_loop(done, i, lambda kt, carry: tile(kt, 1, None, *carry), (m, acc))
    _flash_finish(acc, o_ref)


def _win_attn_kernel(q_ref, kd_ref, vt_ref, o_ref, qs_ref, acc_ref):
    i = pl.program_id(2)
    _stack_queries(q_ref, qs_ref)
    qs = qs_ref[...]

    def span(first, n):
        rows = pl.ds(pl.multiple_of(first * TQ, TQ), n * TQ)
        key = lax.broadcasted_iota(jnp.int32, (n * TQ, Q_PER_KV * TQ), 0)
        qry = (lax.broadcasted_iota(jnp.int32, (n * TQ, Q_PER_KV * TQ), 1) & (TQ - 1)) + (n - 1) * TQ
        dist = qry - key
        _, acc = _flash_tile(qs, kd_ref[rows, :], _with_ones(vt_ref[:, rows]), (dist >= 0) & (dist < WINDOW),
                             *_flash_init())
        acc_ref[...] = acc

    n_back = WINDOW // TQ

    @pl.when(i >= n_back)
    def _():
        span(i - n_back, n_back + 1)

    for early in range(n_back):
        @pl.when(i == early)
        def _(early=early):
            span(0, early + 1)

    _flash_finish(acc_ref[...], o_ref)


def _with_ones(vt):
    return jnp.concatenate([vt.astype(bf16), jnp.ones((VT_ROWS - HEAD_DIM, vt.shape[1]), bf16)], axis=0)


def _prompt_attn_specs():
    nq = SEQ // TQ
    q_spec = pl.BlockSpec((TQ, 2 * LANES), lambda b, g, i: (b * nq + i, g))
    kd_spec = pl.BlockSpec((SEQ, LANES), lambda b, g, i: (b, g))
    vt_spec = pl.BlockSpec((None, HEAD_DIM, SEQ), lambda b, g, i: (b, N_KV + g, 0))
    return nq, q_spec, kd_spec, vt_spec


def _sel_attn_prompt(q, selm, kd, kv_t):
    nq, q_spec, kd_spec, vt_spec = _prompt_attn_specs()
    blk = np.arange(SEQ)[:, None] // SEL_BLOCK == np.arange(LANES)[None, :]
    oh = jnp.asarray(blk.astype(np.float32), bf16)
    return pl.pallas_call(
        _sel_attn_kernel,
        grid=(BATCH, N_KV, nq),
        in_specs=[q_spec, pl.BlockSpec((None, None, TQ, LANES), lambda b, g, i: (b, g, i, 0)),
                  kd_spec, pl.BlockSpec((SEQ, LANES), lambda b, g, i: (0, 0)), vt_spec],
        out_specs=q_spec,
        out_shape=jax.ShapeDtypeStruct((BATCH * SEQ, ATTN_W), f32),
        scratch_shapes=[pltpu.VMEM((Q_PER_KV * TQ, 2 * LANES), bf16)],
        compiler_params=_cparams(("parallel", "parallel", "parallel")),
        name="sel_attn_prompt",
    )(q, selm, kd, oh, kv_t)


def _win_attn_prompt(q, kd, kv_t):
    nq, q_spec, kd_spec, vt_spec = _prompt_attn_specs()
    return pl.pallas_call(
        _win_attn_kernel,
        grid=(BATCH, N_KV, nq),
        in_specs=[q_spec, kd_spec, vt_spec],
        out_specs=q_spec,
        out_shape=jax.ShapeDtypeStruct((BATCH * SEQ, ATTN_W), f32),
        scratch_shapes=[pltpu.VMEM((Q_PER_KV * TQ, LANES), bf16), pltpu.VMEM((VT_ROWS, Q_PER_KV * TQ), f32)],
        compiler_params=_cparams(("parallel", "parallel", "parallel")),
        name="win_attn_prompt",
    )(q, kd, kv_t)


SSD_Q = 256
HALF_INNER = D_INNER // N_SSM_GROUPS
BC_W = N_SSM_GROUPS * D_STATE


def _mamba_consts(conv_w, conv_b, dt_bias, a_log, d_skip, ssm_norm):
    pad = lambda v: jnp.zeros((1, GD_W), f32).at[0, DT_LANE0:DT_LANE0 + N_SSM_HEADS].set(v)
    e16 = np.zeros((GD_W, D_INNER), np.float32)
    for h in range(N_SSM_HEADS):
        e16[DT_LANE0 + h, h * SSM_HEAD_DIM:(h + 1) * SSM_HEAD_DIM] = 1.0
    tri = np.tril(np.ones((SSD_Q, SSD_Q), np.float32))
    return (conv_w, conv_b[None], pad(dt_bias), pad(a_log), jnp.asarray(e16, bf16),
            jnp.repeat(d_skip, SSM_HEAD_DIM)[None], ssm_norm[None], jnp.asarray(tri, bf16))


def _dt_and_decay(gd, dtb_ref, alog_ref):
    lane = lax.broadcasted_iota(jnp.int32, gd.shape, 1)
    live = (lane >= DT_LANE0) & (lane < DT_LANE0 + N_SSM_HEADS)
    dt = jnp.where(live, _softplus(gd + dtb_ref[...]), 0.0)
    return dt, dt * (-jnp.exp(alog_ref[...]))


def _gated_group_norm(y, z, nw_ref):
    y = y * _silu(z)
    outs = []
    for g in range(N_SSM_GROUPS):
        yg = y[:, g * HALF_INNER:(g + 1) * HALF_INNER]
        outs.append(yg * lax.rsqrt(jnp.mean(yg * yg, axis=-1, keepdims=True) + NORM_EPS))
    return jnp.concatenate(outs, axis=1) * nw_ref[...]


def _mamba_prompt_kernel(z_ref, xbc_ref, gd_ref, cw_ref, cb_ref, dtb_ref, alog_ref, e16_ref, dsk_ref, nw_ref, tri_ref,
                         y_ref, st_ref, xpad_ref, state_ref):
    c = pl.program_id(1)
    nq = SSD_Q

    @pl.when(c == 0)
    def _():
        state_ref[...] = jnp.zeros(state_ref.shape, f32)
        xpad_ref[0:8, :] = jnp.zeros((8, CONV_DIM), f32)

    xpad_ref[8:8 + nq, :] = xbc_ref[...]
    conv = cb_ref[...]
    for w in range(CONV_W):
        conv = conv + xpad_ref[8 - (CONV_W - 1) + w:8 - (CONV_W - 1) + w + nq, :] * cw_ref[w:w + 1, :]
    xpad_ref[0:8, :] = xpad_ref[nq:nq + 8, :]
    act = _silu(conv)
    xs, bm, cm = act[:, :D_INNER], act[:, D_INNER:D_INNER + BC_W], act[:, D_INNER + BC_W:]

    dt, a = _dt_and_decay(gd_ref[...], dtb_ref, alog_ref)
    a_cs = sum(jnp.dot(tri_ref[...], t, preferred_element_type=f32) for t in _split3(a))
    ea = jnp.exp(a_cs)
    te = jnp.exp(a_cs[nq - 1:nq, :] - a_cs)
    e16 = e16_ref[...]
    dt_x, ea_x, te_x = _dot_exact(dt, e16), _dot_exact(ea, e16), _dot_exact(te, e16)
    xdt = xs * dt_x
    xw = (xdt * te_x).astype(bf16)
    a_cst = a_cs.T
    tril = lax.broadcasted_iota(jnp.int32, (nq, nq), 1) <= lax.broadcasted_iota(jnp.int32, (nq, nq), 0)

    ys = []
    for g in range(N_SSM_GROUPS):
        bg = bm[:, g * D_STATE:(g + 1) * D_STATE]
        cgb = cm[:, g * D_STATE:(g + 1) * D_STATE].astype(bf16)
        cb = _nt(cgb, bg.astype(bf16))
        st = state_ref[g]
        yoff = jnp.dot(cgb, st.astype(bf16), preferred_element_type=f32)
        gl = slice(g * HALF_INNER, (g + 1) * HALF_INNER)
        state_ref[g] = ea_x[nq - 1:nq, gl] * st + jnp.dot(bg.T.astype(bf16), xw[:, gl], preferred_element_type=f32)
        for hp in range(HALF_INNER // LANES):
            pl_ = slice(g * HALF_INNER + hp * LANES, g * HALF_INNER + (hp + 1) * LANES)
            xp = xdt[:, pl_]
            yp = ea_x[:, pl_] * yoff[:, hp * LANES:(hp + 1) * LANES] + dsk_ref[:, pl_] * xs[:, pl_]
            for e in range(2):
                lane_h = DT_LANE0 + g * (N_SSM_HEADS // N_SSM_GROUPS) + hp * 2 + e
                seg = a_cs[:, lane_h:lane_h + 1] - a_cst[lane_h:lane_h + 1, :]
                wgt = (cb * jnp.exp(jnp.where(tril, seg, NEG))).astype(bf16)
                xz = jnp.where(_half_mask((nq, LANES), e), xp, 0.0).astype(bf16)
                yp = yp + jnp.dot(wgt, xz, preferred_element_type=f32)
            ys.append(yp)
    y = jnp.concatenate(ys, axis=1)
    y_ref[...] = _gated_group_norm(y, z_ref[...], nw_ref).astype(bf16)

    @pl.when(c == pl.num_programs(1) - 1)
    def _():
        st_ref[...] = state_ref[...]


def _mamba_prompt(z, xbc, gd, mc):
    nc = SEQ // SSD_Q
    row = lambda w: pl.BlockSpec((SSD_Q, w), lambda b, c: (b * nc + c, 0))
    y, st = pl.pallas_call(
        _mamba_prompt_kernel,
        grid=(BATCH, nc),
        in_specs=[row(D_INNER), row(CONV_DIM), row(GD_W)] + [_const_spec(a, 2) for a in mc],
        out_specs=[row(D_INNER), pl.BlockSpec((None, N_SSM_GROUPS, D_STATE, HALF_INNER), lambda b, c: (b, 0, 0, 0))],
        out_shape=[jax.ShapeDtypeStruct((BATCH * SEQ, D_INNER), bf16),
                   jax.ShapeDtypeStruct((BATCH, N_SSM_GROUPS, D_STATE, HALF_INNER), f32)],
        scratch_shapes=[pltpu.VMEM((SSD_Q + 8, CONV_DIM), f32), pltpu.VMEM((N_SSM_GROUPS, D_STATE, HALF_INNER), f32)],
        compiler_params=_cparams(("parallel", "arbitrary")),
        name="mamba_prompt",
    )(z, xbc, gd, *mc)
    hpg = N_SSM_HEADS // N_SSM_GROUPS
    st = st.reshape(BATCH, N_SSM_GROUPS, D_STATE, hpg, SSM_HEAD_DIM).transpose(0, 1, 3, 4, 2)
    return y, st.reshape(BATCH, N_SSM_HEADS, SSM_HEAD_DIM, D_STATE)


def _mamba_sample_kernel(z_ref, xbc_ref, gd_ref, sc_ref, s_ref, cw_ref, cb_ref, dtb_ref, alog_ref, e16_ref, dsk_ref,
                         nw_ref, y_ref, so_ref):
    conv = cb_ref[...] + xbc_ref[...] * cw_ref[CONV_W - 1:CONV_W, :]
    for w in range(CONV_W - 1):
        conv = conv + sc_ref[w:w + 1, :] * cw_ref[w:w + 1, :]
    act = _silu(conv)
    xs, bm, cm = act[:, :D_INNER], act[:, D_INNER:D_INNER + BC_W], act[:, D_INNER + BC_W:]
    dt, a = _dt_and_decay(jnp.broadcast_to(gd_ref[...], (8, GD_W)), dtb_ref, alog_ref)
    e16 = e16_ref[...]
    dt_x = _dot_exact(dt, e16)[0:1]
    da_x = _dot_exact(jnp.exp(a), e16)[0:1]
    rows = N_SSM_HEADS * SSM_HEAD_DIM
    xcol = jnp.broadcast_to(xs * dt_x, (D_STATE, rows)).T
    acol = jnp.broadcast_to(da_x, (D_STATE, rows)).T
    rowi = lax.broadcasted_iota(jnp.int32, (rows, D_STATE), 0)
    bfull = jnp.where(rowi < HALF_INNER, bm[:, :D_STATE], bm[:, D_STATE:])
    snew = s_ref[...] * acol + xcol * bfull
    so_ref[...] = snew
    r8 = lax.broadcasted_iota(jnp.int32, (8, D_STATE), 0)
    c8 = jnp.where(r8 == 0, cm[:, :D_STATE], jnp.where(r8 == 1, cm[:, D_STATE:], 0.0)).astype(bf16)
    yy = _nt(c8, snew.astype(bf16))
    lane = lax.broadcasted_iota(jnp.int32, (1, rows), 1)
    y = jnp.where(lane < HALF_INNER, yy[0:1], yy[1:2]) + dsk_ref[...] * xs
    y_ref[...] = _gated_group_norm(y, z_ref[...], nw_ref).astype(bf16)


def _mamba_sample(z, xbc, gd, state_conv, state_ssm, mc):
    nb = z.shape[0]
    mc = mc[:7]
    rows = N_SSM_HEADS * SSM_HEAD_DIM
    one = lambda w: pl.BlockSpec((None, 1, w), lambda b: (b, 0, 0))
    s_spec = pl.BlockSpec((None, rows, D_STATE), lambda b: (b, 0, 0))
    y, so = pl.pallas_call(
        _mamba_sample_kernel,
        grid=(nb,),
        in_specs=[one(D_INNER), one(CONV_DIM), one(GD_W),
                  pl.BlockSpec((None, CONV_W - 1, CONV_DIM), lambda b: (b, 0, 0)), s_spec]
                 + [_const_spec(a, 1) for a in mc],
        out_specs=[one(D_INNER), s_spec],
        out_shape=[jax.ShapeDtypeStruct((nb, 1, D_INNER), bf16), jax.ShapeDtypeStruct((nb, rows, D_STATE), f32)],
        compiler_params=_cparams(("parallel",)),
        name="mamba_sample",
    )(z[:, None], xbc[:, None], gd[:, None], state_conv, state_ssm.reshape(nb, rows, D_STATE), *mc)
    return y[:, 0], so.reshape(nb, N_SSM_HEADS, SSM_HEAD_DIM, D_STATE)


N_CMP_S = PAST_LEN // CMP_STRIDE - 1
N_SLC_S = PAST_LEN // SEL_BLOCK + 1
N_SLC_PAD = 2 * LANES
HALF_PAGE = PAGE_SIZE // SEL_BLOCK
GONE = -3.0e38


def _cmp_attn_sample_kernel(qz_ref, kd_ref, vl_ref, vh_ref, mts_ref, o_ref, imp_ref):
    qz = (qz_ref[...] * ATTN_SCALE).astype(bf16)
    ncmp = kd_ref.shape[0]
    nidx = lax.broadcasted_iota(jnp.int32, (N_HEADS, ncmp), 1)
    rowh = lax.broadcasted_iota(jnp.int32, (N_HEADS, ncmp), 0)
    mask = (nidx < N_CMP_S) & (nidx * CMP_STRIDE + (CMP_BLOCK - 1) <= PAST_LEN)
    rowo = lax.broadcasted_iota(jnp.int32, (N_HEADS, LANES), 0)
    r8 = lax.broadcasted_iota(jnp.int32, (8, N_SLC_PAD), 0)
    o_acc = jnp.zeros((N_HEADS, LANES), f32)
    imp = jnp.zeros((8, N_SLC_PAD), f32)
    for g in range(N_KV):
        gl = slice(g * LANES, (g + 1) * LANES)
        s = jnp.where(mask, _nt(qz, kd_ref[:, gl]), NEG)
        m = jnp.max(s, axis=1, keepdims=True)
        ex = jnp.where(mask, jnp.exp(s - m), 0.0)
        p = ex * (1.0 / jnp.maximum(jnp.sum(ex, axis=1, keepdims=True), 1e-30))
        p = jnp.where(rowh // Q_PER_KV == g, p, 0.0)
        pb = p.astype(bf16)
        o_l = jnp.dot(pb, vl_ref[:, gl], preferred_element_type=f32)
        o_h = jnp.dot(pb, vh_ref[:, gl], preferred_element_type=f32)
        o_acc = o_acc + jnp.where(rowo % 2 == 0, o_l, o_h)
        hi, lo = _split2(p)
        imp_h = jnp.dot(hi, mts_ref[...], preferred_element_type=f32) + jnp.dot(lo, mts_ref[...], preferred_element_type=f32)
        imp = imp + jnp.where(r8 == g, jnp.sum(imp_h, axis=0, keepdims=True), 0.0)
    o_ref[...] = o_acc
    imp_ref[...] = imp


def _topk_sample_kernel(imp_ref, idx_ref):
    rows = imp_ref.shape[0]
    j = lax.broadcasted_iota(jnp.int32, (rows, N_SLC_PAD), 1)
    jt = PAST_LEN // SEL_BLOCK
    imp = imp_ref[...]
    imp = jnp.where((j == 0) | (j == jt) | (j == jt - 1), SELECT_FORCE, imp)
    imp = jnp.where(j > jt, -SELECT_FORCE, imp)
    imp = jnp.where(j >= N_SLC_S, NEG, imp)
    jf = j.astype(f32)
    lane = lax.broadcasted_iota(jnp.int32, (rows, LANES), 1)
    picked = jnp.zeros((rows, LANES), f32)
    for k in range(SEL_TOPN):
        m = jnp.max(imp, axis=1, keepdims=True)
        ix = jnp.min(jnp.where(imp == m, jf, float(N_SLC_PAD)), axis=1, keepdims=True)
        picked = jnp.where(lane == k, ix, picked)
        imp = jnp.where(jf == ix, GONE, imp)
    idx_ref[...] = picked.astype(jnp.int32)


def _cmp_attn_sample(qz, kd, vl, vh):
    nb, ncmp = kd.shape[0], kd.shape[1]
    mts = np.zeros((ncmp, N_SLC_PAD), np.float32)
    mts[:, :N_SLC_S] = _cmp_to_sel_matrix_t(ncmp, N_SLC_S, N_CMP_S).T
    mts = jnp.asarray(mts, bf16)
    kv_spec = pl.BlockSpec((None, ncmp, KV_W), lambda b: (b, 0, 0))
    o16, imp = pl.pallas_call(
        _cmp_attn_sample_kernel,
        grid=(nb,),
        in_specs=[pl.BlockSpec((None, N_HEADS, LANES), lambda b: (b, 0, 0)), kv_spec, kv_spec, kv_spec,
                  pl.BlockSpec(mts.shape, lambda b: (0, 0))],
        out_specs=[pl.BlockSpec((None, N_HEADS, LANES), lambda b: (b, 0, 0)),
                   pl.BlockSpec((None, 8, N_SLC_PAD), lambda b: (b, 0, 0))],
        out_shape=[jax.ShapeDtypeStruct((nb, N_HEADS, LANES), f32), jax.ShapeDtypeStruct((nb, 8, N_SLC_PAD), f32)],
        compiler_params=_cparams(("parallel",)),
        name="cmp_attn_sample",
    )(qz, kd, vl, vh, mts)
    idx = pl.pallas_call(
        _topk_sample_kernel,
        out_shape=jax.ShapeDtypeStruct((nb * 8, LANES), jnp.int32),
        name="topk_sample",
    )(imp.reshape(nb * 8, N_SLC_PAD))
    return o16, idx.reshape(nb, 8, LANES)


def _one_query_softmax(s, s_new, vt, v_new):
    m = jnp.maximum(jnp.max(s, axis=1, keepdims=True), s_new)
    p = jnp.exp(s - m)
    p_new = jnp.exp(s_new - m)
    den = jnp.sum(p, axis=1, keepdims=True) + p_new
    num = _nt(p.astype(bf16), vt) + p_new * v_new
    return num * (1.0 / den)


def _selwin_sample_kernel(idx_ref, pt_ref, *refs):
    del pt_ref
    kv = refs[:SEL_TOPN]
    q_ref, ksn_ref, win_ref, kwn_ref, os_ref, ow_ref = refs[SEL_TOPN:]
    b, g = pl.program_id(0), pl.program_id(1)
    q8 = q_ref[...] * ATTN_SCALE
    qb = q8.astype(bf16)
    kcat = jnp.concatenate([r[0] for r in kv], axis=1).astype(bf16)
    vcat = jnp.concatenate([r[1] for r in kv], axis=1).astype(bf16)
    n = SEL_TOPN * PAGE_SIZE
    lane = lax.broadcasted_iota(jnp.int32, (8, n), 1)
    slot = lane // PAGE_SIZE
    blk = jnp.zeros((8, n), jnp.int32)
    for k in range(SEL_TOPN):
        blk = jnp.where(slot == k, idx_ref[(b * N_KV + g) * SEL_TOPN + k], blk)
    live = (blk < N_SLC_S - 1) & ((lane % PAGE_SIZE) // SEL_BLOCK == blk % HALF_PAGE)
    s = jnp.where(live, jnp.dot(qb, kcat, preferred_element_type=f32), NEG)
    k_new, v_new = ksn_ref[pl.ds(g, 1), :], ksn_ref[pl.ds(N_KV + g, 1), :]
    s_new = jnp.sum(q8 * k_new, axis=1, keepdims=True)
    os_ref[...] = _one_query_softmax(s, s_new, vcat, v_new)
    nbuf = win_ref.shape[-1]
    i = lax.broadcasted_iota(jnp.int32, (8, nbuf), 1)
    s = jnp.where(nbuf - i < WINDOW, jnp.dot(qb, win_ref[0].astype(bf16), preferred_element_type=f32), NEG)
    k_new, v_new = kwn_ref[pl.ds(g, 1), :], kwn_ref[pl.ds(N_KV + g, 1), :]
    s_new = jnp.sum(q8 * k_new, axis=1, keepdims=True)
    ow_ref[...] = _one_query_softmax(s, s_new, win_ref[1].astype(bf16), v_new)


def _selwin_sample(idx, page_table, cache_s, q_s, ks_new, win_buf, kw_new):
    nb = q_s.shape[0]
    nbuf = win_buf.shape[1]
    pages = cache_s.transpose(0, 2, 3, 4, 1)
    wb = win_buf.transpose(0, 2, 3, 4, 1)
    qg = q_s.reshape(nb, N_KV, Q_PER_KV, HEAD_DIM)
    q8 = jnp.concatenate([qg, jnp.zeros_like(qg)], axis=2)

    def blk_map(b, g, idx_ref, pt_ref, k):
        j = jnp.minimum(idx_ref[(b * N_KV + g) * SEL_TOPN + k], N_SLC_S - 2)
        return (pt_ref[b * N_PAGES + j // HALF_PAGE], 0, g, 0, 0)

    blk_specs = [pl.BlockSpec((None, 2, None, HEAD_DIM, PAGE_SIZE), functools.partial(blk_map, k=k))
                 for k in range(SEL_TOPN)]
    new_spec = pl.BlockSpec((None, 2 * N_KV, HEAD_DIM), lambda b, g, *_: (b, 0, 0))
    win_spec = pl.BlockSpec((None, 2, None, HEAD_DIM, nbuf), lambda b, g, *_: (b, 0, g, 0, 0))
    o_spec = pl.BlockSpec((None, None, 8, HEAD_DIM), lambda b, g, *_: (b, g, 0, 0))
    grid_spec = pltpu.PrefetchScalarGridSpec(
        num_scalar_prefetch=2,
        grid=(nb, N_KV),
        in_specs=blk_specs + [o_spec, new_spec, win_spec, new_spec],
        out_specs=[o_spec, o_spec])
    o_shape = jax.ShapeDtypeStruct((nb, N_KV, 8, HEAD_DIM), f32)
    ks3, kw3 = ks_new.reshape(nb, 2 * N_KV, HEAD_DIM), kw_new.reshape(nb, 2 * N_KV, HEAD_DIM)
    os8, ow8 = pl.pallas_call(
        _selwin_sample_kernel,
        grid_spec=grid_spec,
        out_shape=[o_shape, o_shape],
        compiler_params=_cparams(("arbitrary", "arbitrary")),
        name="selwin_sample",
    )(idx.reshape(-1), page_table.reshape(-1), *([pages] * SEL_TOPN), q8, ks3, wb, kw3)
    return os8[:, :, :Q_PER_KV].reshape(nb, ATTN_W), ow8[:, :, :Q_PER_KV].reshape(nb, ATTN_W)


def _sample_query_layouts(q):
    nb = q.shape[0]
    q16 = q.reshape(nb, N_HEADS, HEAD_DIM)
    z = jnp.zeros_like(q16)
    lo, hi = jnp.concatenate([q16, z], axis=-1), jnp.concatenate([z, q16], axis=-1)
    h = jnp.arange(N_HEADS)[None, :, None]
    return jnp.where(h % 2 == 0, lo, hi)


def _unpad_heads_by_parity(o16):
    nb = o16.shape[0]
    o = o16.reshape(nb, N_HEADS // 2, 2, 2, HEAD_DIM)
    return jnp.stack([o[:, :, 0, 0], o[:, :, 1, 1]], axis=2).reshape(nb, ATTN_W)


def _finish(x, ocmp, osel, owin, gd, ssd_y, w_out, ln_mlp, w_up, w_down, ln_final, tm, tm_mlp, tf):
    x1 = _outproj(x, ocmp, osel, owin, gd, ssd_y, _gate_expand(), w_out, tm)
    return _mlp(x1, ln_mlp, w_up, w_down, ln_final, tm_mlp, tf)


def kernel(x_prompt, x_sample, cache_kv_cmp, cache_kv_sel, state_kv_win, state_conv, state_ssm, page_table, ln_mix, w_in, cmp_pe, cmp_w1, cmp_b1, cmp_w2, cmp_b2, conv_w, conv_b, dt_bias, a_log, d_skip, ssm_norm, w_out, ln_mlp, w_up, w_down, ln_final):
    nb = x_sample.shape[0]
    w_in_p = _prep_w_in(w_in[0])
    w_out_b, w_up_b, w_down_b = w_out[0].astype(bf16), w_up[0].astype(bf16), w_down[0].astype(bf16)
    lnw, lnm, lnf = ln_mix[0][None], ln_mlp[0][None], ln_final[None]
    cw = _compress_weights(cmp_pe[0], cmp_w1[0], cmp_b1[0], cmp_w2[0], cmp_b2[0])
    mc = _mamba_consts(conv_w[0], conv_b[0], dt_bias[0], a_log[0], d_skip[0], ssm_norm[0])
    kv_shape = (2, N_KV, HEAD_DIM)

    xp = x_prompt.reshape(BATCH * SEQ, D_MODEL)
    q, z, xbc, gd, kc, kct, kst, kwt, kds, kdw = _inproj_prompt(xp, lnw, w_in_p, 256)
    kd, vl, vh = _compress_prompt(kc, cw)
    ocmp, selm = _cmp_attn_prompt(q, kd, vl, vh)
    osel = _sel_attn_prompt(q, selm, kds, kst)
    owin = _win_attn_prompt(q, kdw, kwt)
    ssd_y, ssm_p = _mamba_prompt(z, xbc, gd, mc)
    y_prompt = _finish(xp, ocmp, osel, owin, gd, ssd_y, w_out_b, lnm, w_up_b, w_down_b, lnf, 256, 512, 1024)

    xs = x_sample.reshape(nb, D_MODEL)
    q_s, z_s, xbc_s, gd_s, kc_s, ks_s, kw_s = _inproj(xs, lnw, w_in_p, nb)
    kd_s, vl_s, vh_s = _compress_sample(cache_kv_cmp[0], page_table, cw)
    ocmp16, idx8 = _cmp_attn_sample(_sample_query_layouts(q_s), kd_s, vl_s, vh_s)
    idx = idx8[:, :N_KV, :SEL_TOPN]
    osel_s, owin_s = _selwin_sample(idx, page_table, cache_kv_sel[0], q_s, ks_s, state_kv_win[0], kw_s)
    ssd_y_s, ssm_s = _mamba_sample(z_s, xbc_s, gd_s, state_conv[0], state_ssm[0], mc)
    y_sample = _finish(xs, _unpad_heads_by_parity(ocmp16), osel_s, owin_s, gd_s, ssd_y_s,
                       w_out_b, lnm, w_up_b, w_down_b, lnf, nb, nb, 1024)

    def token_major(t):
        return t.reshape(BATCH, *kv_shape, t.shape[-1]).transpose(0, 4, 1, 2, 3)[None]

    conv_p = xbc.reshape(BATCH, SEQ, CONV_DIM)[:, -(CONV_W - 1):]
    kv_win_s = jnp.concatenate([state_kv_win[0], kw_s.reshape(nb, 1, *kv_shape)], axis=1)[:, -WINDOW:]
    conv_s = jnp.concatenate([state_conv[0], xbc_s[:, None]], axis=1)[:, -(CONV_W - 1):]
    return (y_prompt.reshape(BATCH, SEQ, D_MODEL), y_sample.reshape(nb, 1, D_MODEL),
            token_major(kct), token_major(kst), token_major(kwt[:, :, -min(WINDOW, SEQ):]),
            conv_p[None], ssm_p[None],
            kc_s.reshape(1, nb, 1, *kv_shape), ks_s.reshape(1, nb, 1, *kv_shape), kv_win_s[None],
            conv_s[None], ssm_s[None])
```

```python
import functools
import math

import jax
import jax.numpy as jnp
import numpy as np
from jax import lax
from jax.experimental import pallas as pl
from jax.experimental.pallas import tpu as pltpu

f32 = jnp.float32
bf16 = jnp.bfloat16

D_MODEL = 2048
BATCH = 2
SEQ = 4096
DEC_BATCH = 32
PAST_LEN = 8192
PAGE_SIZE = 128
HEAD_DIM = 64
N_HEADS = 16
N_KV = 4
Q_PER_KV = 4
ATTN_W = 1024
KV_W = 512
CMP_BLOCK = 32
CMP_STRIDE = 16
CMP_HIDDEN = 128
SEL_BLOCK = 64
SEL_TOPN = 16
WINDOW = 512
SELECT_FORCE = 1.0e4
ATTN_SCALE = HEAD_DIM ** -0.5
D_INNER = 1024
SSM_HEAD_DIM = 64
N_SSM_HEADS = 16
N_SSM_GROUPS = 2
D_STATE = 128
CONV_W = 4
CONV_DIM = D_INNER + 2 * N_SSM_GROUPS * D_STATE
D_FF = 4 * D_MODEL
IN_SPLITS = (ATTN_W, KV_W, KV_W, KV_W, 3 * N_HEADS, D_INNER, CONV_DIM, N_SSM_HEADS)
NORM_EPS = 1e-5

LANES = 128
GD_W = LANES
DT_LANE0 = 3 * N_HEADS
N_PAGES = PAST_LEN // PAGE_SIZE
SEG_PER_PAGE = PAGE_SIZE // CMP_STRIDE
SEG_W = CMP_STRIDE * KV_W
NEG = -1.0e30
SEL_NEG = -1.0e9
VMEM_LIMIT = 56 * 1024 * 1024


def _cparams(sem):
    return pltpu.CompilerParams(dimension_semantics=sem, vmem_limit_bytes=VMEM_LIMIT)


def _nt(a, b):
    return lax.dot_general(a, b, (((1,), (1,)), ((), ())), preferred_element_type=f32)


def _split2(x):
    hi = x.astype(bf16)
    lo = (x - hi.astype(f32)).astype(bf16)
    return hi, lo


def _split3(x):
    hi = x.astype(bf16)
    r = x - hi.astype(f32)
    mid = r.astype(bf16)
    lo = (r - mid.astype(f32)).astype(bf16)
    return hi, mid, lo


def _dot_exact(x, w):
    return sum(jnp.dot(t, w, preferred_element_type=f32) for t in _split3(x))


def _silu(x):
    return x * (1.0 / (1.0 + jnp.exp(-x)))


def _sigmoid(x):
    return 1.0 / (1.0 + jnp.exp(-x))


def _softplus(x):
    return jnp.maximum(x, 0.0) + jnp.log1p(jnp.exp(-jnp.abs(x)))


IN_OUT_WIDTHS = (ATTN_W, D_INNER, CONV_DIM, GD_W, KV_W, KV_W, KV_W)
KD_W = N_KV * LANES
IN_COLS_ROWMAJOR = sum(IN_OUT_WIDTHS)


def _normed(x_ref, lnw_ref):
    x = x_ref[...]
    return (x * lax.rsqrt(jnp.mean(x * x, axis=-1, keepdims=True) + NORM_EPS) * lnw_ref[...]).astype(bf16)


def _inproj_kernel(x_ref, lnw_ref, w_ref, *out_refs):
    h = _normed(x_ref, lnw_ref)
    off = 0
    for ref in out_refs:
        n = ref.shape[-1]
        ref[...] = jnp.dot(h, w_ref[:, off:off + n], preferred_element_type=f32)
        off += n


def _inproj_prompt_kernel(x_ref, lnw_ref, w_ref, q_ref, z_ref, xbc_ref, gd_ref, kc_ref,
                          kct_ref, kst_ref, kwt_ref, kds_ref, kdw_ref, t_ref):
    h = _normed(x_ref, lnw_ref)
    off = 0

    def proj(n):
        nonlocal off
        y = jnp.dot(h, w_ref[:, off:off + n], preferred_element_type=f32)
        off += n
        return y

    for ref in (q_ref, z_ref, xbc_ref, gd_ref):
        ref[...] = proj(ref.shape[-1])
    kc_ref[...] = proj(KV_W)
    kct_ref[...] = kc_ref[...].T
    for t_out in (kst_ref, kwt_ref):
        t_ref[...] = proj(KV_W)
        t_out[...] = t_ref[...].T
    kds_ref[...] = proj(KD_W).astype(bf16)
    kdw_ref[...] = proj(KD_W).astype(bf16)


def _inproj(x, lnw, w_perm, tm):
    m = x.shape[0]
    return pl.pallas_call(
        _inproj_kernel,
        grid=(m // tm,),
        in_specs=[pl.BlockSpec((tm, D_MODEL), lambda i: (i, 0)),
                  pl.BlockSpec((1, D_MODEL), lambda i: (0, 0)),
                  pl.BlockSpec((D_MODEL, IN_COLS_ROWMAJOR), lambda i: (0, 0), pipeline_mode=pl.Buffered(1))],
        out_specs=[pl.BlockSpec((tm, n), lambda i: (i, 0)) for n in IN_OUT_WIDTHS],
        out_shape=[jax.ShapeDtypeStruct((m, n), f32) for n in IN_OUT_WIDTHS],
        compiler_params=_cparams(("parallel",)),
        name="inproj",
    )(x, lnw, w_perm)


def _inproj_prompt(x, lnw, w_perm, tm):
    nq = SEQ // tm
    row = lambda n: pl.BlockSpec((tm, n), lambda i: (i, 0))
    tmin = pl.BlockSpec((None, KV_W, tm), lambda i: (i // nq, 0, i % nq))
    rm = lambda n, dt=f32: jax.ShapeDtypeStruct((BATCH * SEQ, n), dt)
    tshape = jax.ShapeDtypeStruct((BATCH, KV_W, SEQ), f32)
    return pl.pallas_call(
        _inproj_prompt_kernel,
        grid=(BATCH * nq,),
        in_specs=[pl.BlockSpec((tm, D_MODEL), lambda i: (i, 0)),
                  pl.BlockSpec((1, D_MODEL), lambda i: (0, 0)),
                  pl.BlockSpec(w_perm.shape, lambda i: (0, 0), pipeline_mode=pl.Buffered(1))],
        out_specs=[row(ATTN_W), row(D_INNER), row(CONV_DIM), row(GD_W), row(KV_W), tmin, tmin, tmin, row(KD_W), row(KD_W)],
        out_shape=[rm(ATTN_W), rm(D_INNER), rm(CONV_DIM), rm(GD_W), rm(KV_W), tshape, tshape, tshape,
                   rm(KD_W, bf16), rm(KD_W, bf16)],
        scratch_shapes=[pltpu.VMEM((tm, KV_W), f32)],
        compiler_params=_cparams(("parallel",)),
        name="inproj_prompt",
    )(x, lnw, w_perm)


def _prep_w_in(w_in):
    parts, off = [], 0
    for width in IN_SPLITS:
        parts.append(w_in[:, off:off + width])
        off += width
    q, kc, ks, kw, g, z, xbc, dt = parts
    gd = jnp.concatenate([g, dt, jnp.zeros((D_MODEL, GD_W - 4 * N_HEADS), w_in.dtype)], axis=1)

    def dup_keys(kv):
        k = kv[:, :KV_W // 2].reshape(D_MODEL, N_KV, 1, HEAD_DIM)
        return jnp.broadcast_to(k, (D_MODEL, N_KV, 2, HEAD_DIM)).reshape(D_MODEL, KD_W)

    return jnp.concatenate([q, z, xbc, gd, kc, ks, kw, dup_keys(ks), dup_keys(kw)], axis=1).astype(bf16)


def _outproj_kernel(x_ref, oc_ref, os_ref, ow_ref, gd_ref, y_ref, eg_ref, w_ref, o_ref):
    gates = _sigmoid(gd_ref[...])
    hi, lo = _split2(gates)
    attn = None
    for c, br in enumerate((oc_ref, os_ref, ow_ref)):
        ge = (jnp.dot(hi, eg_ref[c], preferred_element_type=f32)
              + jnp.dot(lo, eg_ref[c], preferred_element_type=f32))
        term = ge * br[...]
        attn = term if attn is None else attn + term
    mix = (jnp.dot(attn.astype(bf16), w_ref[:ATTN_W, :], preferred_element_type=f32)
           + jnp.dot(y_ref[...], w_ref[ATTN_W:, :], preferred_element_type=f32))
    o_ref[...] = x_ref[...] + mix


def _outproj(x, ocmp, osel, owin, gd, ssd_y, eg, w_out, tm):
    m = x.shape[0]
    row = lambda w: pl.BlockSpec((tm, w), lambda i: (i, 0))
    return pl.pallas_call(
        _outproj_kernel,
        grid=(m // tm,),
        in_specs=[row(D_MODEL), row(ATTN_W), row(ATTN_W), row(ATTN_W), row(GD_W), row(D_INNER),
                  pl.BlockSpec((3, GD_W, ATTN_W), lambda i: (0, 0, 0)),
                  pl.BlockSpec((ATTN_W + D_INNER, D_MODEL), lambda i: (0, 0))],
        out_specs=row(D_MODEL),
        out_shape=jax.ShapeDtypeStruct((m, D_MODEL), f32),
        compiler_params=_cparams(("parallel",)),
        name="outproj",
    )(x, ocmp, osel, owin, gd, ssd_y, eg, w_out)


def _gate_expand():
    eg = np.zeros((3, GD_W, ATTN_W), np.float32)
    for h in range(N_HEADS):
        for c in range(3):
            eg[c, h * 3 + c, h * HEAD_DIM:(h + 1) * HEAD_DIM] = 1.0
    return jnp.asarray(eg, bf16)


def _mlp_kernel(x_ref, ln_ref, wu_ref, wd_ref, lnf_ref, o_ref, h_ref, acc_ref):
    k = pl.program_id(1)

    @pl.when(k == 0)
    def _():
        x = x_ref[...]
        h = x * lax.rsqrt(jnp.mean(x * x, axis=-1, keepdims=True) + NORM_EPS) * ln_ref[...]
        h_ref[...] = h.astype(bf16)
        acc_ref[...] = jnp.zeros_like(acc_ref)

    u = jnp.dot(h_ref[...], wu_ref[...], preferred_element_type=f32)
    u = jnp.square(jnp.maximum(u, 0.0)).astype(bf16)
    acc_ref[...] += jnp.dot(u, wd_ref[...], preferred_element_type=f32)

    @pl.when(k == pl.num_programs(1) - 1)
    def _():
        y = x_ref[...] + acc_ref[...]
        y = y * lax.rsqrt(jnp.mean(y * y, axis=-1, keepdims=True) + NORM_EPS) * lnf_ref[...]
        o_ref[...] = y


def _mlp(x, ln_mlp, w_up, w_down, ln_final, tm, tf):
    m = x.shape[0]
    return pl.pallas_call(
        _mlp_kernel,
        grid=(m // tm, D_FF // tf),
        in_specs=[pl.BlockSpec((tm, D_MODEL), lambda i, k: (i, 0)),
                  pl.BlockSpec((1, D_MODEL), lambda i, k: (0, 0)),
                  pl.BlockSpec((D_MODEL, tf), lambda i, k: (0, k)),
                  pl.BlockSpec((tf, D_MODEL), lambda i, k: (k, 0)),
                  pl.BlockSpec((1, D_MODEL), lambda i, k: (0, 0))],
        out_specs=pl.BlockSpec((tm, D_MODEL), lambda i, k: (i, 0)),
        out_shape=jax.ShapeDtypeStruct((m, D_MODEL), f32),
        scratch_shapes=[pltpu.VMEM((tm, D_MODEL), bf16), pltpu.VMEM((tm, D_MODEL), f32)],
        compiler_params=_cparams(("parallel", "arbitrary")),
        name="mlp",
    )(x, ln_mlp, w_up, w_down, ln_final)


N_SLAB_K = CMP_STRIDE * LANES


def _compress_weights(cmp_pe, cmp_w1, cmp_b1, cmp_w2, cmp_b2):
    eye2 = jnp.eye(2, dtype=f32)
    w1 = cmp_w1.reshape(2, 2, CMP_STRIDE, HEAD_DIM, CMP_HIDDEN)
    w1cat = jnp.einsum('crjdf,ab->cjadrbf', w1, eye2).reshape(2, N_SLAB_K, 4 * CMP_HIDDEN).astype(bf16)
    w1flat = cmp_w1.reshape(2, CMP_BLOCK * HEAD_DIM, CMP_HIDDEN).astype(bf16)
    pe8 = jnp.broadcast_to(cmp_pe.reshape(2, 1, CMP_BLOCK * HEAD_DIM), (2, 8, CMP_BLOCK * HEAD_DIM))
    b1 = cmp_b1.reshape(2, 1, CMP_HIDDEN)
    wk, wv = cmp_w2[0], cmp_w2[1]
    zk = jnp.zeros_like(wk)
    w2k = jnp.einsum('fd,ab,e->afbed', wk, eye2, jnp.ones((2,), f32)).reshape(2 * CMP_HIDDEN, 4 * HEAD_DIM)
    b2k = jnp.tile(cmp_b2[0], 4)[None]
    vl = jnp.einsum('fd,ab,e->afbed', wv, eye2, jnp.array([1.0, 0.0], f32)).reshape(2 * CMP_HIDDEN, 4 * HEAD_DIM)
    vh = jnp.einsum('fd,ab,e->afbed', wv, eye2, jnp.array([0.0, 1.0], f32)).reshape(2 * CMP_HIDDEN, 4 * HEAD_DIM)
    w2v = jnp.concatenate([vl, vh], axis=1)
    zb = jnp.zeros((HEAD_DIM,), f32)
    b2v = jnp.concatenate([cmp_b2[1], zb, cmp_b2[1], zb, zb, cmp_b2[1], zb, cmp_b2[1]])[None]
    del zk
    return pe8, w1flat, b1, w1cat, w2k.astype(bf16), b2k, w2v.astype(bf16), b2v


def _compress_slab(a, c, gp, prm_refs, out_refs):
    pe_ref, w1f_ref, b1_ref, w1_ref, w2k_ref, b2k_ref, w2v_ref, b2v_ref = prm_refs
    kd_ref, vl_ref, vh_ref = out_refs
    pet = jnp.dot(pe_ref[c].astype(bf16), w1f_ref[c], preferred_element_type=f32)[0:1] + b1_ref[c]
    bias = jnp.concatenate([pet, pet], axis=1)
    p = jnp.dot(a, w1_ref[c], preferred_element_type=f32)
    s = p.shape[0]
    hid = p[:, :2 * CMP_HIDDEN] + pltpu.roll(p[:, 2 * CMP_HIDDEN:], s - 1, axis=0)
    hid = _silu(hid + bias).astype(bf16)
    w = 4 * HEAD_DIM
    if c == 0:
        kd_ref[:, gp * w:(gp + 1) * w] = (jnp.dot(hid, w2k_ref[...], preferred_element_type=f32)
                                          + b2k_ref[...]).astype(bf16)
    else:
        o = jnp.dot(hid, w2v_ref[...], preferred_element_type=f32) + b2v_ref[...]
        vl_ref[:, gp * w:(gp + 1) * w] = o[:, :w].astype(bf16)
        vh_ref[:, gp * w:(gp + 1) * w] = o[:, w:].astype(bf16)


def _slab_lane0(j, c, gp):
    return j * KV_W + c * (KV_W // 2) + gp * LANES


def _compress_all_slabs(x_ref, prm_refs, out_refs):
    for c in range(2):
        for gp in range(2):
            a = jnp.concatenate(
                [x_ref[:, _slab_lane0(j, c, gp):_slab_lane0(j, c, gp) + LANES] for j in range(CMP_STRIDE)],
                axis=1).astype(bf16)
            _compress_slab(a, c, gp, prm_refs, out_refs)


def _compress_prompt_kernel(x_ref, *refs):
    _compress_all_slabs(x_ref, refs[:8], refs[8:11])
    vl_ref, vt_ref = refs[9], refs[11]
    for g in range(N_KV):
        v_pad = vl_ref[:, g * LANES:(g + 1) * LANES].astype(f32)
        vt_ref[g * HEAD_DIM:(g + 1) * HEAD_DIM, :] = v_pad.T[:HEAD_DIM].astype(bf16)


def _seg_perm():
    m = np.zeros((2 * PAGE_SIZE, 2 * PAGE_SIZE), np.float32)
    for pp in range(2):
        for s in range(SEG_PER_PAGE):
            for j in range(CMP_STRIDE):
                m[j * 2 * SEG_PER_PAGE + pp * SEG_PER_PAGE + s, pp * PAGE_SIZE + s * CMP_STRIDE + j] = 1.0
    return jnp.asarray(m, bf16)


def _compress_sample_kernel(pt_ref, *refs):
    del pt_ref
    pages, perm_ref = refs[:N_PAGES], refs[N_PAGES]
    prm_refs, out_refs, o_ref = refs[N_PAGES + 1:N_PAGES + 9], refs[N_PAGES + 9:N_PAGES + 12], refs[-1]
    rows = 2 * SEG_PER_PAGE
    for k in range(N_PAGES // 2):
        two = jnp.concatenate([pages[2 * k][...], pages[2 * k + 1][...]], axis=1).astype(bf16)
        o_ref[k] = _nt(perm_ref[...], two).astype(bf16)
    nseg = N_PAGES * SEG_PER_PAGE
    for c in range(2):
        for gp in range(2):
            l0 = c * (KV_W // 2) + gp * LANES
            a = jnp.concatenate([o_ref[:, j * rows:(j + 1) * rows, l0:l0 + LANES].reshape(nseg, LANES)
                                 for j in range(CMP_STRIDE)], axis=1)
            _compress_slab(a, c, gp, prm_refs, out_refs)


def _const_spec(a, n_grid, n_prefetch=0):
    zeros = (0,) * a.ndim
    if n_grid == 1:
        return pl.BlockSpec(a.shape, (lambda b, *_: zeros))
    return pl.BlockSpec(a.shape, (lambda b, c, *_: zeros))


def _compress_prompt(kc, cw):
    nseg = SEQ // CMP_STRIDE
    x = kc.reshape(BATCH, nseg, SEG_W)
    out = jax.ShapeDtypeStruct((BATCH, nseg, KV_W), bf16)
    vt_rows = N_KV * HEAD_DIM
    return pl.pallas_call(
        _compress_prompt_kernel,
        grid=(BATCH,),
        in_specs=[pl.BlockSpec((None, nseg, SEG_W), lambda b: (b, 0, 0))] + [_const_spec(a, 1) for a in cw],
        out_specs=[pl.BlockSpec((None, nseg, KV_W), lambda b: (b, 0, 0))] * 3
                  + [pl.BlockSpec((None, vt_rows, nseg), lambda b: (b, 0, 0))],
        out_shape=[out] * 3 + [jax.ShapeDtypeStruct((BATCH, vt_rows, nseg), bf16)],
        compiler_params=_cparams(("parallel",)),
        name="compress_prompt",
    )(x, *cw)


def _compress_sample(cache, page_table, cw):
    nseg = PAST_LEN // CMP_STRIDE
    x = _token_minor(cache)
    perm = _seg_perm()
    out = jax.ShapeDtypeStruct((DEC_BATCH, nseg, KV_W), bf16)
    page_specs = [pl.BlockSpec((None, KV_W, PAGE_SIZE), functools.partial(lambda b, pt, p: (pt[b * N_PAGES + p], 0, 0), p=p))
                  for p in range(N_PAGES)]
    grid_spec = pltpu.PrefetchScalarGridSpec(
        num_scalar_prefetch=1,
        grid=(DEC_BATCH,),
        in_specs=page_specs + [_const_spec(perm, 1)] + [_const_spec(a, 1) for a in cw],
        out_specs=[pl.BlockSpec((None, nseg, KV_W), lambda b, pt: (b, 0, 0))] * 3,
        scratch_shapes=[pltpu.VMEM((N_PAGES // 2, 2 * PAGE_SIZE, KV_W), bf16)])
    return pl.pallas_call(
        _compress_sample_kernel,
        grid_spec=grid_spec,
        out_shape=[out] * 3,
        compiler_params=_cparams(("arbitrary",)),
        name="compress_sample",
    )(page_table.reshape(-1), *([x] * N_PAGES), perm, *cw)


def _token_minor(kv):
    n, t = kv.shape[0], kv.shape[1]
    return kv.transpose(0, 2, 3, 4, 1).reshape(n, KV_W, t)


TQ = 256
N_SEL = SEQ // SEL_BLOCK


def _half_mask(shape, hi):
    lane = lax.broadcasted_iota(jnp.int32, shape, 1)
    return (lane >= HEAD_DIM) if hi else (lane < HEAD_DIM)


def _cmp_branch(i, qs_ref, kd_ref, vt_ref, mt_ref, o_ref):
    t0 = i * TQ
    ncmp = kd_ref.shape[0]
    nidx = lax.broadcasted_iota(jnp.int32, (ncmp, Q_PER_KV * TQ), 0)
    tpos = t0 + (lax.broadcasted_iota(jnp.int32, (ncmp, Q_PER_KV * TQ), 1) & (TQ - 1))
    mask = nidx * CMP_STRIDE + (CMP_BLOCK - 1) <= tpos
    s = jnp.where(mask, _nt(kd_ref[...], qs_ref[:, 0:LANES]), NEG)
    m = jnp.max(s, axis=0, keepdims=True)
    ex = jnp.where(mask, jnp.exp2(s - m), 0.0)
    p = ex * (1.0 / jnp.maximum(jnp.sum(ex, axis=0, keepdims=True), 1e-30))
    pb = p.astype(bf16)
    vt = vt_ref[...]
    heads = [jnp.dot(vt, pb[:, r * TQ:(r + 1) * TQ], preferred_element_type=f32) for r in range(Q_PER_KV)]
    o_ref[...] = jnp.concatenate(heads, axis=0).T
    psum = sum(p[:, r * TQ:(r + 1) * TQ] for r in range(Q_PER_KV))
    hi, lo = _split2(psum)
    imp = (jnp.dot(mt_ref[...], hi, preferred_element_type=f32)
           + jnp.dot(mt_ref[...], lo, preferred_element_type=f32))
    imp = imp[:N_SEL]
    j = lax.broadcasted_iota(jnp.int32, (N_SEL, TQ), 0)
    jt = (t0 + lax.broadcasted_iota(jnp.int32, (N_SEL, TQ), 1)) // SEL_BLOCK
    imp = jnp.where((j == 0) | (j == jt) | (j == jt - 1), SELECT_FORCE, imp)
    imp = jnp.where(j > jt, -SELECT_FORCE, imp)
    cnt = jnp.zeros((N_SEL, TQ), f32)
    for jp in range(N_SEL):
        row = imp[jp:jp + 1, :]
        cnt = cnt + jnp.where(j > jp, jnp.where(row >= imp, 1.0, 0.0), jnp.where(row > imp, 1.0, 0.0))
    selm = jnp.where((cnt < SEL_TOPN) & (j <= jt), 0.0, SEL_NEG)
    return jnp.concatenate([selm, selm], axis=0).T


def _cmp_to_sel_matrix_t(n_cmp_pad, n_slc, n_cmp):
    ratio = SEL_BLOCK // CMP_STRIDE
    i = np.arange(n_cmp_pad)[None, :]
    jj = np.arange(n_slc)[:, None]
    diff = i - ratio * jj
    mat = np.zeros((n_slc, n_cmp_pad), np.float32)
    for n in range(CMP_BLOCK // CMP_STRIDE):
        mat += ((diff + n >= 0) & (diff + n < ratio)).astype(np.float32)
    mat[:, n_cmp:] = 0.0
    return mat


VT_ROWS = HEAD_DIM + 16
SEL_KT = 4

def _flash_tile(qs, k, vt, mask, m, acc):
    s = _nt(k, qs)
    if mask is not None:
        s = jnp.where(mask, s, NEG)
    m_next = jnp.maximum(m, jnp.max(s, axis=0, keepdims=True))
    alpha = jnp.exp2(m - m_next)
    p = jnp.exp2(s - m_next).astype(bf16)
    pv = jnp.concatenate([jnp.dot(vt, p[:, r * TQ:(r + 1) * TQ], preferred_element_type=f32)
                          for r in range(Q_PER_KV)], axis=1)
    return m_next, alpha * acc + pv


def _flash_finish(acc, o_ref):
    o_t = acc[:HEAD_DIM] * (1.0 / acc[HEAD_DIM:HEAD_DIM + 1])
    heads = jnp.concatenate([o_t[:, r * TQ:(r + 1) * TQ] for r in range(Q_PER_KV)], axis=0)
    o_ref[...] = heads.T


LOG2E = math.log2(math.e)


def _stack_queries(q_ref, qs_ref):
    for r in range(Q_PER_KV):
        q2 = q_ref[:, (r // 2) * LANES:(r // 2 + 1) * LANES] * (ATTN_SCALE * LOG2E)
        qs_ref[r * TQ:(r + 1) * TQ, 0:LANES] = jnp.where(_half_mask((TQ, LANES), r % 2), q2, 0.0).astype(bf16)


def _flash_init():
    return jnp.full((1, Q_PER_KV * TQ), NEG, f32), jnp.zeros((VT_ROWS, Q_PER_KV * TQ), f32)


def _sel_branch(i, selm, kd_ref, oh_ref, vt_ref, o_ref, qs_ref, m_ref, acc_ref):
    selm = selm.astype(bf16)
    for r in range(Q_PER_KV):
        qs_ref[r * TQ:(r + 1) * TQ, LANES:2 * LANES] = selm
    qs = qs_ref[...]

    def span(first, n, mask, m, acc):
        rows = pl.ds(pl.multiple_of(first * TQ, TQ), n * TQ)
        k = jnp.concatenate([kd_ref[rows, :], oh_ref[rows, :]], axis=1)
        return _flash_tile(qs, k, _with_ones(vt_ref[:, rows]), mask, m, acc)

    full = i // SEL_KT
    for n in range(1, SEL_KT + 1):
        @pl.when(i % SEL_KT == n - 1)
        def _(n=n):
            key = lax.broadcasted_iota(jnp.int32, (n * TQ, Q_PER_KV * TQ), 0)
            qry = (lax.broadcasted_iota(jnp.int32, (n * TQ, Q_PER_KV * TQ), 1) & (TQ - 1)) + (n - 1) * TQ
            m, acc = span(full * SEL_KT, n, key <= qry, *_flash_init())
            m_ref[...] = m
            acc_ref[...] = acc

    _, acc = lax.fori_loop(0, full, lambda t, carry: span(SEL_KT * t, SEL_KT, None, *carry),
                           (m_ref[...], acc_ref[...]))
    _flash_finish(acc, o_ref)


def _win_branch(i, kd_ref, vt_ref, o_ref, qs_ref, acc_ref):
    qs = qs_ref[:, 0:LANES]

    def span(first, n):
        rows = pl.ds(pl.multiple_of(first * TQ, TQ), n * TQ)
        key = lax.broadcasted_iota(jnp.int32, (n * TQ, Q_PER_KV * TQ), 0)
        qry = (lax.broadcasted_iota(jnp.int32, (n * TQ, Q_PER_KV * TQ), 1) & (TQ - 1)) + (n - 1) * TQ
        dist = qry - key
        _, acc = _flash_tile(qs, kd_ref[rows, :], _with_ones(vt_ref[:, rows]), (dist >= 0) & (dist < WINDOW),
                             *_flash_init())
        acc_ref[...] = acc

    n_back = WINDOW // TQ

    @pl.when(i >= n_back)
    def _():
        span(i - n_back, n_back + 1)

    for early in range(n_back):
        @pl.when(i == early)
        def _(early=early):
            span(0, early + 1)

    _flash_finish(acc_ref[...], o_ref)


def _with_ones(vt):
    return jnp.concatenate([vt.astype(bf16), jnp.ones((VT_ROWS - HEAD_DIM, vt.shape[1]), bf16)], axis=0)


def _prompt_attn_kernel(q_ref, kdc_ref, vtc_ref, mt_ref, kds_ref, oh_ref, vts_ref, kdw_ref, vtw_ref,
                        oc_ref, os_ref, ow_ref, qs_ref, m_ref, acc_ref):
    i = pl.program_id(2)
    _stack_queries(q_ref, qs_ref)
    selm = _cmp_branch(i, qs_ref, kdc_ref, vtc_ref, mt_ref, oc_ref)
    _sel_branch(i, selm, kds_ref, oh_ref, vts_ref, os_ref, qs_ref, m_ref, acc_ref)
    _win_branch(i, kdw_ref, vtw_ref, ow_ref, qs_ref, acc_ref)


def _prompt_attention(q, kd_c, vt_c, kd_s, kv_s_t, kd_w, kv_w_t):
    assert WINDOW % TQ == 0
    ncmp = SEQ // CMP_STRIDE
    mt = np.zeros((LANES, ncmp), np.float32)
    mt[:N_SEL] = _cmp_to_sel_matrix_t(ncmp, N_SEL, ncmp - 1)
    mt = jnp.asarray(mt, bf16)
    blk = np.arange(SEQ)[:, None] // SEL_BLOCK == np.arange(LANES)[None, :]
    oh = jnp.asarray(blk.astype(np.float32), bf16)
    nq = SEQ // TQ
    q_spec = pl.BlockSpec((TQ, 2 * LANES), lambda b, g, i: (b * nq + i, g))
    cmp_spec = pl.BlockSpec((None, ncmp, LANES), lambda b, g, i: (b, 0, g))
    cmp_vt_spec = pl.BlockSpec((None, HEAD_DIM, ncmp), lambda b, g, i: (b, g, 0))
    kd_spec = pl.BlockSpec((SEQ, LANES), lambda b, g, i: (b, g))
    vt_spec = pl.BlockSpec((None, HEAD_DIM, SEQ), lambda b, g, i: (b, N_KV + g, 0))
    const = lambda a: pl.BlockSpec(a.shape, lambda b, g, i: (0, 0))
    out = jax.ShapeDtypeStruct((BATCH * SEQ, ATTN_W), f32)
    return pl.pallas_call(
        _prompt_attn_kernel,
        grid=(BATCH, N_KV, nq),
        in_specs=[q_spec, cmp_spec, cmp_vt_spec, const(mt), kd_spec, const(oh), vt_spec, kd_spec, vt_spec],
        out_specs=[q_spec, q_spec, q_spec],
        out_shape=[out, out, out],
        scratch_shapes=[pltpu.VMEM((Q_PER_KV * TQ, 2 * LANES), bf16), pltpu.VMEM((1, Q_PER_KV * TQ), f32),
                        pltpu.VMEM((VT_ROWS, Q_PER_KV * TQ), f32)],
        compiler_params=_cparams(("parallel", "parallel", "parallel")),
        name="prompt_attention",
    )(q, kd_c, vt_c, mt, kd_s, oh, kv_s_t, kd_w, kv_w_t)


SSD_Q = 256
HALF_INNER = D_INNER // N_SSM_GROUPS
BC_W = N_SSM_GROUPS * D_STATE


def _mamba_consts(conv_w, conv_b, dt_bias, a_log, d_skip, ssm_norm):
    pad = lambda v: jnp.zeros((1, GD_W), f32).at[0, DT_LANE0:DT_LANE0 + N_SSM_HEADS].set(v)
    e16 = np.zeros((GD_W, D_INNER), np.float32)
    for h in range(N_SSM_HEADS):
        e16[DT_LANE0 + h, h * SSM_HEAD_DIM:(h + 1) * SSM_HEAD_DIM] = 1.0
    tri = np.tril(np.ones((SSD_Q, SSD_Q), np.float32))
    return (conv_w, conv_b[None], pad(dt_bias), pad(a_log), jnp.asarray(e16, bf16),
            jnp.repeat(d_skip, SSM_HEAD_DIM)[None], ssm_norm[None], jnp.asarray(tri, bf16))


def _dt_and_decay(gd, dtb_ref, alog_ref):
    lane = lax.broadcasted_iota(jnp.int32, gd.shape, 1)
    live = (lane >= DT_LANE0) & (lane < DT_LANE0 + N_SSM_HEADS)
    dt = jnp.where(live, _softplus(gd + dtb_ref[...]), 0.0)
    return dt, dt * (-jnp.exp(alog_ref[...]))


def _gated_group_norm(y, z, nw_ref):
    y = y * _silu(z)
    outs = []
    for g in range(N_SSM_GROUPS):
        yg = y[:, g * HALF_INNER:(g + 1) * HALF_INNER]
        outs.append(yg * lax.rsqrt(jnp.mean(yg * yg, axis=-1, keepdims=True) + NORM_EPS))
    return jnp.concatenate(outs, axis=1) * nw_ref[...]


def _mamba_prompt_kernel(z_ref, xbc_ref, gd_ref, cw_ref, cb_ref, dtb_ref, alog_ref, e16_ref, dsk_ref, nw_ref, tri_ref,
                         y_ref, st_ref, xpad_ref, state_ref):
    c = pl.program_id(1)
    nq = SSD_Q

    @pl.when(c == 0)
    def _():
        state_ref[...] = jnp.zeros(state_ref.shape, f32)
        xpad_ref[0:8, :] = jnp.zeros((8, CONV_DIM), f32)

    xpad_ref[8:8 + nq, :] = xbc_ref[...]
    conv = cb_ref[...]
    for w in range(CONV_W):
        conv = conv + xpad_ref[8 - (CONV_W - 1) + w:8 - (CONV_W - 1) + w + nq, :] * cw_ref[w:w + 1, :]
    xpad_ref[0:8, :] = xpad_ref[nq:nq + 8, :]
    act = _silu(conv)
    xs, bm, cm = act[:, :D_INNER], act[:, D_INNER:D_INNER + BC_W], act[:, D_INNER + BC_W:]

    dt, a = _dt_and_decay(gd_ref[...], dtb_ref, alog_ref)
    a_cs = sum(jnp.dot(tri_ref[...], t, preferred_element_type=f32) for t in _split3(a))
    ea = jnp.exp(a_cs)
    te = jnp.exp(a_cs[nq - 1:nq, :] - a_cs)
    e16 = e16_ref[...]
    dt_x, ea_x, te_x = _dot_exact(dt, e16), _dot_exact(ea, e16), _dot_exact(te, e16)
    xdt = xs * dt_x
    xw = (xdt * te_x).astype(bf16)
    a_cst = a_cs.T
    tril = lax.broadcasted_iota(jnp.int32, (nq, nq), 1) <= lax.broadcasted_iota(jnp.int32, (nq, nq), 0)

    ys = []
    for g in range(N_SSM_GROUPS):
        bg = bm[:, g * D_STATE:(g + 1) * D_STATE]
        cgb = cm[:, g * D_STATE:(g + 1) * D_STATE].astype(bf16)
        cb = _nt(cgb, bg.astype(bf16))
        st = state_ref[g]
        yoff = jnp.dot(cgb, st.astype(bf16), preferred_element_type=f32)
        gl = slice(g * HALF_INNER, (g + 1) * HALF_INNER)
        state_ref[g] = ea_x[nq - 1:nq, gl] * st + jnp.dot(bg.T.astype(bf16), xw[:, gl], preferred_element_type=f32)
        for hp in range(HALF_INNER // LANES):
            pl_ = slice(g * HALF_INNER + hp * LANES, g * HALF_INNER + (hp + 1) * LANES)
            xp = xdt[:, pl_]
            yp = ea_x[:, pl_] * yoff[:, hp * LANES:(hp + 1) * LANES] + dsk_ref[:, pl_] * xs[:, pl_]
            for e in range(2):
                lane_h = DT_LANE0 + g * (N_SSM_HEADS // N_SSM_GROUPS) + hp * 2 + e
                seg = a_cs[:, lane_h:lane_h + 1] - a_cst[lane_h:lane_h + 1, :]
                wgt = (cb * jnp.exp(jnp.where(tril, seg, NEG))).astype(bf16)
                xz = jnp.where(_half_mask((nq, LANES), e), xp, 0.0).astype(bf16)
                yp = yp + jnp.dot(wgt, xz, preferred_element_type=f32)
            ys.append(yp)
    y = jnp.concatenate(ys, axis=1)
    y_ref[...] = _gated_group_norm(y, z_ref[...], nw_ref).astype(bf16)

    @pl.when(c == pl.num_programs(1) - 1)
    def _():
        st_ref[...] = state_ref[...]


def _mamba_prompt(z, xbc, gd, mc):
    nc = SEQ // SSD_Q
    row = lambda w: pl.BlockSpec((SSD_Q, w), lambda b, c: (b * nc + c, 0))
    y, st = pl.pallas_call(
        _mamba_prompt_kernel,
        grid=(BATCH, nc),
        in_specs=[row(D_INNER), row(CONV_DIM), row(GD_W)] + [_const_spec(a, 2) for a in mc],
        out_specs=[row(D_INNER), pl.BlockSpec((None, N_SSM_GROUPS, D_STATE, HALF_INNER), lambda b, c: (b, 0, 0, 0))],
        out_shape=[jax.ShapeDtypeStruct((BATCH * SEQ, D_INNER), bf16),
                   jax.ShapeDtypeStruct((BATCH, N_SSM_GROUPS, D_STATE, HALF_INNER), f32)],
        scratch_shapes=[pltpu.VMEM((SSD_Q + 8, CONV_DIM), f32), pltpu.VMEM((N_SSM_GROUPS, D_STATE, HALF_INNER), f32)],
        compiler_params=_cparams(("parallel", "arbitrary")),
        name="mamba_prompt",
    )(z, xbc, gd, *mc)
    hpg = N_SSM_HEADS // N_SSM_GROUPS
    st = st.reshape(BATCH, N_SSM_GROUPS, D_STATE, hpg, SSM_HEAD_DIM).transpose(0, 1, 3, 4, 2)
    return y, st.reshape(BATCH, N_SSM_HEADS, SSM_HEAD_DIM, D_STATE)


def _mamba_sample_kernel(z_ref, xbc_ref, gd_ref, sc_ref, s_ref, cw_ref, cb_ref, dtb_ref, alog_ref, e16_ref, dsk_ref,
                         nw_ref, y_ref, so_ref):
    conv = cb_ref[...] + xbc_ref[...] * cw_ref[CONV_W - 1:CONV_W, :]
    for w in range(CONV_W - 1):
        conv = conv + sc_ref[w:w + 1, :] * cw_ref[w:w + 1, :]
    act = _silu(conv)
    xs, bm, cm = act[:, :D_INNER], act[:, D_INNER:D_INNER + BC_W], act[:, D_INNER + BC_W:]
    dt, a = _dt_and_decay(jnp.broadcast_to(gd_ref[...], (8, GD_W)), dtb_ref, alog_ref)
    e16 = e16_ref[...]
    dt_x = _dot_exact(dt, e16)[0:1]
    da_x = _dot_exact(jnp.exp(a), e16)[0:1]
    rows = N_SSM_HEADS * SSM_HEAD_DIM
    xcol = jnp.broadcast_to(xs * dt_x, (D_STATE, rows)).T
    acol = jnp.broadcast_to(da_x, (D_STATE, rows)).T
    rowi = lax.broadcasted_iota(jnp.int32, (rows, D_STATE), 0)
    bfull = jnp.where(rowi < HALF_INNER, bm[:, :D_STATE], bm[:, D_STATE:])
    snew = s_ref[...] * acol + xcol * bfull
    so_ref[...] = snew
    r8 = lax.broadcasted_iota(jnp.int32, (8, D_STATE), 0)
    c8 = jnp.where(r8 == 0, cm[:, :D_STATE], jnp.where(r8 == 1, cm[:, D_STATE:], 0.0)).astype(bf16)
    yy = _nt(c8, snew.astype(bf16))
    lane = lax.broadcasted_iota(jnp.int32, (1, rows), 1)
    y = jnp.where(lane < HALF_INNER, yy[0:1], yy[1:2]) + dsk_ref[...] * xs
    y_ref[...] = _gated_group_norm(y, z_ref[...], nw_ref).astype(bf16)


def _mamba_sample(z, xbc, gd, state_conv, state_ssm, mc):
    nb = z.shape[0]
    mc = mc[:7]
    rows = N_SSM_HEADS * SSM_HEAD_DIM
    one = lambda w: pl.BlockSpec((None, 1, w), lambda b: (b, 0, 0))
    s_spec = pl.BlockSpec((None, rows, D_STATE), lambda b: (b, 0, 0))
    y, so = pl.pallas_call(
        _mamba_sample_kernel,
        grid=(nb,),
        in_specs=[one(D_INNER), one(CONV_DIM), one(GD_W),
                  pl.BlockSpec((None, CONV_W - 1, CONV_DIM), lambda b: (b, 0, 0)), s_spec]
                 + [_const_spec(a, 1) for a in mc],
        out_specs=[one(D_INNER), s_spec],
        out_shape=[jax.ShapeDtypeStruct((nb, 1, D_INNER), bf16), jax.ShapeDtypeStruct((nb, rows, D_STATE), f32)],
        compiler_params=_cparams(("parallel",)),
        name="mamba_sample",
    )(z[:, None], xbc[:, None], gd[:, None], state_conv, state_ssm.reshape(nb, rows, D_STATE), *mc)
    return y[:, 0], so.reshape(nb, N_SSM_HEADS, SSM_HEAD_DIM, D_STATE)


N_CMP_S = PAST_LEN // CMP_STRIDE - 1
N_SLC_S = PAST_LEN // SEL_BLOCK + 1
N_SLC_PAD = 2 * LANES
HALF_PAGE = PAGE_SIZE // SEL_BLOCK
GONE = -3.0e38


def _cmp_attn_sample_kernel(qz_ref, kd_ref, vl_ref, vh_ref, mts_ref, o_ref, imp_ref):
    qz = (qz_ref[...] * ATTN_SCALE).astype(bf16)
    ncmp = kd_ref.shape[0]
    nidx = lax.broadcasted_iota(jnp.int32, (N_HEADS, ncmp), 1)
    rowh = lax.broadcasted_iota(jnp.int32, (N_HEADS, ncmp), 0)
    mask = (nidx < N_CMP_S) & (nidx * CMP_STRIDE + (CMP_BLOCK - 1) <= PAST_LEN)
    rowo = lax.broadcasted_iota(jnp.int32, (N_HEADS, LANES), 0)
    r8 = lax.broadcasted_iota(jnp.int32, (8, N_SLC_PAD), 0)
    o_acc = jnp.zeros((N_HEADS, LANES), f32)
    imp = jnp.zeros((8, N_SLC_PAD), f32)
    for g in range(N_KV):
        gl = slice(g * LANES, (g + 1) * LANES)
        s = jnp.where(mask, _nt(qz, kd_ref[:, gl]), NEG)
        m = jnp.max(s, axis=1, keepdims=True)
        ex = jnp.where(mask, jnp.exp(s - m), 0.0)
        p = ex * (1.0 / jnp.maximum(jnp.sum(ex, axis=1, keepdims=True), 1e-30))
        p = jnp.where(rowh // Q_PER_KV == g, p, 0.0)
        pb = p.astype(bf16)
        o_l = jnp.dot(pb, vl_ref[:, gl], preferred_element_type=f32)
        o_h = jnp.dot(pb, vh_ref[:, gl], preferred_element_type=f32)
        o_acc = o_acc + jnp.where(rowo % 2 == 0, o_l, o_h)
        hi, lo = _split2(p)
        imp_h = jnp.dot(hi, mts_ref[...], preferred_element_type=f32) + jnp.dot(lo, mts_ref[...], preferred_element_type=f32)
        imp = imp + jnp.where(r8 == g, jnp.sum(imp_h, axis=0, keepdims=True), 0.0)
    o_ref[...] = o_acc
    imp_ref[...] = imp


def _topk_sample_kernel(imp_ref, idx_ref):
    rows = imp_ref.shape[0]
    j = lax.broadcasted_iota(jnp.int32, (rows, N_SLC_PAD), 1)
    jt = PAST_LEN // SEL_BLOCK
    imp = imp_ref[...]
    imp = jnp.where((j == 0) | (j == jt) | (j == jt - 1), SELECT_FORCE, imp)
    imp = jnp.where(j > jt, -SELECT_FORCE, imp)
    imp = jnp.where(j >= N_SLC_S, NEG, imp)
    jf = j.astype(f32)
    lane = lax.broadcasted_iota(jnp.int32, (rows, LANES), 1)
    picked = jnp.zeros((rows, LANES), f32)
    for k in range(SEL_TOPN):
        m = jnp.max(imp, axis=1, keepdims=True)
        ix = jnp.min(jnp.where(imp == m, jf, float(N_SLC_PAD)), axis=1, keepdims=True)
        picked = jnp.where(lane == k, ix, picked)
        imp = jnp.where(jf == ix, GONE, imp)
    idx_ref[...] = picked.astype(jnp.int32)


def _cmp_attn_sample(qz, kd, vl, vh):
    nb, ncmp = kd.shape[0], kd.shape[1]
    mts = np.zeros((ncmp, N_SLC_PAD), np.float32)
    mts[:, :N_SLC_S] = _cmp_to_sel_matrix_t(ncmp, N_SLC_S, N_CMP_S).T
    mts = jnp.asarray(mts, bf16)
    kv_spec = pl.BlockSpec((None, ncmp, KV_W), lambda b: (b, 0, 0))
    o16, imp = pl.pallas_call(
        _cmp_attn_sample_kernel,
        grid=(nb,),
        in_specs=[pl.BlockSpec((None, N_HEADS, LANES), lambda b: (b, 0, 0)), kv_spec, kv_spec, kv_spec,
                  pl.BlockSpec(mts.shape, lambda b: (0, 0))],
        out_specs=[pl.BlockSpec((None, N_HEADS, LANES), lambda b: (b, 0, 0)),
                   pl.BlockSpec((None, 8, N_SLC_PAD), lambda b: (b, 0, 0))],
        out_shape=[jax.ShapeDtypeStruct((nb, N_HEADS, LANES), f32), jax.ShapeDtypeStruct((nb, 8, N_SLC_PAD), f32)],
        compiler_params=_cparams(("parallel",)),
        name="cmp_attn_sample",
    )(qz, kd, vl, vh, mts)
    idx = pl.pallas_call(
        _topk_sample_kernel,
        out_shape=jax.ShapeDtypeStruct((nb * 8, LANES), jnp.int32),
        name="topk_sample",
    )(imp.reshape(nb * 8, N_SLC_PAD))
    return o16, idx.reshape(nb, 8, LANES)


def _one_query_softmax(s, s_new, vt, v_new):
    m = jnp.maximum(jnp.max(s, axis=1, keepdims=True), s_new)
    p = jnp.exp(s - m)
    p_new = jnp.exp(s_new - m)
    den = jnp.sum(p, axis=1, keepdims=True) + p_new
    num = _nt(p.astype(bf16), vt) + p_new * v_new
    return num * (1.0 / den)


def _selwin_sample_kernel(idx_ref, pt_ref, *refs):
    del pt_ref
    kv = refs[:SEL_TOPN]
    q_ref, ksn_ref, win_ref, kwn_ref, os_ref, ow_ref = refs[SEL_TOPN:]
    b, g = pl.program_id(0), pl.program_id(1)
    q8 = q_ref[...] * ATTN_SCALE
    qb = q8.astype(bf16)
    kcat = jnp.concatenate([r[0] for r in kv], axis=1).astype(bf16)
    vcat = jnp.concatenate([r[1] for r in kv], axis=1).astype(bf16)
    n = SEL_TOPN * PAGE_SIZE
    lane = lax.broadcasted_iota(jnp.int32, (8, n), 1)
    slot = lane // PAGE_SIZE
    blk = jnp.zeros((8, n), jnp.int32)
    for k in range(SEL_TOPN):
        blk = jnp.where(slot == k, idx_ref[(b * N_KV + g) * SEL_TOPN + k], blk)
    live = (blk < N_SLC_S - 1) & ((lane % PAGE_SIZE) // SEL_BLOCK == blk % HALF_PAGE)
    s = jnp.where(live, jnp.dot(qb, kcat, preferred_element_type=f32), NEG)
    k_new, v_new = ksn_ref[pl.ds(g, 1), :], ksn_ref[pl.ds(N_KV + g, 1), :]
    s_new = jnp.sum(q8 * k_new, axis=1, keepdims=True)
    os_ref[...] = _one_query_softmax(s, s_new, vcat, v_new)
    nbuf = win_ref.shape[-1]
    i = lax.broadcasted_iota(jnp.int32, (8, nbuf), 1)
    s = jnp.where(nbuf - i < WINDOW, jnp.dot(qb, win_ref[0].astype(bf16), preferred_element_type=f32), NEG)
    k_new, v_new = kwn_ref[pl.ds(g, 1), :], kwn_ref[pl.ds(N_KV + g, 1), :]
    s_new = jnp.sum(q8 * k_new, axis=1, keepdims=True)
    ow_ref[...] = _one_query_softmax(s, s_new, win_ref[1].astype(bf16), v_new)


def _selwin_sample(idx, page_table, cache_s, q_s, ks_new, win_buf, kw_new):
    nb = q_s.shape[0]
    nbuf = win_buf.shape[1]
    pages = cache_s.transpose(0, 2, 3, 4, 1)
    wb = win_buf.transpose(0, 2, 3, 4, 1)
    qg = q_s.reshape(nb, N_KV, Q_PER_KV, HEAD_DIM)
    q8 = jnp.concatenate([qg, jnp.zeros_like(qg)], axis=2)

    def blk_map(b, g, idx_ref, pt_ref, k):
        j = jnp.minimum(idx_ref[(b * N_KV + g) * SEL_TOPN + k], N_SLC_S - 2)
        return (pt_ref[b * N_PAGES + j // HALF_PAGE], 0, g, 0, 0)

    blk_specs = [pl.BlockSpec((None, 2, None, HEAD_DIM, PAGE_SIZE), functools.partial(blk_map, k=k))
                 for k in range(SEL_TOPN)]
    new_spec = pl.BlockSpec((None, 2 * N_KV, HEAD_DIM), lambda b, g, *_: (b, 0, 0))
    win_spec = pl.BlockSpec((None, 2, None, HEAD_DIM, nbuf), lambda b, g, *_: (b, 0, g, 0, 0))
    o_spec = pl.BlockSpec((None, None, 8, HEAD_DIM), lambda b, g, *_: (b, g, 0, 0))
    grid_spec = pltpu.PrefetchScalarGridSpec(
        num_scalar_prefetch=2,
        grid=(nb, N_KV),
        in_specs=blk_specs + [o_spec, new_spec, win_spec, new_spec],
        out_specs=[o_spec, o_spec])
    o_shape = jax.ShapeDtypeStruct((nb, N_KV, 8, HEAD_DIM), f32)
    ks3, kw3 = ks_new.reshape(nb, 2 * N_KV, HEAD_DIM), kw_new.reshape(nb, 2 * N_KV, HEAD_DIM)
    os8, ow8 = pl.pallas_call(
        _selwin_sample_kernel,
        grid_spec=grid_spec,
        out_shape=[o_shape, o_shape],
        compiler_params=_cparams(("arbitrary", "arbitrary")),
        name="selwin_sample",
    )(idx.reshape(-1), page_table.reshape(-1), *([pages] * SEL_TOPN), q8, ks3, wb, kw3)
    return os8[:, :, :Q_PER_KV].reshape(nb, ATTN_W), ow8[:, :, :Q_PER_KV].reshape(nb, ATTN_W)


def _sample_query_layouts(q):
    nb = q.shape[0]
    q16 = q.reshape(nb, N_HEADS, HEAD_DIM)
    z = jnp.zeros_like(q16)
    lo, hi = jnp.concatenate([q16, z], axis=-1), jnp.concatenate([z, q16], axis=-1)
    h = jnp.arange(N_HEADS)[None, :, None]
    return jnp.where(h % 2 == 0, lo, hi)


def _unpad_heads_by_parity(o16):
    nb = o16.shape[0]
    o = o16.reshape(nb, N_HEADS // 2, 2, 2, HEAD_DIM)
    return jnp.stack([o[:, :, 0, 0], o[:, :, 1, 1]], axis=2).reshape(nb, ATTN_W)


def _finish(x, ocmp, osel, owin, gd, ssd_y, w_out, ln_mlp, w_up, w_down, ln_final, tm, tm_mlp, tf):
    x1 = _outproj(x, ocmp, osel, owin, gd, ssd_y, _gate_expand(), w_out, tm)
    return _mlp(x1, ln_mlp, w_up, w_down, ln_final, tm_mlp, tf)


def kernel(x_prompt, x_sample, cache_kv_cmp, cache_kv_sel, state_kv_win, state_conv, state_ssm, page_table, ln_mix, w_in, cmp_pe, cmp_w1, cmp_b1, cmp_w2, cmp_b2, conv_w, conv_b, dt_bias, a_log, d_skip, ssm_norm, w_out, ln_mlp, w_up, w_down, ln_final):
    nb = x_sample.shape[0]
    w_in_p = _prep_w_in(w_in[0])
    w_out_b, w_up_b, w_down_b = w_out[0].astype(bf16), w_up[0].astype(bf16), w_down[0].astype(bf16)
    lnw, lnm, lnf = ln_mix[0][None], ln_mlp[0][None], ln_final[None]
    cw = _compress_weights(cmp_pe[0], cmp_w1[0], cmp_b1[0], cmp_w2[0], cmp_b2[0])
    mc = _mamba_consts(conv_w[0], conv_b[0], dt_bias[0], a_log[0], d_skip[0], ssm_norm[0])
    kv_shape = (2, N_KV, HEAD_DIM)

    xp = x_prompt.reshape(BATCH * SEQ, D_MODEL)
    q, z, xbc, gd, kc, kct, kst, kwt, kds, kdw = _inproj_prompt(xp, lnw, w_in_p, 256)
    kd, _, _, vt = _compress_prompt(kc, cw)
    ocmp, osel, owin = _prompt_attention(q, kd, vt, kds, kst, kdw, kwt)
    ssd_y, ssm_p = _mamba_prompt(z, xbc, gd, mc)
    y_prompt = _finish(xp, ocmp, osel, owin, gd, ssd_y, w_out_b, lnm, w_up_b, w_down_b, lnf, 256, 512, 1024)

    xs = x_sample.reshape(nb, D_MODEL)
    q_s, z_s, xbc_s, gd_s, kc_s, ks_s, kw_s = _inproj(xs, lnw, w_in_p, nb)
    kd_s, vl_s, vh_s = _compress_sample(cache_kv_cmp[0], page_table, cw)
    ocmp16, idx8 = _cmp_attn_sample(_sample_query_layouts(q_s), kd_s, vl_s, vh_s)
    idx = idx8[:, :N_KV, :SEL_TOPN]
    osel_s, owin_s = _selwin_sample(idx, page_table, cache_kv_sel[0], q_s, ks_s, state_kv_win[0], kw_s)
    ssd_y_s, ssm_s = _mamba_sample(z_s, xbc_s, gd_s, state_conv[0], state_ssm[0], mc)
    y_sample = _finish(xs, _unpad_heads_by_parity(ocmp16), osel_s, owin_s, gd_s, ssd_y_s,
                       w_out_b, lnm, w_up_b, w_down_b, lnf, nb, nb, 1024)

    def token_major(t):
        return t.reshape(BATCH, *kv_shape, t.shape[-1]).transpose(0, 4, 1, 2, 3)[None]

    conv_p = xbc.reshape(BATCH, SEQ, CONV_DIM)[:, -(CONV_W - 1):]
    kv_win_s = jnp.concatenate([state_kv_win[0], kw_s.reshape(nb, 1, *kv_shape)], axis=1)[:, -WINDOW:]
    conv_s = jnp.concatenate([state_conv[0], xbc_s[:, None]], axis=1)[:, -(CONV_W - 1):]
    return (y_prompt.reshape(BATCH, SEQ, D_MODEL), y_sample.reshape(nb, 1, D_MODEL),
            token_major(kct), token_major(kst), token_major(kwt[:, :, -min(WINDOW, SEQ):]),
            conv_p[None], ssm_p[None],
            kc_s.reshape(1, nb, 1, *kv_shape), ks_s.reshape(1, nb, 1, *kv_shape), kv_win_s[None],
            conv_s[None], ssm_s[None])
```

```python
import functools
import math

import jax
import jax.numpy as jnp
import numpy as np
from jax import lax
from jax.experimental import pallas as pl
from jax.experimental.pallas import tpu as pltpu

f32 = jnp.float32
bf16 = jnp.bfloat16

D_MODEL = 2048
BATCH = 2
SEQ = 4096
DEC_BATCH = 32
PAST_LEN = 8192
PAGE_SIZE = 128
HEAD_DIM = 64
N_HEADS = 16
N_KV = 4
Q_PER_KV = 4
ATTN_W = 1024
KV_W = 512
CMP_BLOCK = 32
CMP_STRIDE = 16
CMP_HIDDEN = 128
SEL_BLOCK = 64
SEL_TOPN = 16
WINDOW = 512
SELECT_FORCE = 1.0e4
ATTN_SCALE = HEAD_DIM ** -0.5
D_INNER = 1024
SSM_HEAD_DIM = 64
N_SSM_HEADS = 16
N_SSM_GROUPS = 2
D_STATE = 128
CONV_W = 4
CONV_DIM = D_INNER + 2 * N_SSM_GROUPS * D_STATE
D_FF = 4 * D_MODEL
IN_SPLITS = (ATTN_W, KV_W, KV_W, KV_W, 3 * N_HEADS, D_INNER, CONV_DIM, N_SSM_HEADS)
NORM_EPS = 1e-5

LANES = 128
GD_W = LANES
DT_LANE0 = 3 * N_HEADS
N_PAGES = PAST_LEN // PAGE_SIZE
SEG_PER_PAGE = PAGE_SIZE // CMP_STRIDE
NEG = -1.0e30
SEL_NEG = -1.0e9
VMEM_LIMIT = 56 * 1024 * 1024


def _cparams(sem):
    return pltpu.CompilerParams(dimension_semantics=sem, vmem_limit_bytes=VMEM_LIMIT)


def _nt(a, b):
    return lax.dot_general(a, b, (((1,), (1,)), ((), ())), preferred_element_type=f32)


def _split2(x):
    hi = x.astype(bf16)
    lo = (x - hi.astype(f32)).astype(bf16)
    return hi, lo


def _split3(x):
    hi = x.astype(bf16)
    r = x - hi.astype(f32)
    mid = r.astype(bf16)
    lo = (r - mid.astype(f32)).astype(bf16)
    return hi, mid, lo


def _dot_exact(x, w):
    return sum(jnp.dot(t, w, preferred_element_type=f32) for t in _split3(x))


def _silu(x):
    return x * (1.0 / (1.0 + jnp.exp(-x)))


def _sigmoid(x):
    return 1.0 / (1.0 + jnp.exp(-x))


def _softplus(x):
    return jnp.maximum(x, 0.0) + jnp.log1p(jnp.exp(-jnp.abs(x)))


IN_OUT_WIDTHS = (ATTN_W, D_INNER, CONV_DIM, GD_W, KV_W, KV_W, KV_W)
KD_W = N_KV * LANES
IN_COLS_ROWMAJOR = sum(IN_OUT_WIDTHS)


def _normed(x_ref, lnw_ref):
    x = x_ref[...]
    return (x * lax.rsqrt(jnp.mean(x * x, axis=-1, keepdims=True) + NORM_EPS) * lnw_ref[...]).astype(bf16)


def _inproj_kernel(x_ref, lnw_ref, w_ref, *out_refs):
    h = _normed(x_ref, lnw_ref)
    off = 0
    for ref in out_refs:
        n = ref.shape[-1]
        ref[...] = jnp.dot(h, w_ref[:, off:off + n], preferred_element_type=f32)
        off += n


def _inproj_prompt_kernel(x_ref, lnw_ref, w_ref, q_ref, z_ref, xbc_ref, gd_ref,
                          kct_ref, kst_ref, kwt_ref, kds_ref, kdw_ref, t_ref):
    h = _normed(x_ref, lnw_ref)
    off = 0

    def proj(n):
        nonlocal off
        y = jnp.dot(h, w_ref[:, off:off + n], preferred_element_type=f32)
        off += n
        return y

    for ref in (q_ref, z_ref, xbc_ref, gd_ref):
        ref[...] = proj(ref.shape[-1])
    for t_out in (kct_ref, kst_ref, kwt_ref):
        t_ref[...] = proj(KV_W)
        t_out[...] = t_ref[...].T
    kds_ref[...] = proj(KD_W).astype(bf16)
    kdw_ref[...] = proj(KD_W).astype(bf16)


def _inproj(x, lnw, w_perm, tm):
    m = x.shape[0]
    return pl.pallas_call(
        _inproj_kernel,
        grid=(m // tm,),
        in_specs=[pl.BlockSpec((tm, D_MODEL), lambda i: (i, 0)),
                  pl.BlockSpec((1, D_MODEL), lambda i: (0, 0)),
                  pl.BlockSpec((D_MODEL, IN_COLS_ROWMAJOR), lambda i: (0, 0), pipeline_mode=pl.Buffered(1))],
        out_specs=[pl.BlockSpec((tm, n), lambda i: (i, 0)) for n in IN_OUT_WIDTHS],
        out_shape=[jax.ShapeDtypeStruct((m, n), f32) for n in IN_OUT_WIDTHS],
        compiler_params=_cparams(("parallel",)),
        name="inproj",
    )(x, lnw, w_perm)


def _inproj_prompt(x, lnw, w_perm, tm):
    nq = SEQ // tm
    row = lambda n: pl.BlockSpec((tm, n), lambda i: (i, 0))
    tmin = pl.BlockSpec((None, KV_W, tm), lambda i: (i // nq, 0, i % nq))
    rm = lambda n, dt=f32: jax.ShapeDtypeStruct((BATCH * SEQ, n), dt)
    tshape = jax.ShapeDtypeStruct((BATCH, KV_W, SEQ), f32)
    return pl.pallas_call(
        _inproj_prompt_kernel,
        grid=(BATCH * nq,),
        in_specs=[pl.BlockSpec((tm, D_MODEL), lambda i: (i, 0)),
                  pl.BlockSpec((1, D_MODEL), lambda i: (0, 0)),
                  pl.BlockSpec(w_perm.shape, lambda i: (0, 0), pipeline_mode=pl.Buffered(1))],
        out_specs=[row(ATTN_W), row(D_INNER), row(CONV_DIM), row(GD_W), tmin, tmin, tmin, row(KD_W), row(KD_W)],
        out_shape=[rm(ATTN_W), rm(D_INNER), rm(CONV_DIM), rm(GD_W), tshape, tshape, tshape,
                   rm(KD_W, bf16), rm(KD_W, bf16)],
        scratch_shapes=[pltpu.VMEM((tm, KV_W), f32)],
        compiler_params=_cparams(("parallel",)),
        name="inproj_prompt",
    )(x, lnw, w_perm)


def _prep_w_in(w_in):
    parts, off = [], 0
    for width in IN_SPLITS:
        parts.append(w_in[:, off:off + width])
        off += width
    q, kc, ks, kw, g, z, xbc, dt = parts
    gd = jnp.concatenate([g, dt, jnp.zeros((D_MODEL, GD_W - 4 * N_HEADS), w_in.dtype)], axis=1)

    def dup_keys(kv):
        k = kv[:, :KV_W // 2].reshape(D_MODEL, N_KV, 1, HEAD_DIM)
        return jnp.broadcast_to(k, (D_MODEL, N_KV, 2, HEAD_DIM)).reshape(D_MODEL, KD_W)

    return jnp.concatenate([q, z, xbc, gd, kc, ks, kw, dup_keys(ks), dup_keys(kw)], axis=1).astype(bf16)


def _outproj_kernel(x_ref, oc_ref, os_ref, ow_ref, gd_ref, y_ref, eg_ref, w_ref, o_ref):
    gates = _sigmoid(gd_ref[...])
    hi, lo = _split2(gates)
    attn = None
    for c, br in enumerate((oc_ref, os_ref, ow_ref)):
        ge = (jnp.dot(hi, eg_ref[c], preferred_element_type=f32)
              + jnp.dot(lo, eg_ref[c], preferred_element_type=f32))
        term = ge * br[...]
        attn = term if attn is None else attn + term
    mix = (jnp.dot(attn.astype(bf16), w_ref[:ATTN_W, :], preferred_element_type=f32)
           + jnp.dot(y_ref[...], w_ref[ATTN_W:, :], preferred_element_type=f32))
    o_ref[...] = x_ref[...] + mix


def _outproj(x, ocmp, osel, owin, gd, ssd_y, eg, w_out, tm):
    m = x.shape[0]
    row = lambda w: pl.BlockSpec((tm, w), lambda i: (i, 0))
    return pl.pallas_call(
        _outproj_kernel,
        grid=(m // tm,),
        in_specs=[row(D_MODEL), row(ATTN_W), row(ATTN_W), row(ATTN_W), row(GD_W), row(D_INNER),
                  pl.BlockSpec((3, GD_W, ATTN_W), lambda i: (0, 0, 0)),
                  pl.BlockSpec((ATTN_W + D_INNER, D_MODEL), lambda i: (0, 0))],
        out_specs=row(D_MODEL),
        out_shape=jax.ShapeDtypeStruct((m, D_MODEL), f32),
        compiler_params=_cparams(("parallel",)),
        name="outproj",
    )(x, ocmp, osel, owin, gd, ssd_y, eg, w_out)


def _gate_expand():
    eg = np.zeros((3, GD_W, ATTN_W), np.float32)
    for h in range(N_HEADS):
        for c in range(3):
            eg[c, h * 3 + c, h * HEAD_DIM:(h + 1) * HEAD_DIM] = 1.0
    return jnp.asarray(eg, bf16)


def _mlp_kernel(x_ref, ln_ref, wu_ref, wd_ref, lnf_ref, o_ref, h_ref, acc_ref):
    k = pl.program_id(1)

    @pl.when(k == 0)
    def _():
        x = x_ref[...]
        h = x * lax.rsqrt(jnp.mean(x * x, axis=-1, keepdims=True) + NORM_EPS) * ln_ref[...]
        h_ref[...] = h.astype(bf16)
        acc_ref[...] = jnp.zeros_like(acc_ref)

    u = jnp.dot(h_ref[...], wu_ref[...], preferred_element_type=f32)
    u = jnp.square(jnp.maximum(u, 0.0)).astype(bf16)
    acc_ref[...] += jnp.dot(u, wd_ref[...], preferred_element_type=f32)

    @pl.when(k == pl.num_programs(1) - 1)
    def _():
        y = x_ref[...] + acc_ref[...]
        y = y * lax.rsqrt(jnp.mean(y * y, axis=-1, keepdims=True) + NORM_EPS) * lnf_ref[...]
        o_ref[...] = y


def _mlp(x, ln_mlp, w_up, w_down, ln_final, tm, tf):
    m = x.shape[0]
    return pl.pallas_call(
        _mlp_kernel,
        grid=(m // tm, D_FF // tf),
        in_specs=[pl.BlockSpec((tm, D_MODEL), lambda i, k: (i, 0)),
                  pl.BlockSpec((1, D_MODEL), lambda i, k: (0, 0)),
                  pl.BlockSpec((D_MODEL, tf), lambda i, k: (0, k)),
                  pl.BlockSpec((tf, D_MODEL), lambda i, k: (k, 0)),
                  pl.BlockSpec((1, D_MODEL), lambda i, k: (0, 0))],
        out_specs=pl.BlockSpec((tm, D_MODEL), lambda i, k: (i, 0)),
        out_shape=jax.ShapeDtypeStruct((m, D_MODEL), f32),
        scratch_shapes=[pltpu.VMEM((tm, D_MODEL), bf16), pltpu.VMEM((tm, D_MODEL), f32)],
        compiler_params=_cparams(("parallel", "arbitrary")),
        name="mlp",
    )(x, ln_mlp, w_up, w_down, ln_final)


N_SLAB_K = CMP_STRIDE * LANES


def _compress_weights(cmp_pe, cmp_w1, cmp_b1, cmp_w2, cmp_b2):
    eye2 = jnp.eye(2, dtype=f32)
    w1 = cmp_w1.reshape(2, 2, CMP_STRIDE, HEAD_DIM, CMP_HIDDEN)
    w1cat = jnp.einsum('crjdf,ab->cjadrbf', w1, eye2).reshape(2, N_SLAB_K, 4 * CMP_HIDDEN).astype(bf16)
    w1flat = cmp_w1.reshape(2, CMP_BLOCK * HEAD_DIM, CMP_HIDDEN).astype(bf16)
    pe8 = jnp.broadcast_to(cmp_pe.reshape(2, 1, CMP_BLOCK * HEAD_DIM), (2, 8, CMP_BLOCK * HEAD_DIM))
    b1 = cmp_b1.reshape(2, 1, CMP_HIDDEN)
    wk, wv = cmp_w2[0], cmp_w2[1]
    zk = jnp.zeros_like(wk)
    w2k = jnp.einsum('fd,ab,e->afbed', wk, eye2, jnp.ones((2,), f32)).reshape(2 * CMP_HIDDEN, 4 * HEAD_DIM)
    b2k = jnp.tile(cmp_b2[0], 4)[None]
    vl = jnp.einsum('fd,ab,e->afbed', wv, eye2, jnp.array([1.0, 0.0], f32)).reshape(2 * CMP_HIDDEN, 4 * HEAD_DIM)
    vh = jnp.einsum('fd,ab,e->afbed', wv, eye2, jnp.array([0.0, 1.0], f32)).reshape(2 * CMP_HIDDEN, 4 * HEAD_DIM)
    w2v = jnp.concatenate([vl, vh], axis=1)
    zb = jnp.zeros((HEAD_DIM,), f32)
    b2v = jnp.concatenate([cmp_b2[1], zb, cmp_b2[1], zb, zb, cmp_b2[1], zb, cmp_b2[1]])[None]
    del zk
    return pe8, w1flat, b1, w1cat, w2k.astype(bf16), b2k, w2v.astype(bf16), b2v


def _compress_slab(a, c, gp, prm_refs, out_refs):
    pe_ref, w1f_ref, b1_ref, w1_ref, w2k_ref, b2k_ref, w2v_ref, b2v_ref = prm_refs
    kd_ref, vl_ref, vh_ref = out_refs
    pet = jnp.dot(pe_ref[c].astype(bf16), w1f_ref[c], preferred_element_type=f32)[0:1] + b1_ref[c]
    bias = jnp.concatenate([pet, pet], axis=1)
    p = jnp.dot(a, w1_ref[c], preferred_element_type=f32)
    s = p.shape[0]
    hid = p[:, :2 * CMP_HIDDEN] + pltpu.roll(p[:, 2 * CMP_HIDDEN:], s - 1, axis=0)
    hid = _silu(hid + bias).astype(bf16)
    w = 4 * HEAD_DIM
    if c == 0:
        kd_ref[:, gp * w:(gp + 1) * w] = (jnp.dot(hid, w2k_ref[...], preferred_element_type=f32)
                                          + b2k_ref[...]).astype(bf16)
    else:
        o = jnp.dot(hid, w2v_ref[...], preferred_element_type=f32) + b2v_ref[...]
        vl_ref[:, gp * w:(gp + 1) * w] = o[:, :w].astype(bf16)
        vh_ref[:, gp * w:(gp + 1) * w] = o[:, w:].astype(bf16)


def _compress_token_minor(pair, n_pairs, perm_ref, prm_refs, out_refs, o_ref):
    rows = 2 * SEG_PER_PAGE
    for k in range(n_pairs):
        o_ref[k] = _nt(perm_ref[...], pair(k).astype(bf16)).astype(bf16)
    nseg = n_pairs * rows
    for c in range(2):
        for gp in range(2):
            l0 = c * (KV_W // 2) + gp * LANES
            a = jnp.concatenate([o_ref[:, j * rows:(j + 1) * rows, l0:l0 + LANES].reshape(nseg, LANES)
                                 for j in range(CMP_STRIDE)], axis=1)
            _compress_slab(a, c, gp, prm_refs, out_refs)


def _compress_prompt_kernel(x_ref, perm_ref, *refs):
    chunk = 2 * PAGE_SIZE
    _compress_token_minor(lambda k: x_ref[:, k * chunk:(k + 1) * chunk], SEQ // chunk, perm_ref,
                          refs[:8], refs[8:11], refs[12])
    vl_ref, vt_ref = refs[9], refs[11]
    for g in range(N_KV):
        v_pad = vl_ref[:, g * LANES:(g + 1) * LANES].astype(f32)
        vt_ref[g * HEAD_DIM:(g + 1) * HEAD_DIM, :] = v_pad.T[:HEAD_DIM].astype(bf16)


def _seg_perm():
    m = np.zeros((2 * PAGE_SIZE, 2 * PAGE_SIZE), np.float32)
    for pp in range(2):
        for s in range(SEG_PER_PAGE):
            for j in range(CMP_STRIDE):
                m[j * 2 * SEG_PER_PAGE + pp * SEG_PER_PAGE + s, pp * PAGE_SIZE + s * CMP_STRIDE + j] = 1.0
    return jnp.asarray(m, bf16)


def _compress_sample_kernel(pt_ref, *refs):
    del pt_ref
    pages, perm_ref = refs[:N_PAGES], refs[N_PAGES]
    prm_refs, out_refs, o_ref = refs[N_PAGES + 1:N_PAGES + 9], refs[N_PAGES + 9:N_PAGES + 12], refs[-1]
    _compress_token_minor(lambda k: jnp.concatenate([pages[2 * k][...], pages[2 * k + 1][...]], axis=1),
                          N_PAGES // 2, perm_ref, prm_refs, out_refs, o_ref)


def _const_spec(a, n_grid, n_prefetch=0):
    zeros = (0,) * a.ndim
    if n_grid == 1:
        return pl.BlockSpec(a.shape, (lambda b, *_: zeros))
    return pl.BlockSpec(a.shape, (lambda b, c, *_: zeros))


def _compress_prompt(kc_t, cw):
    nseg = SEQ // CMP_STRIDE
    perm = _seg_perm()
    out = jax.ShapeDtypeStruct((BATCH, nseg, KV_W), bf16)
    vt_rows = N_KV * HEAD_DIM
    return pl.pallas_call(
        _compress_prompt_kernel,
        grid=(BATCH,),
        in_specs=[pl.BlockSpec((None, KV_W, SEQ), lambda b: (b, 0, 0)), _const_spec(perm, 1)]
                 + [_const_spec(a, 1) for a in cw],
        out_specs=[pl.BlockSpec((None, nseg, KV_W), lambda b: (b, 0, 0))] * 3
                  + [pl.BlockSpec((None, vt_rows, nseg), lambda b: (b, 0, 0))],
        out_shape=[out] * 3 + [jax.ShapeDtypeStruct((BATCH, vt_rows, nseg), bf16)],
        scratch_shapes=[pltpu.VMEM((SEQ // (2 * PAGE_SIZE), 2 * PAGE_SIZE, KV_W), bf16)],
        compiler_params=_cparams(("parallel",)),
        name="compress_prompt",
    )(kc_t, perm, *cw)


def _compress_sample(cache, page_table, cw):
    nseg = PAST_LEN // CMP_STRIDE
    x = _token_minor(cache)
    perm = _seg_perm()
    out = jax.ShapeDtypeStruct((DEC_BATCH, nseg, KV_W), bf16)
    page_specs = [pl.BlockSpec((None, KV_W, PAGE_SIZE), functools.partial(lambda b, pt, p: (pt[b * N_PAGES + p], 0, 0), p=p))
                  for p in range(N_PAGES)]
    grid_spec = pltpu.PrefetchScalarGridSpec(
        num_scalar_prefetch=1,
        grid=(DEC_BATCH,),
        in_specs=page_specs + [_const_spec(perm, 1)] + [_const_spec(a, 1) for a in cw],
        out_specs=[pl.BlockSpec((None, nseg, KV_W), lambda b, pt: (b, 0, 0))] * 3,
        scratch_shapes=[pltpu.VMEM((N_PAGES // 2, 2 * PAGE_SIZE, KV_W), bf16)])
    return pl.pallas_call(
        _compress_sample_kernel,
        grid_spec=grid_spec,
        out_shape=[out] * 3,
        compiler_params=_cparams(("arbitrary",)),
        name="compress_sample",
    )(page_table.reshape(-1), *([x] * N_PAGES), perm, *cw)


def _token_minor(kv):
    n, t = kv.shape[0], kv.shape[1]
    return kv.transpose(0, 2, 3, 4, 1).reshape(n, KV_W, t)


TQ = 256
N_SEL = SEQ // SEL_BLOCK


def _half_mask(shape, hi):
    lane = lax.broadcasted_iota(jnp.int32, shape, 1)
    return (lane >= HEAD_DIM) if hi else (lane < HEAD_DIM)


def _cmp_branch(i, qs_ref, kd_ref, vt_ref, mt_ref, o_ref):
    t0 = i * TQ
    ncmp = kd_ref.shape[0]
    nidx = lax.broadcasted_iota(jnp.int32, (ncmp, Q_PER_KV * TQ), 0)
    tpos = t0 + (lax.broadcasted_iota(jnp.int32, (ncmp, Q_PER_KV * TQ), 1) & (TQ - 1))
    mask = nidx * CMP_STRIDE + (CMP_BLOCK - 1) <= tpos
    s = jnp.where(mask, _nt(kd_ref[...], qs_ref[:, 0:LANES]), NEG)
    m = jnp.max(s, axis=0, keepdims=True)
    ex = jnp.where(mask, jnp.exp2(s - m), 0.0)
    p = ex * (1.0 / jnp.maximum(jnp.sum(ex, axis=0, keepdims=True), 1e-30))
    pb = p.astype(bf16)
    vt = vt_ref[...]
    heads = [jnp.dot(vt, pb[:, r * TQ:(r + 1) * TQ], preferred_element_type=f32) for r in range(Q_PER_KV)]
    o_ref[...] = jnp.concatenate(heads, axis=0).T
    psum = sum(p[:, r * TQ:(r + 1) * TQ] for r in range(Q_PER_KV))
    hi, lo = _split2(psum)
    imp = (jnp.dot(mt_ref[...], hi, preferred_element_type=f32)
           + jnp.dot(mt_ref[...], lo, preferred_element_type=f32))
    imp = imp[:N_SEL]
    j = lax.broadcasted_iota(jnp.int32, (N_SEL, TQ), 0)
    jt = (t0 + lax.broadcasted_iota(jnp.int32, (N_SEL, TQ), 1)) // SEL_BLOCK
    imp = jnp.where((j == 0) | (j == jt) | (j == jt - 1), SELECT_FORCE, imp)
    imp = jnp.where(j > jt, -SELECT_FORCE, imp)
    cnt = jnp.zeros((N_SEL, TQ), f32)
    for jp in range(N_SEL):
        row = imp[jp:jp + 1, :]
        cnt = cnt + jnp.where(j > jp, jnp.where(row >= imp, 1.0, 0.0), jnp.where(row > imp, 1.0, 0.0))
    selm = jnp.where((cnt < SEL_TOPN) & (j <= jt), 0.0, SEL_NEG)
    return jnp.concatenate([selm, selm], axis=0).T


def _cmp_to_sel_matrix_t(n_cmp_pad, n_slc, n_cmp):
    ratio = SEL_BLOCK // CMP_STRIDE
    i = np.arange(n_cmp_pad)[None, :]
    jj = np.arange(n_slc)[:, None]
    diff = i - ratio * jj
    mat = np.zeros((n_slc, n_cmp_pad), np.float32)
    for n in range(CMP_BLOCK // CMP_STRIDE):
        mat += ((diff + n >= 0) & (diff + n < ratio)).astype(np.float32)
    mat[:, n_cmp:] = 0.0
    return mat


VT_ROWS = HEAD_DIM + 16
SEL_KT = 4

def _flash_tile(qs, k, vt, mask, m, acc):
    s = _nt(k, qs)
    if mask is not None:
        s = jnp.where(mask, s, NEG)
    m_next = jnp.maximum(m, jnp.max(s, axis=0, keepdims=True))
    alpha = jnp.exp2(m - m_next)
    p = jnp.exp2(s - m_next).astype(bf16)
    pv = jnp.concatenate([jnp.dot(vt, p[:, r * TQ:(r + 1) * TQ], preferred_element_type=f32)
                          for r in range(Q_PER_KV)], axis=1)
    return m_next, alpha * acc + pv


def _flash_finish(acc, o_ref):
    o_t = acc[:HEAD_DIM] * (1.0 / acc[HEAD_DIM:HEAD_DIM + 1])
    heads = jnp.concatenate([o_t[:, r * TQ:(r + 1) * TQ] for r in range(Q_PER_KV)], axis=0)
    o_ref[...] = heads.T


LOG2E = math.log2(math.e)


def _stack_queries(q_ref, qs_ref):
    for r in range(Q_PER_KV):
        q2 = q_ref[:, (r // 2) * LANES:(r // 2 + 1) * LANES] * (ATTN_SCALE * LOG2E)
        qs_ref[r * TQ:(r + 1) * TQ, 0:LANES] = jnp.where(_half_mask((TQ, LANES), r % 2), q2, 0.0).astype(bf16)


def _flash_init():
    return jnp.full((1, Q_PER_KV * TQ), NEG, f32), jnp.zeros((VT_ROWS, Q_PER_KV * TQ), f32)


def _sel_branch(i, selm, kd_ref, oh_ref, vt_ref, o_ref, qs_ref, m_ref, acc_ref):
    selm = selm.astype(bf16)
    for r in range(Q_PER_KV):
        qs_ref[r * TQ:(r + 1) * TQ, LANES:2 * LANES] = selm
    qs = qs_ref[...]

    def span(first, n, mask, m, acc):
        rows = pl.ds(pl.multiple_of(first * TQ, TQ), n * TQ)
        k = jnp.concatenate([kd_ref[rows, :], oh_ref[rows, :]], axis=1)
        return _flash_tile(qs, k, _with_ones(vt_ref[:, rows]), mask, m, acc)

    full = i // SEL_KT
    m, acc = lax.fori_loop(0, full, lambda t, carry: span(SEL_KT * t, SEL_KT, None, *carry), _flash_init())
    m_ref[...] = m
    acc_ref[...] = acc
    for n in range(1, SEL_KT + 1):
        @pl.when(i % SEL_KT == n - 1)
        def _(n=n):
            key = lax.broadcasted_iota(jnp.int32, (n * TQ, Q_PER_KV * TQ), 0)
            qry = (lax.broadcasted_iota(jnp.int32, (n * TQ, Q_PER_KV * TQ), 1) & (TQ - 1)) + (n - 1) * TQ
            _, acc_n = span(full * SEL_KT, n, key <= qry, m_ref[...], acc_ref[...])
            acc_ref[...] = acc_n

    _flash_finish(acc_ref[...], o_ref)


def _win_branch(i, kd_ref, vt_ref, o_ref, qs_ref, acc_ref):
    qs = qs_ref[:, 0:LANES]

    def span(first, n):
        rows = pl.ds(pl.multiple_of(first * TQ, TQ), n * TQ)
        key = lax.broadcasted_iota(jnp.int32, (n * TQ, Q_PER_KV * TQ), 0)
        qry = (lax.broadcasted_iota(jnp.int32, (n * TQ, Q_PER_KV * TQ), 1) & (TQ - 1)) + (n - 1) * TQ
        dist = qry - key
        _, acc = _flash_tile(qs, kd_ref[rows, :], _with_ones(vt_ref[:, rows]), (dist >= 0) & (dist < WINDOW),
                             *_flash_init())
        acc_ref[...] = acc

    n_back = WINDOW // TQ

    @pl.when(i >= n_back)
    def _():
        span(i - n_back, n_back + 1)

    for early in range(n_back):
        @pl.when(i == early)
        def _(early=early):
            span(0, early + 1)

    _flash_finish(acc_ref[...], o_ref)


def _with_ones(vt):
    return jnp.concatenate([vt.astype(bf16), jnp.ones((VT_ROWS - HEAD_DIM, vt.shape[1]), bf16)], axis=0)


def _prompt_attn_kernel(q_ref, kdc_ref, vtc_ref, mt_ref, kds_ref, oh_ref, vts_ref, kdw_ref, vtw_ref,
                        oc_ref, os_ref, ow_ref, qs_ref, m_ref, acc_ref):
    i = pl.program_id(2)
    _stack_queries(q_ref, qs_ref)
    selm = _cmp_branch(i, qs_ref, kdc_ref, vtc_ref, mt_ref, oc_ref)
    _sel_branch(i, selm, kds_ref, oh_ref, vts_ref, os_ref, qs_ref, m_ref, acc_ref)
    _win_branch(i, kdw_ref, vtw_ref, ow_ref, qs_ref, acc_ref)


def _prompt_attention(q, kd_c, vt_c, kd_s, kv_s_t, kd_w, kv_w_t):
    assert WINDOW % TQ == 0
    ncmp = SEQ // CMP_STRIDE
    mt = np.zeros((LANES, ncmp), np.float32)
    mt[:N_SEL] = _cmp_to_sel_matrix_t(ncmp, N_SEL, ncmp - 1)
    mt = jnp.asarray(mt, bf16)
    blk = np.arange(SEQ)[:, None] // SEL_BLOCK == np.arange(LANES)[None, :]
    oh = jnp.asarray(blk.astype(np.float32), bf16)
    nq = SEQ // TQ
    q_spec = pl.BlockSpec((TQ, 2 * LANES), lambda b, g, i: (b * nq + i, g))
    cmp_spec = pl.BlockSpec((None, ncmp, LANES), lambda b, g, i: (b, 0, g))
    cmp_vt_spec = pl.BlockSpec((None, HEAD_DIM, ncmp), lambda b, g, i: (b, g, 0))
    kd_spec = pl.BlockSpec((SEQ, LANES), lambda b, g, i: (b, g))
    vt_spec = pl.BlockSpec((None, HEAD_DIM, SEQ), lambda b, g, i: (b, N_KV + g, 0))
    const = lambda a: pl.BlockSpec(a.shape, lambda b, g, i: (0, 0))
    out = jax.ShapeDtypeStruct((BATCH * SEQ, ATTN_W), f32)
    return pl.pallas_call(
        _prompt_attn_kernel,
        grid=(BATCH, N_KV, nq),
        in_specs=[q_spec, cmp_spec, cmp_vt_spec, const(mt), kd_spec, const(oh), vt_spec, kd_spec, vt_spec],
        out_specs=[q_spec, q_spec, q_spec],
        out_shape=[out, out, out],
        scratch_shapes=[pltpu.VMEM((Q_PER_KV * TQ, 2 * LANES), bf16), pltpu.VMEM((1, Q_PER_KV * TQ), f32),
                        pltpu.VMEM((VT_ROWS, Q_PER_KV * TQ), f32)],
        compiler_params=_cparams(("parallel", "parallel", "parallel")),
        name="prompt_attention",
    )(q, kd_c, vt_c, mt, kd_s, oh, kv_s_t, kd_w, kv_w_t)


SSD_Q = 256
HALF_INNER = D_INNER // N_SSM_GROUPS
BC_W = N_SSM_GROUPS * D_STATE


def _mamba_consts(conv_w, conv_b, dt_bias, a_log, d_skip, ssm_norm):
    pad = lambda v: jnp.zeros((1, GD_W), f32).at[0, DT_LANE0:DT_LANE0 + N_SSM_HEADS].set(v)
    e16 = np.zeros((GD_W, D_INNER), np.float32)
    for h in range(N_SSM_HEADS):
        e16[DT_LANE0 + h, h * SSM_HEAD_DIM:(h + 1) * SSM_HEAD_DIM] = 1.0
    tri = np.tril(np.ones((SSD_Q, SSD_Q), np.float32))
    return (conv_w, conv_b[None], pad(dt_bias), pad(a_log), jnp.asarray(e16, bf16),
            jnp.repeat(d_skip, SSM_HEAD_DIM)[None], ssm_norm[None], jnp.asarray(tri, bf16))


def _dt_and_decay(gd, dtb_ref, alog_ref):
    lane = lax.broadcasted_iota(jnp.int32, gd.shape, 1)
    live = (lane >= DT_LANE0) & (lane < DT_LANE0 + N_SSM_HEADS)
    dt = jnp.where(live, _softplus(gd + dtb_ref[...]), 0.0)
    return dt, dt * (-jnp.exp(alog_ref[...]))


def _gated_group_norm(y, z, nw_ref):
    y = y * _silu(z)
    outs = []
    for g in range(N_SSM_GROUPS):
        yg = y[:, g * HALF_INNER:(g + 1) * HALF_INNER]
        outs.append(yg * lax.rsqrt(jnp.mean(yg * yg, axis=-1, keepdims=True) + NORM_EPS))
    return jnp.concatenate(outs, axis=1) * nw_ref[...]


def _mamba_prompt_kernel(z_ref, xbc_ref, gd_ref, cw_ref, cb_ref, dtb_ref, alog_ref, e16_ref, dsk_ref, nw_ref, tri_ref,
                         y_ref, st_ref, xpad_ref, state_ref):
    c = pl.program_id(1)
    nq = SSD_Q

    @pl.when(c == 0)
    def _():
        state_ref[...] = jnp.zeros(state_ref.shape, f32)
        xpad_ref[0:8, :] = jnp.zeros((8, CONV_DIM), f32)

    xpad_ref[8:8 + nq, :] = xbc_ref[...]
    conv = cb_ref[...]
    for w in range(CONV_W):
        conv = conv + xpad_ref[8 - (CONV_W - 1) + w:8 - (CONV_W - 1) + w + nq, :] * cw_ref[w:w + 1, :]
    xpad_ref[0:8, :] = xpad_ref[nq:nq + 8, :]
    act = _silu(conv)
    xs, bm, cm = act[:, :D_INNER], act[:, D_INNER:D_INNER + BC_W], act[:, D_INNER + BC_W:]

    dt, a = _dt_and_decay(gd_ref[...], dtb_ref, alog_ref)
    a_cs = sum(jnp.dot(tri_ref[...], t, preferred_element_type=f32) for t in _split3(a))
    ea = jnp.exp(a_cs)
    te = jnp.exp(a_cs[nq - 1:nq, :] - a_cs)
    e16 = e16_ref[...]
    dt_x, ea_x, te_x = _dot_exact(dt, e16), _dot_exact(ea, e16), _dot_exact(te, e16)
    xdt = xs * dt_x
    xw = (xdt * te_x).astype(bf16)
    a_cst = a_cs.T
    tril = lax.broadcasted_iota(jnp.int32, (nq, nq), 1) <= lax.broadcasted_iota(jnp.int32, (nq, nq), 0)

    ys = []
    for g in range(N_SSM_GROUPS):
        bg = bm[:, g * D_STATE:(g + 1) * D_STATE]
        cgb = cm[:, g * D_STATE:(g + 1) * D_STATE].astype(bf16)
        cb = _nt(cgb, bg.astype(bf16))
        st = state_ref[g]
        yoff = jnp.dot(cgb, st.astype(bf16), preferred_element_type=f32)
        gl = slice(g * HALF_INNER, (g + 1) * HALF_INNER)
        state_ref[g] = ea_x[nq - 1:nq, gl] * st + jnp.dot(bg.T.astype(bf16), xw[:, gl], preferred_element_type=f32)
        for hp in range(HALF_INNER // LANES):
            pl_ = slice(g * HALF_INNER + hp * LANES, g * HALF_INNER + (hp + 1) * LANES)
            xp = xdt[:, pl_]
            yp = ea_x[:, pl_] * yoff[:, hp * LANES:(hp + 1) * LANES] + dsk_ref[:, pl_] * xs[:, pl_]
            for e in range(2):
                lane_h = DT_LANE0 + g * (N_SSM_HEADS // N_SSM_GROUPS) + hp * 2 + e
                seg = a_cs[:, lane_h:lane_h + 1] - a_cst[lane_h:lane_h + 1, :]
                wgt = (cb * jnp.exp(jnp.where(tril, seg, NEG))).astype(bf16)
                xz = jnp.where(_half_mask((nq, LANES), e), xp, 0.0).astype(bf16)
                yp = yp + jnp.dot(wgt, xz, preferred_element_type=f32)
            ys.append(yp)
    y = jnp.concatenate(ys, axis=1)
    y_ref[...] = _gated_group_norm(y, z_ref[...], nw_ref).astype(bf16)

    @pl.when(c == pl.num_programs(1) - 1)
    def _():
        st_ref[...] = state_ref[...]


def _mamba_prompt(z, xbc, gd, mc):
    nc = SEQ // SSD_Q
    row = lambda w: pl.BlockSpec((SSD_Q, w), lambda b, c: (b * nc + c, 0))
    y, st = pl.pallas_call(
        _mamba_prompt_kernel,
        grid=(BATCH, nc),
        in_specs=[row(D_INNER), row(CONV_DIM), row(GD_W)] + [_const_spec(a, 2) for a in mc],
        out_specs=[row(D_INNER), pl.BlockSpec((None, N_SSM_GROUPS, D_STATE, HALF_INNER), lambda b, c: (b, 0, 0, 0))],
        out_shape=[jax.ShapeDtypeStruct((BATCH * SEQ, D_INNER), bf16),
                   jax.ShapeDtypeStruct((BATCH, N_SSM_GROUPS, D_STATE, HALF_INNER), f32)],
        scratch_shapes=[pltpu.VMEM((SSD_Q + 8, CONV_DIM), f32), pltpu.VMEM((N_SSM_GROUPS, D_STATE, HALF_INNER), f32)],
        compiler_params=_cparams(("parallel", "arbitrary")),
        name="mamba_prompt",
    )(z, xbc, gd, *mc)
    hpg = N_SSM_HEADS // N_SSM_GROUPS
    st = st.reshape(BATCH, N_SSM_GROUPS, D_STATE, hpg, SSM_HEAD_DIM).transpose(0, 1, 3, 4, 2)
    return y, st.reshape(BATCH, N_SSM_HEADS, SSM_HEAD_DIM, D_STATE)


def _mamba_sample_kernel(z_ref, xbc_ref, gd_ref, sc_ref, s_ref, cw_ref, cb_ref, dtb_ref, alog_ref, e16_ref, dsk_ref,
                         nw_ref, y_ref, so_ref):
    conv = cb_ref[...] + xbc_ref[...] * cw_ref[CONV_W - 1:CONV_W, :]
    for w in range(CONV_W - 1):
        conv = conv + sc_ref[w:w + 1, :] * cw_ref[w:w + 1, :]
    act = _silu(conv)
    xs, bm, cm = act[:, :D_INNER], act[:, D_INNER:D_INNER + BC_W], act[:, D_INNER + BC_W:]
    dt, a = _dt_and_decay(jnp.broadcast_to(gd_ref[...], (8, GD_W)), dtb_ref, alog_ref)
    e16 = e16_ref[...]
    dt_x = _dot_exact(dt, e16)[0:1]
    da_x = _dot_exact(jnp.exp(a), e16)[0:1]
    rows = N_SSM_HEADS * SSM_HEAD_DIM
    xcol = jnp.broadcast_to(xs * dt_x, (D_STATE, rows)).T
    acol = jnp.broadcast_to(da_x, (D_STATE, rows)).T
    rowi = lax.broadcasted_iota(jnp.int32, (rows, D_STATE), 0)
    bfull = jnp.where(rowi < HALF_INNER, bm[:, :D_STATE], bm[:, D_STATE:])
    snew = s_ref[...] * acol + xcol * bfull
    so_ref[...] = snew
    r8 = lax.broadcasted_iota(jnp.int32, (8, D_STATE), 0)
    c8 = jnp.where(r8 == 0, cm[:, :D_STATE], jnp.where(r8 == 1, cm[:, D_STATE:], 0.0)).astype(bf16)
    yy = _nt(c8, snew.astype(bf16))
    lane = lax.broadcasted_iota(jnp.int32, (1, rows), 1)
    y = jnp.where(lane < HALF_INNER, yy[0:1], yy[1:2]) + dsk_ref[...] * xs
    y_ref[...] = _gated_group_norm(y, z_ref[...], nw_ref).astype(bf16)


def _mamba_sample(z, xbc, gd, state_conv, state_ssm, mc):
    nb = z.shape[0]
    mc = mc[:7]
    rows = N_SSM_HEADS * SSM_HEAD_DIM
    one = lambda w: pl.BlockSpec((None, 1, w), lambda b: (b, 0, 0))
    s_spec = pl.BlockSpec((None, rows, D_STATE), lambda b: (b, 0, 0))
    y, so = pl.pallas_call(
        _mamba_sample_kernel,
        grid=(nb,),
        in_specs=[one(D_INNER), one(CONV_DIM), one(GD_W),
                  pl.BlockSpec((None, CONV_W - 1, CONV_DIM), lambda b: (b, 0, 0)), s_spec]
                 + [_const_spec(a, 1) for a in mc],
        out_specs=[one(D_INNER), s_spec],
        out_shape=[jax.ShapeDtypeStruct((nb, 1, D_INNER), bf16), jax.ShapeDtypeStruct((nb, rows, D_STATE), f32)],
        compiler_params=_cparams(("parallel",)),
        name="mamba_sample",
    )(z[:, None], xbc[:, None], gd[:, None], state_conv, state_ssm.reshape(nb, rows, D_STATE), *mc)
    return y[:, 0], so.reshape(nb, N_SSM_HEADS, SSM_HEAD_DIM, D_STATE)


N_CMP_S = PAST_LEN // CMP_STRIDE - 1
N_SLC_S = PAST_LEN // SEL_BLOCK + 1
N_SLC_PAD = 2 * LANES
HALF_PAGE = PAGE_SIZE // SEL_BLOCK
GONE = -3.0e38


def _cmp_attn_sample_kernel(qz_ref, kd_ref, vl_ref, vh_ref, mts_ref, o_ref, imp_ref):
    qz = (qz_ref[...] * ATTN_SCALE).astype(bf16)
    ncmp = kd_ref.shape[0]
    nidx = lax.broadcasted_iota(jnp.int32, (N_HEADS, ncmp), 1)
    rowh = lax.broadcasted_iota(jnp.int32, (N_HEADS, ncmp), 0)
    mask = (nidx < N_CMP_S) & (nidx * CMP_STRIDE + (CMP_BLOCK - 1) <= PAST_LEN)
    rowo = lax.broadcasted_iota(jnp.int32, (N_HEADS, LANES), 0)
    r8 = lax.broadcasted_iota(jnp.int32, (8, N_SLC_PAD), 0)
    o_acc = jnp.zeros((N_HEADS, LANES), f32)
    imp = jnp.zeros((8, N_SLC_PAD), f32)
    for g in range(N_KV):
        gl = slice(g * LANES, (g + 1) * LANES)
        s = jnp.where(mask, _nt(qz, kd_ref[:, gl]), NEG)
        m = jnp.max(s, axis=1, keepdims=True)
        ex = jnp.where(mask, jnp.exp(s - m), 0.0)
        p = ex * (1.0 / jnp.maximum(jnp.sum(ex, axis=1, keepdims=True), 1e-30))
        p = jnp.where(rowh // Q_PER_KV == g, p, 0.0)
        pb = p.astype(bf16)
        o_l = jnp.dot(pb, vl_ref[:, gl], preferred_element_type=f32)
        o_h = jnp.dot(pb, vh_ref[:, gl], preferred_element_type=f32)
        o_acc = o_acc + jnp.where(rowo % 2 == 0, o_l, o_h)
        hi, lo = _split2(p)
        imp_h = jnp.dot(hi, mts_ref[...], preferred_element_type=f32) + jnp.dot(lo, mts_ref[...], preferred_element_type=f32)
        imp = imp + jnp.where(r8 == g, jnp.sum(imp_h, axis=0, keepdims=True), 0.0)
    o_ref[...] = o_acc
    imp_ref[...] = imp


def _topk_sample_kernel(imp_ref, idx_ref):
    rows = imp_ref.shape[0]
    j = lax.broadcasted_iota(jnp.int32, (rows, N_SLC_PAD), 1)
    jt = PAST_LEN // SEL_BLOCK
    imp = imp_ref[...]
    imp = jnp.where((j == 0) | (j == jt) | (j == jt - 1), SELECT_FORCE, imp)
    imp = jnp.where(j > jt, -SELECT_FORCE, imp)
    imp = jnp.where(j >= N_SLC_S, NEG, imp)
    jf = j.astype(f32)
    lane = lax.broadcasted_iota(jnp.int32, (rows, LANES), 1)
    picked = jnp.zeros((rows, LANES), f32)
    for k in range(SEL_TOPN):
        m = jnp.max(imp, axis=1, keepdims=True)
        ix = jnp.min(jnp.where(imp == m, jf, float(N_SLC_PAD)), axis=1, keepdims=True)
        picked = jnp.where(lane == k, ix, picked)
        imp = jnp.where(jf == ix, GONE, imp)
    idx_ref[...] = picked.astype(jnp.int32)


def _cmp_attn_sample(qz, kd, vl, vh):
    nb, ncmp = kd.shape[0], kd.shape[1]
    mts = np.zeros((ncmp, N_SLC_PAD), np.float32)
    mts[:, :N_SLC_S] = _cmp_to_sel_matrix_t(ncmp, N_SLC_S, N_CMP_S).T
    mts = jnp.asarray(mts, bf16)
    kv_spec = pl.BlockSpec((None, ncmp, KV_W), lambda b: (b, 0, 0))
    o16, imp = pl.pallas_call(
        _cmp_attn_sample_kernel,
        grid=(nb,),
        in_specs=[pl.BlockSpec((None, N_HEADS, LANES), lambda b: (b, 0, 0)), kv_spec, kv_spec, kv_spec,
                  pl.BlockSpec(mts.shape, lambda b: (0, 0))],
        out_specs=[pl.BlockSpec((None, N_HEADS, LANES), lambda b: (b, 0, 0)),
                   pl.BlockSpec((None, 8, N_SLC_PAD), lambda b: (b, 0, 0))],
        out_shape=[jax.ShapeDtypeStruct((nb, N_HEADS, LANES), f32), jax.ShapeDtypeStruct((nb, 8, N_SLC_PAD), f32)],
        compiler_params=_cparams(("parallel",)),
        name="cmp_attn_sample",
    )(qz, kd, vl, vh, mts)
    idx = pl.pallas_call(
        _topk_sample_kernel,
        out_shape=jax.ShapeDtypeStruct((nb * 8, LANES), jnp.int32),
        name="topk_sample",
    )(imp.reshape(nb * 8, N_SLC_PAD))
    return o16, idx.reshape(nb, 8, LANES)


def _one_query_softmax(s, s_new, vt, v_new):
    m = jnp.maximum(jnp.max(s, axis=1, keepdims=True), s_new)
    p = jnp.exp(s - m)
    p_new = jnp.exp(s_new - m)
    den = jnp.sum(p, axis=1, keepdims=True) + p_new
    num = _nt(p.astype(bf16), vt) + p_new * v_new
    return num * (1.0 / den)


def _selwin_sample_kernel(idx_ref, pt_ref, *refs):
    del pt_ref
    kv = refs[:SEL_TOPN]
    q_ref, ksn_ref, win_ref, kwn_ref, os_ref, ow_ref = refs[SEL_TOPN:]
    b, g = pl.program_id(0), pl.program_id(1)
    q8 = q_ref[...] * ATTN_SCALE
    qb = q8.astype(bf16)
    kcat = jnp.concatenate([r[0] for r in kv], axis=1).astype(bf16)
    vcat = jnp.concatenate([r[1] for r in kv], axis=1).astype(bf16)
    n = SEL_TOPN * PAGE_SIZE
    lane = lax.broadcasted_iota(jnp.int32, (8, n), 1)
    slot = lane // PAGE_SIZE
    blk = jnp.zeros((8, n), jnp.int32)
    for k in range(SEL_TOPN):
        blk = jnp.where(slot == k, idx_ref[(b * N_KV + g) * SEL_TOPN + k], blk)
    live = (blk < N_SLC_S - 1) & ((lane % PAGE_SIZE) // SEL_BLOCK == blk % HALF_PAGE)
    s = jnp.where(live, jnp.dot(qb, kcat, preferred_element_type=f32), NEG)
    k_new, v_new = ksn_ref[pl.ds(g, 1), :], ksn_ref[pl.ds(N_KV + g, 1), :]
    s_new = jnp.sum(q8 * k_new, axis=1, keepdims=True)
    os_ref[...] = _one_query_softmax(s, s_new, vcat, v_new)
    nbuf = win_ref.shape[-1]
    i = lax.broadcasted_iota(jnp.int32, (8, nbuf), 1)
    s = jnp.where(nbuf - i < WINDOW, jnp.dot(qb, win_ref[0].astype(bf16), preferred_element_type=f32), NEG)
    k_new, v_new = kwn_ref[pl.ds(g, 1), :], kwn_ref[pl.ds(N_KV + g, 1), :]
    s_new = jnp.sum(q8 * k_new, axis=1, keepdims=True)
    ow_ref[...] = _one_query_softmax(s, s_new, win_ref[1].astype(bf16), v_new)


def _selwin_sample(idx, page_table, cache_s, q_s, ks_new, win_buf, kw_new):
    nb = q_s.shape[0]
    nbuf = win_buf.shape[1]
    pages = cache_s.transpose(0, 2, 3, 4, 1)
    wb = win_buf.transpose(0, 2, 3, 4, 1)
    qg = q_s.reshape(nb, N_KV, Q_PER_KV, HEAD_DIM)
    q8 = jnp.concatenate([qg, jnp.zeros_like(qg)], axis=2)

    def blk_map(b, g, idx_ref, pt_ref, k):
        j = jnp.minimum(idx_ref[(b * N_KV + g) * SEL_TOPN + k], N_SLC_S - 2)
        return (pt_ref[b * N_PAGES + j // HALF_PAGE], 0, g, 0, 0)

    blk_specs = [pl.BlockSpec((None, 2, None, HEAD_DIM, PAGE_SIZE), functools.partial(blk_map, k=k))
                 for k in range(SEL_TOPN)]
    new_spec = pl.BlockSpec((None, 2 * N_KV, HEAD_DIM), lambda b, g, *_: (b, 0, 0))
    win_spec = pl.BlockSpec((None, 2, None, HEAD_DIM, nbuf), lambda b, g, *_: (b, 0, g, 0, 0))
    o_spec = pl.BlockSpec((None, None, 8, HEAD_DIM), lambda b, g, *_: (b, g, 0, 0))
    grid_spec = pltpu.PrefetchScalarGridSpec(
        num_scalar_prefetch=2,
        grid=(nb, N_KV),
        in_specs=blk_specs + [o_spec, new_spec, win_spec, new_spec],
        out_specs=[o_spec, o_spec])
    o_shape = jax.ShapeDtypeStruct((nb, N_KV, 8, HEAD_DIM), f32)
    ks3, kw3 = ks_new.reshape(nb, 2 * N_KV, HEAD_DIM), kw_new.reshape(nb, 2 * N_KV, HEAD_DIM)
    os8, ow8 = pl.pallas_call(
        _selwin_sample_kernel,
        grid_spec=grid_spec,
        out_shape=[o_shape, o_shape],
        compiler_params=_cparams(("arbitrary", "arbitrary")),
        name="selwin_sample",
    )(idx.reshape(-1), page_table.reshape(-1), *([pages] * SEL_TOPN), q8, ks3, wb, kw3)
    return os8[:, :, :Q_PER_KV].reshape(nb, ATTN_W), ow8[:, :, :Q_PER_KV].reshape(nb, ATTN_W)


def _sample_query_layouts(q):
    nb = q.shape[0]
    q16 = q.reshape(nb, N_HEADS, HEAD_DIM)
    z = jnp.zeros_like(q16)
    lo, hi = jnp.concatenate([q16, z], axis=-1), jnp.concatenate([z, q16], axis=-1)
    h = jnp.arange(N_HEADS)[None, :, None]
    return jnp.where(h % 2 == 0, lo, hi)


def _unpad_heads_by_parity(o16):
    nb = o16.shape[0]
    o = o16.reshape(nb, N_HEADS // 2, 2, 2, HEAD_DIM)
    return jnp.stack([o[:, :, 0, 0], o[:, :, 1, 1]], axis=2).reshape(nb, ATTN_W)


def _finish(x, ocmp, osel, owin, gd, ssd_y, w_out, ln_mlp, w_up, w_down, ln_final, tm, tm_mlp, tf):
    x1 = _outproj(x, ocmp, osel, owin, gd, ssd_y, _gate_expand(), w_out, tm)
    return _mlp(x1, ln_mlp, w_up, w_down, ln_final, tm_mlp, tf)


def kernel(x_prompt, x_sample, cache_kv_cmp, cache_kv_sel, state_kv_win, state_conv, state_ssm, page_table, ln_mix, w_in, cmp_pe, cmp_w1, cmp_b1, cmp_w2, cmp_b2, conv_w, conv_b, dt_bias, a_log, d_skip, ssm_norm, w_out, ln_mlp, w_up, w_down, ln_final):
    nb = x_sample.shape[0]
    w_in_p = _prep_w_in(w_in[0])
    w_out_b, w_up_b, w_down_b = w_out[0].astype(bf16), w_up[0].astype(bf16), w_down[0].astype(bf16)
    lnw, lnm, lnf = ln_mix[0][None], ln_mlp[0][None], ln_final[None]
    cw = _compress_weights(cmp_pe[0], cmp_w1[0], cmp_b1[0], cmp_w2[0], cmp_b2[0])
    mc = _mamba_consts(conv_w[0], conv_b[0], dt_bias[0], a_log[0], d_skip[0], ssm_norm[0])
    kv_shape = (2, N_KV, HEAD_DIM)

    xp = x_prompt.reshape(BATCH * SEQ, D_MODEL)
    q, z, xbc, gd, kct, kst, kwt, kds, kdw = _inproj_prompt(xp, lnw, w_in_p, 256)
    kd, _, _, vt = _compress_prompt(kct, cw)
    ocmp, osel, owin = _prompt_attention(q, kd, vt, kds, kst, kdw, kwt)
    ssd_y, ssm_p = _mamba_prompt(z, xbc, gd, mc)
    y_prompt = _finish(xp, ocmp, osel, owin, gd, ssd_y, w_out_b, lnm, w_up_b, w_down_b, lnf, 256, 512, 1024)

    xs = x_sample.reshape(nb, D_MODEL)
    q_s, z_s, xbc_s, gd_s, kc_s, ks_s, kw_s = _inproj(xs, lnw, w_in_p, nb)
    kd_s, vl_s, vh_s = _compress_sample(cache_kv_cmp[0], page_table, cw)
    ocmp16, idx8 = _cmp_attn_sample(_sample_query_layouts(q_s), kd_s, vl_s, vh_s)
    idx = idx8[:, :N_KV, :SEL_TOPN]
    osel_s, owin_s = _selwin_sample(idx, page_table, cache_kv_sel[0], q_s, ks_s, state_kv_win[0], kw_s)
    ssd_y_s, ssm_s = _mamba_sample(z_s, xbc_s, gd_s, state_conv[0], state_ssm[0], mc)
    y_sample = _finish(xs, _unpad_heads_by_parity(ocmp16), osel_s, owin_s, gd_s, ssd_y_s,
                       w_out_b, lnm, w_up_b, w_down_b, lnf, nb, nb, 1024)

    def token_major(t):
        return t.reshape(BATCH, *kv_shape, t.shape[-1]).transpose(0, 4, 1, 2, 3)[None]

    conv_p = xbc.reshape(BATCH, SEQ, CONV_DIM)[:, -(CONV_W - 1):]
    kv_win_s = jnp.concatenate([state_kv_win[0], kw_s.reshape(nb, 1, *kv_shape)], axis=1)[:, -WINDOW:]
    conv_s = jnp.concatenate([state_conv[0], xbc_s[:, None]], axis=1)[:, -(CONV_W - 1):]
    return (y_prompt.reshape(BATCH, SEQ, D_MODEL), y_sample.reshape(nb, 1, D_MODEL),
            token_major(kct), token_major(kst), token_major(kwt[:, :, -min(WINDOW, SEQ):]),
            conv_p[None], ssm_p[None],
            kc_s.reshape(1, nb, 1, *kv_shape), ks_s.reshape(1, nb, 1, *kv_shape), kv_win_s[None],
            conv_s[None], ssm_s[None])
```

```python
import functools
import math

import jax
import jax.numpy as jnp
import numpy as np
from jax import lax
from jax.experimental import pallas as pl
from jax.experimental.pallas import tpu as pltpu

f32 = jnp.float32
bf16 = jnp.bfloat16

D_MODEL = 2048
BATCH = 2
SEQ = 4096
DEC_BATCH = 32
PAST_LEN = 8192
PAGE_SIZE = 128
HEAD_DIM = 64
N_HEADS = 16
N_KV = 4
Q_PER_KV = 4
ATTN_W = 1024
KV_W = 512
CMP_BLOCK = 32
CMP_STRIDE = 16
CMP_HIDDEN = 128
SEL_BLOCK = 64
SEL_TOPN = 16
WINDOW = 512
SELECT_FORCE = 1.0e4
ATTN_SCALE = HEAD_DIM ** -0.5
D_INNER = 1024
SSM_HEAD_DIM = 64
N_SSM_HEADS = 16
N_SSM_GROUPS = 2
D_STATE = 128
CONV_W = 4
CONV_DIM = D_INNER + 2 * N_SSM_GROUPS * D_STATE
D_FF = 4 * D_MODEL
IN_SPLITS = (ATTN_W, KV_W, KV_W, KV_W, 3 * N_HEADS, D_INNER, CONV_DIM, N_SSM_HEADS)
NORM_EPS = 1e-5

LANES = 128
GD_W = LANES
DT_LANE0 = 3 * N_HEADS
N_PAGES = PAST_LEN // PAGE_SIZE
SEG_PER_PAGE = PAGE_SIZE // CMP_STRIDE
NEG = -1.0e30
SEL_NEG = -1.0e9
VMEM_LIMIT = 56 * 1024 * 1024


def _cparams(sem):
    return pltpu.CompilerParams(dimension_semantics=sem, vmem_limit_bytes=VMEM_LIMIT)


def _nt(a, b):
    return lax.dot_general(a, b, (((1,), (1,)), ((), ())), preferred_element_type=f32)


def _split2(x):
    hi = x.astype(bf16)
    lo = (x - hi.astype(f32)).astype(bf16)
    return hi, lo


def _split3(x):
    hi = x.astype(bf16)
    r = x - hi.astype(f32)
    mid = r.astype(bf16)
    lo = (r - mid.astype(f32)).astype(bf16)
    return hi, mid, lo


def _dot_exact(x, w):
    return sum(jnp.dot(t, w, preferred_element_type=f32) for t in _split3(x))


def _silu(x):
    return x * (1.0 / (1.0 + jnp.exp(-x)))


def _sigmoid(x):
    return 1.0 / (1.0 + jnp.exp(-x))


def _softplus(x):
    return jnp.maximum(x, 0.0) + jnp.log1p(jnp.exp(-jnp.abs(x)))


IN_OUT_WIDTHS = (ATTN_W, D_INNER, CONV_DIM, GD_W, KV_W, KV_W, KV_W)
KD_W = N_KV * LANES
IN_COLS_ROWMAJOR = sum(IN_OUT_WIDTHS)


def _normed(x_ref, lnw_ref):
    x = x_ref[...]
    return (x * lax.rsqrt(jnp.mean(x * x, axis=-1, keepdims=True) + NORM_EPS) * lnw_ref[...]).astype(bf16)


def _inproj_kernel(x_ref, lnw_ref, w_ref, *out_refs):
    h = _normed(x_ref, lnw_ref)
    off = 0
    for ref in out_refs:
        n = ref.shape[-1]
        ref[...] = jnp.dot(h, w_ref[:, off:off + n], preferred_element_type=f32)
        off += n


def _inproj_prompt_kernel(x_ref, lnw_ref, w_ref, q_ref, z_ref, xbc_ref, gd_ref,
                          kct_ref, kst_ref, kwt_ref, kds_ref, kdw_ref, t_ref):
    h = _normed(x_ref, lnw_ref)
    off = 0

    def proj(n):
        nonlocal off
        y = jnp.dot(h, w_ref[:, off:off + n], preferred_element_type=f32)
        off += n
        return y

    for ref in (q_ref, z_ref, xbc_ref, gd_ref):
        ref[...] = proj(ref.shape[-1])
    for t_out in (kct_ref, kst_ref, kwt_ref):
        t_ref[...] = proj(KV_W)
        t_out[...] = t_ref[...].T
    kds_ref[...] = proj(KD_W).astype(bf16)
    kdw_ref[...] = proj(KD_W).astype(bf16)


def _inproj(x, lnw, w_perm, tm):
    m = x.shape[0]
    return pl.pallas_call(
        _inproj_kernel,
        grid=(m // tm,),
        in_specs=[pl.BlockSpec((tm, D_MODEL), lambda i: (i, 0)),
                  pl.BlockSpec((1, D_MODEL), lambda i: (0, 0)),
                  pl.BlockSpec((D_MODEL, IN_COLS_ROWMAJOR), lambda i: (0, 0), pipeline_mode=pl.Buffered(1))],
        out_specs=[pl.BlockSpec((tm, n), lambda i: (i, 0)) for n in IN_OUT_WIDTHS],
        out_shape=[jax.ShapeDtypeStruct((m, n), f32) for n in IN_OUT_WIDTHS],
        compiler_params=_cparams(("parallel",)),
        name="inproj",
    )(x, lnw, w_perm)


def _inproj_prompt(x, lnw, w_perm, tm):
    nq = SEQ // tm
    row = lambda n: pl.BlockSpec((tm, n), lambda i: (i, 0))
    tmin = pl.BlockSpec((None, KV_W, tm), lambda i: (i // nq, 0, i % nq))
    rm = lambda n, dt=f32: jax.ShapeDtypeStruct((BATCH * SEQ, n), dt)
    tshape = jax.ShapeDtypeStruct((BATCH, KV_W, SEQ), f32)
    return pl.pallas_call(
        _inproj_prompt_kernel,
        grid=(BATCH * nq,),
        in_specs=[pl.BlockSpec((tm, D_MODEL), lambda i: (i, 0)),
                  pl.BlockSpec((1, D_MODEL), lambda i: (0, 0)),
                  pl.BlockSpec(w_perm.shape, lambda i: (0, 0), pipeline_mode=pl.Buffered(1))],
        out_specs=[row(ATTN_W), row(D_INNER), row(CONV_DIM), row(GD_W), tmin, tmin, tmin, row(KD_W), row(KD_W)],
        out_shape=[rm(ATTN_W), rm(D_INNER), rm(CONV_DIM), rm(GD_W), tshape, tshape, tshape,
                   rm(KD_W, bf16), rm(KD_W, bf16)],
        scratch_shapes=[pltpu.VMEM((tm, KV_W), f32)],
        compiler_params=_cparams(("parallel",)),
        name="inproj_prompt",
    )(x, lnw, w_perm)


def _prep_w_in(w_in):
    parts, off = [], 0
    for width in IN_SPLITS:
        parts.append(w_in[:, off:off + width])
        off += width
    q, kc, ks, kw, g, z, xbc, dt = parts
    gd = jnp.concatenate([g, dt, jnp.zeros((D_MODEL, GD_W - 4 * N_HEADS), w_in.dtype)], axis=1)

    def dup_keys(kv):
        k = kv[:, :KV_W // 2].reshape(D_MODEL, N_KV, 1, HEAD_DIM)
        return jnp.broadcast_to(k, (D_MODEL, N_KV, 2, HEAD_DIM)).reshape(D_MODEL, KD_W)

    return jnp.concatenate([q, z, xbc, gd, kc, ks, kw, dup_keys(ks), dup_keys(kw)], axis=1).astype(bf16)


def _outproj_kernel(x_ref, oc_ref, os_ref, ow_ref, gd_ref, y_ref, eg_ref, w_ref, o_ref):
    gates = _sigmoid(gd_ref[...])
    hi, lo = _split2(gates)
    attn = None
    for c, br in enumerate((oc_ref, os_ref, ow_ref)):
        ge = (jnp.dot(hi, eg_ref[c], preferred_element_type=f32)
              + jnp.dot(lo, eg_ref[c], preferred_element_type=f32))
        term = ge * br[...]
        attn = term if attn is None else attn + term
    mix = (jnp.dot(attn.astype(bf16), w_ref[:ATTN_W, :], preferred_element_type=f32)
           + jnp.dot(y_ref[...], w_ref[ATTN_W:, :], preferred_element_type=f32))
    o_ref[...] = x_ref[...] + mix


def _outproj_mixed_kernel(x_ref, a_ref, y_ref, w_ref, o_ref):
    mix = (jnp.dot(a_ref[...], w_ref[:ATTN_W, :], preferred_element_type=f32)
           + jnp.dot(y_ref[...], w_ref[ATTN_W:, :], preferred_element_type=f32))
    o_ref[...] = x_ref[...] + mix


def _outproj_mixed(x, attn, ssd_y, w_out, tm):
    m = x.shape[0]
    row = lambda w: pl.BlockSpec((tm, w), lambda i: (i, 0))
    return pl.pallas_call(
        _outproj_mixed_kernel,
        grid=(m // tm,),
        in_specs=[row(D_MODEL), row(ATTN_W), row(D_INNER),
                  pl.BlockSpec((ATTN_W + D_INNER, D_MODEL), lambda i: (0, 0))],
        out_specs=row(D_MODEL),
        out_shape=jax.ShapeDtypeStruct((m, D_MODEL), f32),
        compiler_params=_cparams(("parallel",)),
        name="outproj_mixed",
    )(x, attn, ssd_y, w_out)


def _outproj(x, ocmp, osel, owin, gd, ssd_y, eg, w_out, tm):
    m = x.shape[0]
    row = lambda w: pl.BlockSpec((tm, w), lambda i: (i, 0))
    return pl.pallas_call(
        _outproj_kernel,
        grid=(m // tm,),
        in_specs=[row(D_MODEL), row(ATTN_W), row(ATTN_W), row(ATTN_W), row(GD_W), row(D_INNER),
                  pl.BlockSpec((3, GD_W, ATTN_W), lambda i: (0, 0, 0)),
                  pl.BlockSpec((ATTN_W + D_INNER, D_MODEL), lambda i: (0, 0))],
        out_specs=row(D_MODEL),
        out_shape=jax.ShapeDtypeStruct((m, D_MODEL), f32),
        compiler_params=_cparams(("parallel",)),
        name="outproj",
    )(x, ocmp, osel, owin, gd, ssd_y, eg, w_out)


def _gate_expand():
    eg = np.zeros((3, GD_W, ATTN_W), np.float32)
    for h in range(N_HEADS):
        for c in range(3):
            eg[c, h * 3 + c, h * HEAD_DIM:(h + 1) * HEAD_DIM] = 1.0
    return jnp.asarray(eg, bf16)


def _mlp_kernel(x_ref, ln_ref, wu_ref, wd_ref, lnf_ref, o_ref, h_ref, acc_ref):
    k = pl.program_id(1)

    @pl.when(k == 0)
    def _():
        x = x_ref[...]
        h = x * lax.rsqrt(jnp.mean(x * x, axis=-1, keepdims=True) + NORM_EPS) * ln_ref[...]
        h_ref[...] = h.astype(bf16)
        acc_ref[...] = jnp.zeros_like(acc_ref)

    u = jnp.dot(h_ref[...], wu_ref[...], preferred_element_type=f32)
    u = jnp.square(jnp.maximum(u, 0.0)).astype(bf16)
    acc_ref[...] += jnp.dot(u, wd_ref[...], preferred_element_type=f32)

    @pl.when(k == pl.num_programs(1) - 1)
    def _():
        y = x_ref[...] + acc_ref[...]
        y = y * lax.rsqrt(jnp.mean(y * y, axis=-1, keepdims=True) + NORM_EPS) * lnf_ref[...]
        o_ref[...] = y


def _mlp(x, ln_mlp, w_up, w_down, ln_final, tm, tf):
    m = x.shape[0]
    return pl.pallas_call(
        _mlp_kernel,
        grid=(m // tm, D_FF // tf),
        in_specs=[pl.BlockSpec((tm, D_MODEL), lambda i, k: (i, 0)),
                  pl.BlockSpec((1, D_MODEL), lambda i, k: (0, 0)),
                  pl.BlockSpec((D_MODEL, tf), lambda i, k: (0, k)),
                  pl.BlockSpec((tf, D_MODEL), lambda i, k: (k, 0)),
                  pl.BlockSpec((1, D_MODEL), lambda i, k: (0, 0))],
        out_specs=pl.BlockSpec((tm, D_MODEL), lambda i, k: (i, 0)),
        out_shape=jax.ShapeDtypeStruct((m, D_MODEL), f32),
        scratch_shapes=[pltpu.VMEM((tm, D_MODEL), bf16), pltpu.VMEM((tm, D_MODEL), f32)],
        compiler_params=_cparams(("parallel", "arbitrary")),
        name="mlp",
    )(x, ln_mlp, w_up, w_down, ln_final)


N_SLAB_K = CMP_STRIDE * LANES


def _compress_weights(cmp_pe, cmp_w1, cmp_b1, cmp_w2, cmp_b2):
    eye2 = jnp.eye(2, dtype=f32)
    w1 = cmp_w1.reshape(2, 2, CMP_STRIDE, HEAD_DIM, CMP_HIDDEN)
    w1cat = jnp.einsum('crjdf,ab->cjadrbf', w1, eye2).reshape(2, N_SLAB_K, 4 * CMP_HIDDEN).astype(bf16)
    w1flat = cmp_w1.reshape(2, CMP_BLOCK * HEAD_DIM, CMP_HIDDEN).astype(bf16)
    pe8 = jnp.broadcast_to(cmp_pe.reshape(2, 1, CMP_BLOCK * HEAD_DIM), (2, 8, CMP_BLOCK * HEAD_DIM))
    b1 = cmp_b1.reshape(2, 1, CMP_HIDDEN)
    wk, wv = cmp_w2[0], cmp_w2[1]
    zk = jnp.zeros_like(wk)
    w2k = jnp.einsum('fd,ab,e->afbed', wk, eye2, jnp.ones((2,), f32)).reshape(2 * CMP_HIDDEN, 4 * HEAD_DIM)
    b2k = jnp.tile(cmp_b2[0], 4)[None]
    vl = jnp.einsum('fd,ab,e->afbed', wv, eye2, jnp.array([1.0, 0.0], f32)).reshape(2 * CMP_HIDDEN, 4 * HEAD_DIM)
    vh = jnp.einsum('fd,ab,e->afbed', wv, eye2, jnp.array([0.0, 1.0], f32)).reshape(2 * CMP_HIDDEN, 4 * HEAD_DIM)
    w2v = jnp.concatenate([vl, vh], axis=1)
    zb = jnp.zeros((HEAD_DIM,), f32)
    b2v = jnp.concatenate([cmp_b2[1], zb, cmp_b2[1], zb, zb, cmp_b2[1], zb, cmp_b2[1]])[None]
    del zk
    return pe8, w1flat, b1, w1cat, w2k.astype(bf16), b2k, w2v.astype(bf16), b2v


def _compress_slab(a, c, gp, prm_refs, out_refs):
    pe_ref, w1f_ref, b1_ref, w1_ref, w2k_ref, b2k_ref, w2v_ref, b2v_ref = prm_refs
    kd_ref, vl_ref, vh_ref = out_refs
    pet = jnp.dot(pe_ref[c].astype(bf16), w1f_ref[c], preferred_element_type=f32)[0:1] + b1_ref[c]
    bias = jnp.concatenate([pet, pet], axis=1)
    p = jnp.dot(a, w1_ref[c], preferred_element_type=f32)
    s = p.shape[0]
    hid = p[:, :2 * CMP_HIDDEN] + pltpu.roll(p[:, 2 * CMP_HIDDEN:], s - 1, axis=0)
    hid = _silu(hid + bias).astype(bf16)
    w = 4 * HEAD_DIM
    if c == 0:
        kd_ref[:, gp * w:(gp + 1) * w] = (jnp.dot(hid, w2k_ref[...], preferred_element_type=f32)
                                          + b2k_ref[...]).astype(bf16)
    else:
        o = jnp.dot(hid, w2v_ref[...], preferred_element_type=f32) + b2v_ref[...]
        vl_ref[:, gp * w:(gp + 1) * w] = o[:, :w].astype(bf16)
        vh_ref[:, gp * w:(gp + 1) * w] = o[:, w:].astype(bf16)


def _compress_token_minor(pair, n_pairs, perm_ref, prm_refs, out_refs, o_ref):
    rows = 2 * SEG_PER_PAGE
    for k in range(n_pairs):
        o_ref[k] = _nt(perm_ref[...], pair(k).astype(bf16)).astype(bf16)
    nseg = n_pairs * rows
    for c in range(2):
        for gp in range(2):
            l0 = c * (KV_W // 2) + gp * LANES
            a = jnp.concatenate([o_ref[:, j * rows:(j + 1) * rows, l0:l0 + LANES].reshape(nseg, LANES)
                                 for j in range(CMP_STRIDE)], axis=1)
            _compress_slab(a, c, gp, prm_refs, out_refs)


def _compress_prompt_kernel(x_ref, perm_ref, *refs):
    chunk = 2 * PAGE_SIZE
    _compress_token_minor(lambda k: x_ref[:, k * chunk:(k + 1) * chunk], SEQ // chunk, perm_ref,
                          refs[:8], refs[8:11], refs[12])
    vl_ref, vt_ref = refs[9], refs[11]
    for g in range(N_KV):
        v_pad = vl_ref[:, g * LANES:(g + 1) * LANES].astype(f32)
        vt_ref[g * HEAD_DIM:(g + 1) * HEAD_DIM, :] = v_pad.T[:HEAD_DIM].astype(bf16)


def _seg_perm():
    m = np.zeros((2 * PAGE_SIZE, 2 * PAGE_SIZE), np.float32)
    for pp in range(2):
        for s in range(SEG_PER_PAGE):
            for j in range(CMP_STRIDE):
                m[j * 2 * SEG_PER_PAGE + pp * SEG_PER_PAGE + s, pp * PAGE_SIZE + s * CMP_STRIDE + j] = 1.0
    return jnp.asarray(m, bf16)


def _compress_sample_kernel(pt_ref, *refs):
    del pt_ref
    pages, perm_ref = refs[:N_PAGES], refs[N_PAGES]
    prm_refs, out_refs, o_ref = refs[N_PAGES + 1:N_PAGES + 9], refs[N_PAGES + 9:N_PAGES + 12], refs[-1]
    _compress_token_minor(lambda k: jnp.concatenate([pages[2 * k][...], pages[2 * k + 1][...]], axis=1),
                          N_PAGES // 2, perm_ref, prm_refs, out_refs, o_ref)


def _const_spec(a, n_grid, n_prefetch=0):
    zeros = (0,) * a.ndim
    if n_grid == 1:
        return pl.BlockSpec(a.shape, (lambda b, *_: zeros))
    return pl.BlockSpec(a.shape, (lambda b, c, *_: zeros))


def _compress_prompt(kc_t, cw):
    nseg = SEQ // CMP_STRIDE
    perm = _seg_perm()
    out = jax.ShapeDtypeStruct((BATCH, nseg, KV_W), bf16)
    vt_rows = N_KV * HEAD_DIM
    return pl.pallas_call(
        _compress_prompt_kernel,
        grid=(BATCH,),
        in_specs=[pl.BlockSpec((None, KV_W, SEQ), lambda b: (b, 0, 0)), _const_spec(perm, 1)]
                 + [_const_spec(a, 1) for a in cw],
        out_specs=[pl.BlockSpec((None, nseg, KV_W), lambda b: (b, 0, 0))] * 3
                  + [pl.BlockSpec((None, vt_rows, nseg), lambda b: (b, 0, 0))],
        out_shape=[out] * 3 + [jax.ShapeDtypeStruct((BATCH, vt_rows, nseg), bf16)],
        scratch_shapes=[pltpu.VMEM((SEQ // (2 * PAGE_SIZE), 2 * PAGE_SIZE, KV_W), bf16)],
        compiler_params=_cparams(("parallel",)),
        name="compress_prompt",
    )(kc_t, perm, *cw)


def _compress_sample(cache, page_table, cw):
    nseg = PAST_LEN // CMP_STRIDE
    x = _token_minor(cache)
    perm = _seg_perm()
    out = jax.ShapeDtypeStruct((DEC_BATCH, nseg, KV_W), bf16)
    page_specs = [pl.BlockSpec((None, KV_W, PAGE_SIZE), functools.partial(lambda b, pt, p: (pt[b * N_PAGES + p], 0, 0), p=p))
                  for p in range(N_PAGES)]
    grid_spec = pltpu.PrefetchScalarGridSpec(
        num_scalar_prefetch=1,
        grid=(DEC_BATCH,),
        in_specs=page_specs + [_const_spec(perm, 1)] + [_const_spec(a, 1) for a in cw],
        out_specs=[pl.BlockSpec((None, nseg, KV_W), lambda b, pt: (b, 0, 0))] * 3,
        scratch_shapes=[pltpu.VMEM((N_PAGES // 2, 2 * PAGE_SIZE, KV_W), bf16)])
    return pl.pallas_call(
        _compress_sample_kernel,
        grid_spec=grid_spec,
        out_shape=[out] * 3,
        compiler_params=_cparams(("arbitrary",)),
        name="compress_sample",
    )(page_table.reshape(-1), *([x] * N_PAGES), perm, *cw)


def _token_minor(kv):
    n, t = kv.shape[0], kv.shape[1]
    return kv.transpose(0, 2, 3, 4, 1).reshape(n, KV_W, t)


TQ = 256
N_SEL = SEQ // SEL_BLOCK


def _half_mask(shape, hi):
    lane = lax.broadcasted_iota(jnp.int32, shape, 1)
    return (lane >= HEAD_DIM) if hi else (lane < HEAD_DIM)


def _cmp_branch(i, qs_ref, kd_ref, vt_ref, mt_ref):
    t0 = i * TQ
    ncmp = kd_ref.shape[0]
    nidx = lax.broadcasted_iota(jnp.int32, (ncmp, Q_PER_KV * TQ), 0)
    tpos = t0 + (lax.broadcasted_iota(jnp.int32, (ncmp, Q_PER_KV * TQ), 1) & (TQ - 1))
    mask = nidx * CMP_STRIDE + (CMP_BLOCK - 1) <= tpos
    s = jnp.where(mask, _nt(kd_ref[...], qs_ref[:, 0:LANES]), NEG)
    m = jnp.max(s, axis=0, keepdims=True)
    ex = jnp.where(mask, jnp.exp2(s - m), 0.0)
    p = ex * (1.0 / jnp.maximum(jnp.sum(ex, axis=0, keepdims=True), 1e-30))
    pb = p.astype(bf16)
    vt = vt_ref[...]
    heads = [jnp.dot(vt, pb[:, r * TQ:(r + 1) * TQ], preferred_element_type=f32) for r in range(Q_PER_KV)]
    heads = jnp.concatenate(heads, axis=0)
    psum = sum(p[:, r * TQ:(r + 1) * TQ] for r in range(Q_PER_KV))
    hi, lo = _split2(psum)
    imp = (jnp.dot(mt_ref[...], hi, preferred_element_type=f32)
           + jnp.dot(mt_ref[...], lo, preferred_element_type=f32))
    imp = imp[:N_SEL]
    j = lax.broadcasted_iota(jnp.int32, (N_SEL, TQ), 0)
    jt = (t0 + lax.broadcasted_iota(jnp.int32, (N_SEL, TQ), 1)) // SEL_BLOCK
    imp = jnp.where((j == 0) | (j == jt) | (j == jt - 1), SELECT_FORCE, imp)
    imp = jnp.where(j > jt, -SELECT_FORCE, imp)
    cnt = jnp.zeros((N_SEL, TQ), f32)
    for jp in range(N_SEL):
        row = imp[jp:jp + 1, :]
        cnt = cnt + jnp.where(j > jp, jnp.where(row >= imp, 1.0, 0.0), jnp.where(row > imp, 1.0, 0.0))
    selm = jnp.where((cnt < SEL_TOPN) & (j <= jt), 0.0, SEL_NEG)
    return heads, jnp.concatenate([selm, selm], axis=0).T


def _cmp_to_sel_matrix_t(n_cmp_pad, n_slc, n_cmp):
    ratio = SEL_BLOCK // CMP_STRIDE
    i = np.arange(n_cmp_pad)[None, :]
    jj = np.arange(n_slc)[:, None]
    diff = i - ratio * jj
    mat = np.zeros((n_slc, n_cmp_pad), np.float32)
    for n in range(CMP_BLOCK // CMP_STRIDE):
        mat += ((diff + n >= 0) & (diff + n < ratio)).astype(np.float32)
    mat[:, n_cmp:] = 0.0
    return mat


VT_ROWS = HEAD_DIM + 16
SEL_KT = 4

def _flash_tile(qs, k, vt, mask, m, acc):
    s = _nt(k, qs)
    if mask is not None:
        s = jnp.where(mask, s, NEG)
    m_next = jnp.maximum(m, jnp.max(s, axis=0, keepdims=True))
    alpha = jnp.exp2(m - m_next)
    p = jnp.exp2(s - m_next).astype(bf16)
    pv = jnp.concatenate([jnp.dot(vt, p[:, r * TQ:(r + 1) * TQ], preferred_element_type=f32)
                          for r in range(Q_PER_KV)], axis=1)
    return m_next, alpha * acc + pv


def _flash_finish(acc):
    o_t = acc[:HEAD_DIM] * (1.0 / acc[HEAD_DIM:HEAD_DIM + 1])
    return jnp.concatenate([o_t[:, r * TQ:(r + 1) * TQ] for r in range(Q_PER_KV)], axis=0)


LOG2E = math.log2(math.e)


def _stack_queries(q_ref, qs_ref):
    for r in range(Q_PER_KV):
        q2 = q_ref[:, (r // 2) * LANES:(r // 2 + 1) * LANES] * (ATTN_SCALE * LOG2E)
        qs_ref[r * TQ:(r + 1) * TQ, 0:LANES] = jnp.where(_half_mask((TQ, LANES), r % 2), q2, 0.0).astype(bf16)


def _flash_init():
    return jnp.full((1, Q_PER_KV * TQ), NEG, f32), jnp.zeros((VT_ROWS, Q_PER_KV * TQ), f32)


def _sel_branch(i, selm, kd_ref, oh_ref, vt_ref, qs_ref, m_ref, acc_ref):
    selm = selm.astype(bf16)
    for r in range(Q_PER_KV):
        qs_ref[r * TQ:(r + 1) * TQ, LANES:2 * LANES] = selm
    qs = qs_ref[...]

    def span(first, n, mask, m, acc):
        rows = pl.ds(pl.multiple_of(first * TQ, TQ), n * TQ)
        k = jnp.concatenate([kd_ref[rows, :], oh_ref[rows, :]], axis=1)
        return _flash_tile(qs, k, _with_ones(vt_ref[:, rows]), mask, m, acc)

    full = i // SEL_KT
    m, acc = lax.fori_loop(0, full, lambda t, carry: span(SEL_KT * t, SEL_KT, None, *carry), _flash_init())
    m_ref[...] = m
    acc_ref[...] = acc
    for n in range(1, SEL_KT + 1):
        @pl.when(i % SEL_KT == n - 1)
        def _(n=n):
            key = lax.broadcasted_iota(jnp.int32, (n * TQ, Q_PER_KV * TQ), 0)
            qry = (lax.broadcasted_iota(jnp.int32, (n * TQ, Q_PER_KV * TQ), 1) & (TQ - 1)) + (n - 1) * TQ
            _, acc_n = span(full * SEL_KT, n, key <= qry, m_ref[...], acc_ref[...])
            acc_ref[...] = acc_n

    return _flash_finish(acc_ref[...])


def _win_branch(i, kd_ref, vt_ref, qs_ref, acc_ref):
    qs = qs_ref[:, 0:LANES]

    def span(first, n):
        rows = pl.ds(pl.multiple_of(first * TQ, TQ), n * TQ)
        key = lax.broadcasted_iota(jnp.int32, (n * TQ, Q_PER_KV * TQ), 0)
        qry = (lax.broadcasted_iota(jnp.int32, (n * TQ, Q_PER_KV * TQ), 1) & (TQ - 1)) + (n - 1) * TQ
        dist = qry - key
        _, acc = _flash_tile(qs, kd_ref[rows, :], _with_ones(vt_ref[:, rows]), (dist >= 0) & (dist < WINDOW),
                             *_flash_init())
        acc_ref[...] = acc

    n_back = WINDOW // TQ

    @pl.when(i >= n_back)
    def _():
        span(i - n_back, n_back + 1)

    for early in range(n_back):
        @pl.when(i == early)
        def _(early=early):
            span(0, early + 1)

    return _flash_finish(acc_ref[...])


def _with_ones(vt):
    return jnp.concatenate([vt.astype(bf16), jnp.ones((VT_ROWS - HEAD_DIM, vt.shape[1]), bf16)], axis=0)


def _prompt_attn_kernel(q_ref, gd_ref, kdc_ref, vtc_ref, mt_ref, kds_ref, oh_ref, vts_ref, kdw_ref, vtw_ref,
                        o_ref, qs_ref, m_ref, acc_ref, gt_ref):
    g, i = pl.program_id(1), pl.program_id(2)
    _stack_queries(q_ref, qs_ref)
    gt_ref[...] = _sigmoid(gd_ref[...]).T

    def gated(heads, branch):
        rows = [gt_ref[pl.ds((g * Q_PER_KV + r) * 3 + branch, 1), :] for r in range(Q_PER_KV)]
        return jnp.concatenate([heads[r * HEAD_DIM:(r + 1) * HEAD_DIM] * rows[r] for r in range(Q_PER_KV)], axis=0)

    heads, selm = _cmp_branch(i, qs_ref, kdc_ref, vtc_ref, mt_ref)
    total = gated(heads, 0)
    total = total + gated(_sel_branch(i, selm, kds_ref, oh_ref, vts_ref, qs_ref, m_ref, acc_ref), 1)
    total = total + gated(_win_branch(i, kdw_ref, vtw_ref, qs_ref, acc_ref), 2)
    o_ref[...] = total.T.astype(bf16)


def _prompt_attention(q, gd, kd_c, vt_c, kd_s, kv_s_t, kd_w, kv_w_t):
    assert WINDOW % TQ == 0
    ncmp = SEQ // CMP_STRIDE
    mt = np.zeros((LANES, ncmp), np.float32)
    mt[:N_SEL] = _cmp_to_sel_matrix_t(ncmp, N_SEL, ncmp - 1)
    mt = jnp.asarray(mt, bf16)
    blk = np.arange(SEQ)[:, None] // SEL_BLOCK == np.arange(LANES)[None, :]
    oh = jnp.asarray(blk.astype(np.float32), bf16)
    nq = SEQ // TQ
    q_spec = pl.BlockSpec((TQ, 2 * LANES), lambda b, g, i: (b * nq + i, g))
    cmp_spec = pl.BlockSpec((None, ncmp, LANES), lambda b, g, i: (b, 0, g))
    cmp_vt_spec = pl.BlockSpec((None, HEAD_DIM, ncmp), lambda b, g, i: (b, g, 0))
    kd_spec = pl.BlockSpec((SEQ, LANES), lambda b, g, i: (b, g))
    vt_spec = pl.BlockSpec((None, HEAD_DIM, SEQ), lambda b, g, i: (b, N_KV + g, 0))
    const = lambda a: pl.BlockSpec(a.shape, lambda b, g, i: (0, 0))
    gd_spec = pl.BlockSpec((TQ, GD_W), lambda b, g, i: (b * nq + i, 0))
    return pl.pallas_call(
        _prompt_attn_kernel,
        grid=(BATCH, N_KV, nq),
        in_specs=[q_spec, gd_spec, cmp_spec, cmp_vt_spec, const(mt), kd_spec, const(oh), vt_spec, kd_spec, vt_spec],
        out_specs=q_spec,
        out_shape=jax.ShapeDtypeStruct((BATCH * SEQ, ATTN_W), bf16),
        scratch_shapes=[pltpu.VMEM((Q_PER_KV * TQ, 2 * LANES), bf16), pltpu.VMEM((1, Q_PER_KV * TQ), f32),
                        pltpu.VMEM((VT_ROWS, Q_PER_KV * TQ), f32), pltpu.VMEM((GD_W, TQ), f32)],
        compiler_params=_cparams(("parallel", "parallel", "parallel")),
        name="prompt_attention",
    )(q, gd, kd_c, vt_c, mt, kd_s, oh, kv_s_t, kd_w, kv_w_t)


SSD_Q = 256
HALF_INNER = D_INNER // N_SSM_GROUPS
BC_W = N_SSM_GROUPS * D_STATE


def _mamba_consts(conv_w, conv_b, dt_bias, a_log, d_skip, ssm_norm):
    pad = lambda v: jnp.zeros((1, GD_W), f32).at[0, DT_LANE0:DT_LANE0 + N_SSM_HEADS].set(v)
    e16 = np.zeros((GD_W, D_INNER), np.float32)
    for h in range(N_SSM_HEADS):
        e16[DT_LANE0 + h, h * SSM_HEAD_DIM:(h + 1) * SSM_HEAD_DIM] = 1.0
    tri = np.tril(np.ones((SSD_Q, SSD_Q), np.float32))
    return (conv_w, conv_b[None], pad(dt_bias), pad(a_log), jnp.asarray(e16, bf16),
            jnp.repeat(d_skip, SSM_HEAD_DIM)[None], ssm_norm[None], jnp.asarray(tri, bf16))


def _dt_and_decay(gd, dtb_ref, alog_ref):
    lane = lax.broadcasted_iota(jnp.int32, gd.shape, 1)
    live = (lane >= DT_LANE0) & (lane < DT_LANE0 + N_SSM_HEADS)
    dt = jnp.where(live, _softplus(gd + dtb_ref[...]), 0.0)
    return dt, dt * (-jnp.exp(alog_ref[...]))


def _gated_group_norm(y, z, nw_ref):
    y = y * _silu(z)
    outs = []
    for g in range(N_SSM_GROUPS):
        yg = y[:, g * HALF_INNER:(g + 1) * HALF_INNER]
        outs.append(yg * lax.rsqrt(jnp.mean(yg * yg, axis=-1, keepdims=True) + NORM_EPS))
    return jnp.concatenate(outs, axis=1) * nw_ref[...]


def _mamba_prompt_kernel(z_ref, xbc_ref, gd_ref, cw_ref, cb_ref, dtb_ref, alog_ref, e16_ref, dsk_ref, nw_ref, tri_ref,
                         y_ref, st_ref, xpad_ref, state_ref):
    c = pl.program_id(1)
    nq = SSD_Q

    @pl.when(c == 0)
    def _():
        state_ref[...] = jnp.zeros(state_ref.shape, f32)
        xpad_ref[0:8, :] = jnp.zeros((8, CONV_DIM), f32)

    xpad_ref[8:8 + nq, :] = xbc_ref[...]
    conv = cb_ref[...]
    for w in range(CONV_W):
        conv = conv + xpad_ref[8 - (CONV_W - 1) + w:8 - (CONV_W - 1) + w + nq, :] * cw_ref[w:w + 1, :]
    xpad_ref[0:8, :] = xpad_ref[nq:nq + 8, :]
    act = _silu(conv)
    xs, bm, cm = act[:, :D_INNER], act[:, D_INNER:D_INNER + BC_W], act[:, D_INNER + BC_W:]

    dt, a = _dt_and_decay(gd_ref[...], dtb_ref, alog_ref)
    a_cs = sum(jnp.dot(tri_ref[...], t, preferred_element_type=f32) for t in _split3(a))
    ea = jnp.exp(a_cs)
    te = jnp.exp(a_cs[nq - 1:nq, :] - a_cs)
    e16 = e16_ref[...]
    dt_x, ea_x, te_x = _dot_exact(dt, e16), _dot_exact(ea, e16), _dot_exact(te, e16)
    xdt = xs * dt_x
    xw = (xdt * te_x).astype(bf16)
    a_cst = a_cs.T
    tril = lax.broadcasted_iota(jnp.int32, (nq, nq), 1) <= lax.broadcasted_iota(jnp.int32, (nq, nq), 0)

    ys = []
    for g in range(N_SSM_GROUPS):
        bg = bm[:, g * D_STATE:(g + 1) * D_STATE]
        cgb = cm[:, g * D_STATE:(g + 1) * D_STATE].astype(bf16)
        cb = _nt(cgb, bg.astype(bf16))
        st = state_ref[g]
        yoff = jnp.dot(cgb, st.astype(bf16), preferred_element_type=f32)
        gl = slice(g * HALF_INNER, (g + 1) * HALF_INNER)
        state_ref[g] = ea_x[nq - 1:nq, gl] * st + jnp.dot(bg.T.astype(bf16), xw[:, gl], preferred_element_type=f32)
        for hp in range(HALF_INNER // LANES):
            pl_ = slice(g * HALF_INNER + hp * LANES, g * HALF_INNER + (hp + 1) * LANES)
            xp = xdt[:, pl_]
            yp = ea_x[:, pl_] * yoff[:, hp * LANES:(hp + 1) * LANES] + dsk_ref[:, pl_] * xs[:, pl_]
            for e in range(2):
                lane_h = DT_LANE0 + g * (N_SSM_HEADS // N_SSM_GROUPS) + hp * 2 + e
                seg = a_cs[:, lane_h:lane_h + 1] - a_cst[lane_h:lane_h + 1, :]
                wgt = (cb * jnp.exp(jnp.where(tril, seg, NEG))).astype(bf16)
                xz = jnp.where(_half_mask((nq, LANES), e), xp, 0.0).astype(bf16)
                yp = yp + jnp.dot(wgt, xz, preferred_element_type=f32)
            ys.append(yp)
    y = jnp.concatenate(ys, axis=1)
    y_ref[...] = _gated_group_norm(y, z_ref[...], nw_ref).astype(bf16)

    @pl.when(c == pl.num_programs(1) - 1)
    def _():
        st_ref[...] = state_ref[...]


def _mamba_prompt(z, xbc, gd, mc):
    nc = SEQ // SSD_Q
    row = lambda w: pl.BlockSpec((SSD_Q, w), lambda b, c: (b * nc + c, 0))
    y, st = pl.pallas_call(
        _mamba_prompt_kernel,
        grid=(BATCH, nc),
        in_specs=[row(D_INNER), row(CONV_DIM), row(GD_W)] + [_const_spec(a, 2) for a in mc],
        out_specs=[row(D_INNER), pl.BlockSpec((None, N_SSM_GROUPS, D_STATE, HALF_INNER), lambda b, c: (b, 0, 0, 0))],
        out_shape=[jax.ShapeDtypeStruct((BATCH * SEQ, D_INNER), bf16),
                   jax.ShapeDtypeStruct((BATCH, N_SSM_GROUPS, D_STATE, HALF_INNER), f32)],
        scratch_shapes=[pltpu.VMEM((SSD_Q + 8, CONV_DIM), f32), pltpu.VMEM((N_SSM_GROUPS, D_STATE, HALF_INNER), f32)],
        compiler_params=_cparams(("parallel", "arbitrary")),
        name="mamba_prompt",
    )(z, xbc, gd, *mc)
    hpg = N_SSM_HEADS // N_SSM_GROUPS
    st = st.reshape(BATCH, N_SSM_GROUPS, D_STATE, hpg, SSM_HEAD_DIM).transpose(0, 1, 3, 4, 2)
    return y, st.reshape(BATCH, N_SSM_HEADS, SSM_HEAD_DIM, D_STATE)


def _mamba_sample_kernel(z_ref, xbc_ref, gd_ref, sc_ref, s_ref, cw_ref, cb_ref, dtb_ref, alog_ref, e16_ref, dsk_ref,
                         nw_ref, y_ref, so_ref):
    conv = cb_ref[...] + xbc_ref[...] * cw_ref[CONV_W - 1:CONV_W, :]
    for w in range(CONV_W - 1):
        conv = conv + sc_ref[w:w + 1, :] * cw_ref[w:w + 1, :]
    act = _silu(conv)
    xs, bm, cm = act[:, :D_INNER], act[:, D_INNER:D_INNER + BC_W], act[:, D_INNER + BC_W:]
    dt, a = _dt_and_decay(jnp.broadcast_to(gd_ref[...], (8, GD_W)), dtb_ref, alog_ref)
    e16 = e16_ref[...]
    dt_x = _dot_exact(dt, e16)[0:1]
    da_x = _dot_exact(jnp.exp(a), e16)[0:1]
    rows = N_SSM_HEADS * SSM_HEAD_DIM
    xcol = jnp.broadcast_to(xs * dt_x, (D_STATE, rows)).T
    acol = jnp.broadcast_to(da_x, (D_STATE, rows)).T
    rowi = lax.broadcasted_iota(jnp.int32, (rows, D_STATE), 0)
    bfull = jnp.where(rowi < HALF_INNER, bm[:, :D_STATE], bm[:, D_STATE:])
    snew = s_ref[...] * acol + xcol * bfull
    so_ref[...] = snew
    r8 = lax.broadcasted_iota(jnp.int32, (8, D_STATE), 0)
    c8 = jnp.where(r8 == 0, cm[:, :D_STATE], jnp.where(r8 == 1, cm[:, D_STATE:], 0.0)).astype(bf16)
    yy = _nt(c8, snew.astype(bf16))
    lane = lax.broadcasted_iota(jnp.int32, (1, rows), 1)
    y = jnp.where(lane < HALF_INNER, yy[0:1], yy[1:2]) + dsk_ref[...] * xs
    y_ref[...] = _gated_group_norm(y, z_ref[...], nw_ref).astype(bf16)


def _mamba_sample(z, xbc, gd, state_conv, state_ssm, mc):
    nb = z.shape[0]
    mc = mc[:7]
    rows = N_SSM_HEADS * SSM_HEAD_DIM
    one = lambda w: pl.BlockSpec((None, 1, w), lambda b: (b, 0, 0))
    s_spec = pl.BlockSpec((None, rows, D_STATE), lambda b: (b, 0, 0))
    y, so = pl.pallas_call(
        _mamba_sample_kernel,
        grid=(nb,),
        in_specs=[one(D_INNER), one(CONV_DIM), one(GD_W),
                  pl.BlockSpec((None, CONV_W - 1, CONV_DIM), lambda b: (b, 0, 0)), s_spec]
                 + [_const_spec(a, 1) for a in mc],
        out_specs=[one(D_INNER), s_spec],
        out_shape=[jax.ShapeDtypeStruct((nb, 1, D_INNER), bf16), jax.ShapeDtypeStruct((nb, rows, D_STATE), f32)],
        compiler_params=_cparams(("parallel",)),
        name="mamba_sample",
    )(z[:, None], xbc[:, None], gd[:, None], state_conv, state_ssm.reshape(nb, rows, D_STATE), *mc)
    return y[:, 0], so.reshape(nb, N_SSM_HEADS, SSM_HEAD_DIM, D_STATE)


N_CMP_S = PAST_LEN // CMP_STRIDE - 1
N_SLC_S = PAST_LEN // SEL_BLOCK + 1
N_SLC_PAD = 2 * LANES
HALF_PAGE = PAGE_SIZE // SEL_BLOCK
GONE = -3.0e38


def _cmp_attn_sample_kernel(qz_ref, kd_ref, vl_ref, vh_ref, mts_ref, o_ref, imp_ref):
    qz = (qz_ref[...] * ATTN_SCALE).astype(bf16)
    ncmp = kd_ref.shape[0]
    nidx = lax.broadcasted_iota(jnp.int32, (N_HEADS, ncmp), 1)
    rowh = lax.broadcasted_iota(jnp.int32, (N_HEADS, ncmp), 0)
    mask = (nidx < N_CMP_S) & (nidx * CMP_STRIDE + (CMP_BLOCK - 1) <= PAST_LEN)
    rowo = lax.broadcasted_iota(jnp.int32, (N_HEADS, LANES), 0)
    r8 = lax.broadcasted_iota(jnp.int32, (8, N_SLC_PAD), 0)
    o_acc = jnp.zeros((N_HEADS, LANES), f32)
    imp = jnp.zeros((8, N_SLC_PAD), f32)
    for g in range(N_KV):
        gl = slice(g * LANES, (g + 1) * LANES)
        s = jnp.where(mask, _nt(qz, kd_ref[:, gl]), NEG)
        m = jnp.max(s, axis=1, keepdims=True)
        ex = jnp.where(mask, jnp.exp(s - m), 0.0)
        p = ex * (1.0 / jnp.maximum(jnp.sum(ex, axis=1, keepdims=True), 1e-30))
        p = jnp.where(rowh // Q_PER_KV == g, p, 0.0)
        pb = p.astype(bf16)
        o_l = jnp.dot(pb, vl_ref[:, gl], preferred_element_type=f32)
        o_h = jnp.dot(pb, vh_ref[:, gl], preferred_element_type=f32)
        o_acc = o_acc + jnp.where(rowo % 2 == 0, o_l, o_h)
        hi, lo = _split2(p)
        imp_h = jnp.dot(hi, mts_ref[...], preferred_element_type=f32) + jnp.dot(lo, mts_ref[...], preferred_element_type=f32)
        imp = imp + jnp.where(r8 == g, jnp.sum(imp_h, axis=0, keepdims=True), 0.0)
    o_ref[...] = o_acc
    imp_ref[...] = imp


def _topk_sample_kernel(imp_ref, idx_ref):
    rows = imp_ref.shape[0]
    j = lax.broadcasted_iota(jnp.int32, (rows, N_SLC_PAD), 1)
    jt = PAST_LEN // SEL_BLOCK
    imp = imp_ref[...]
    imp = jnp.where((j == 0) | (j == jt) | (j == jt - 1), SELECT_FORCE, imp)
    imp = jnp.where(j > jt, -SELECT_FORCE, imp)
    imp = jnp.where(j >= N_SLC_S, NEG, imp)
    jf = j.astype(f32)
    lane = lax.broadcasted_iota(jnp.int32, (rows, LANES), 1)
    picked = jnp.zeros((rows, LANES), f32)
    for k in range(SEL_TOPN):
        m = jnp.max(imp, axis=1, keepdims=True)
        ix = jnp.min(jnp.where(imp == m, jf, float(N_SLC_PAD)), axis=1, keepdims=True)
        picked = jnp.where(lane == k, ix, picked)
        imp = jnp.where(jf == ix, GONE, imp)
    idx_ref[...] = picked.astype(jnp.int32)


def _cmp_attn_sample(qz, kd, vl, vh):
    nb, ncmp = kd.shape[0], kd.shape[1]
    mts = np.zeros((ncmp, N_SLC_PAD), np.float32)
    mts[:, :N_SLC_S] = _cmp_to_sel_matrix_t(ncmp, N_SLC_S, N_CMP_S).T
    mts = jnp.asarray(mts, bf16)
    kv_spec = pl.BlockSpec((None, ncmp, KV_W), lambda b: (b, 0, 0))
    o16, imp = pl.pallas_call(
        _cmp_attn_sample_kernel,
        grid=(nb,),
        in_specs=[pl.BlockSpec((None, N_HEADS, LANES), lambda b: (b, 0, 0)), kv_spec, kv_spec, kv_spec,
                  pl.BlockSpec(mts.shape, lambda b: (0, 0))],
        out_specs=[pl.BlockSpec((None, N_HEADS, LANES), lambda b: (b, 0, 0)),
                   pl.BlockSpec((None, 8, N_SLC_PAD), lambda b: (b, 0, 0))],
        out_shape=[jax.ShapeDtypeStruct((nb, N_HEADS, LANES), f32), jax.ShapeDtypeStruct((nb, 8, N_SLC_PAD), f32)],
        compiler_params=_cparams(("parallel",)),
        name="cmp_attn_sample",
    )(qz, kd, vl, vh, mts)
    idx = pl.pallas_call(
        _topk_sample_kernel,
        out_shape=jax.ShapeDtypeStruct((nb * 8, LANES), jnp.int32),
        name="topk_sample",
    )(imp.reshape(nb * 8, N_SLC_PAD))
    return o16, idx.reshape(nb, 8, LANES)


def _one_query_softmax(s, s_new, vt, v_new):
    m = jnp.maximum(jnp.max(s, axis=1, keepdims=True), s_new)
    p = jnp.exp(s - m)
    p_new = jnp.exp(s_new - m)
    den = jnp.sum(p, axis=1, keepdims=True) + p_new
    num = _nt(p.astype(bf16), vt) + p_new * v_new
    return num * (1.0 / den)


def _selwin_sample_kernel(idx_ref, pt_ref, *refs):
    del pt_ref
    kv = refs[:SEL_TOPN]
    q_ref, ksn_ref, win_ref, kwn_ref, os_ref, ow_ref = refs[SEL_TOPN:]
    b, g = pl.program_id(0), pl.program_id(1)
    q8 = q_ref[...] * ATTN_SCALE
    qb = q8.astype(bf16)
    kcat = jnp.concatenate([r[0] for r in kv], axis=1).astype(bf16)
    vcat = jnp.concatenate([r[1] for r in kv], axis=1).astype(bf16)
    n = SEL_TOPN * PAGE_SIZE
    lane = lax.broadcasted_iota(jnp.int32, (8, n), 1)
    slot = lane // PAGE_SIZE
    blk = jnp.zeros((8, n), jnp.int32)
    for k in range(SEL_TOPN):
        blk = jnp.where(slot == k, idx_ref[(b * N_KV + g) * SEL_TOPN + k], blk)
    live = (blk < N_SLC_S - 1) & ((lane % PAGE_SIZE) // SEL_BLOCK == blk % HALF_PAGE)
    s = jnp.where(live, jnp.dot(qb, kcat, preferred_element_type=f32), NEG)
    k_new, v_new = ksn_ref[pl.ds(g, 1), :], ksn_ref[pl.ds(N_KV + g, 1), :]
    s_new = jnp.sum(q8 * k_new, axis=1, keepdims=True)
    os_ref[...] = _one_query_softmax(s, s_new, vcat, v_new)
    nbuf = win_ref.shape[-1]
    i = lax.broadcasted_iota(jnp.int32, (8, nbuf), 1)
    s = jnp.where(nbuf - i < WINDOW, jnp.dot(qb, win_ref[0].astype(bf16), preferred_element_type=f32), NEG)
    k_new, v_new = kwn_ref[pl.ds(g, 1), :], kwn_ref[pl.ds(N_KV + g, 1), :]
    s_new = jnp.sum(q8 * k_new, axis=1, keepdims=True)
    ow_ref[...] = _one_query_softmax(s, s_new, win_ref[1].astype(bf16), v_new)


def _selwin_sample(idx, page_table, cache_s, q_s, ks_new, win_buf, kw_new):
    nb = q_s.shape[0]
    nbuf = win_buf.shape[1]
    pages = cache_s.transpose(0, 2, 3, 4, 1)
    wb = win_buf.transpose(0, 2, 3, 4, 1)
    qg = q_s.reshape(nb, N_KV, Q_PER_KV, HEAD_DIM)
    q8 = jnp.concatenate([qg, jnp.zeros_like(qg)], axis=2)

    def blk_map(b, g, idx_ref, pt_ref, k):
        j = jnp.minimum(idx_ref[(b * N_KV + g) * SEL_TOPN + k], N_SLC_S - 2)
        return (pt_ref[b * N_PAGES + j // HALF_PAGE], 0, g, 0, 0)

    blk_specs = [pl.BlockSpec((None, 2, None, HEAD_DIM, PAGE_SIZE), functools.partial(blk_map, k=k))
                 for k in range(SEL_TOPN)]
    new_spec = pl.BlockSpec((None, 2 * N_KV, HEAD_DIM), lambda b, g, *_: (b, 0, 0))
    win_spec = pl.BlockSpec((None, 2, None, HEAD_DIM, nbuf), lambda b, g, *_: (b, 0, g, 0, 0))
    o_spec = pl.BlockSpec((None, None, 8, HEAD_DIM), lambda b, g, *_: (b, g, 0, 0))
    grid_spec = pltpu.PrefetchScalarGridSpec(
        num_scalar_prefetch=2,
        grid=(nb, N_KV),
        in_specs=blk_specs + [o_spec, new_spec, win_spec, new_spec],
        out_specs=[o_spec, o_spec])
    o_shape = jax.ShapeDtypeStruct((nb, N_KV, 8, HEAD_DIM), f32)
    ks3, kw3 = ks_new.reshape(nb, 2 * N_KV, HEAD_DIM), kw_new.reshape(nb, 2 * N_KV, HEAD_DIM)
    os8, ow8 = pl.pallas_call(
        _selwin_sample_kernel,
        grid_spec=grid_spec,
        out_shape=[o_shape, o_shape],
        compiler_params=_cparams(("arbitrary", "arbitrary")),
        name="selwin_sample",
    )(idx.reshape(-1), page_table.reshape(-1), *([pages] * SEL_TOPN), q8, ks3, wb, kw3)
    return os8[:, :, :Q_PER_KV].reshape(nb, ATTN_W), ow8[:, :, :Q_PER_KV].reshape(nb, ATTN_W)


def _sample_query_layouts(q):
    nb = q.shape[0]
    q16 = q.reshape(nb, N_HEADS, HEAD_DIM)
    z = jnp.zeros_like(q16)
    lo, hi = jnp.concatenate([q16, z], axis=-1), jnp.concatenate([z, q16], axis=-1)
    h = jnp.arange(N_HEADS)[None, :, None]
    return jnp.where(h % 2 == 0, lo, hi)


def _unpad_heads_by_parity(o16):
    nb = o16.shape[0]
    o = o16.reshape(nb, N_HEADS // 2, 2, 2, HEAD_DIM)
    return jnp.stack([o[:, :, 0, 0], o[:, :, 1, 1]], axis=2).reshape(nb, ATTN_W)


def _finish(x, ocmp, osel, owin, gd, ssd_y, w_out, ln_mlp, w_up, w_down, ln_final, tm, tm_mlp, tf):
    x1 = _outproj(x, ocmp, osel, owin, gd, ssd_y, _gate_expand(), w_out, tm)
    return _mlp(x1, ln_mlp, w_up, w_down, ln_final, tm_mlp, tf)


def kernel(x_prompt, x_sample, cache_kv_cmp, cache_kv_sel, state_kv_win, state_conv, state_ssm, page_table, ln_mix, w_in, cmp_pe, cmp_w1, cmp_b1, cmp_w2, cmp_b2, conv_w, conv_b, dt_bias, a_log, d_skip, ssm_norm, w_out, ln_mlp, w_up, w_down, ln_final):
    nb = x_sample.shape[0]
    w_in_p = _prep_w_in(w_in[0])
    w_out_b, w_up_b, w_down_b = w_out[0].astype(bf16), w_up[0].astype(bf16), w_down[0].astype(bf16)
    lnw, lnm, lnf = ln_mix[0][None], ln_mlp[0][None], ln_final[None]
    cw = _compress_weights(cmp_pe[0], cmp_w1[0], cmp_b1[0], cmp_w2[0], cmp_b2[0])
    mc = _mamba_consts(conv_w[0], conv_b[0], dt_bias[0], a_log[0], d_skip[0], ssm_norm[0])
    kv_shape = (2, N_KV, HEAD_DIM)

    xp = x_prompt.reshape(BATCH * SEQ, D_MODEL)
    q, z, xbc, gd, kct, kst, kwt, kds, kdw = _inproj_prompt(xp, lnw, w_in_p, 256)
    kd, _, _, vt = _compress_prompt(kct, cw)
    attn = _prompt_attention(q, gd, kd, vt, kds, kst, kdw, kwt)
    ssd_y, ssm_p = _mamba_prompt(z, xbc, gd, mc)
    y_prompt = _mlp(_outproj_mixed(xp, attn, ssd_y, w_out_b, 256), lnm, w_up_b, w_down_b, lnf, 512, 1024)

    xs = x_sample.reshape(nb, D_MODEL)
    q_s, z_s, xbc_s, gd_s, kc_s, ks_s, kw_s = _inproj(xs, lnw, w_in_p, nb)
    kd_s, vl_s, vh_s = _compress_sample(cache_kv_cmp[0], page_table, cw)
    ocmp16, idx8 = _cmp_attn_sample(_sample_query_layouts(q_s), kd_s, vl_s, vh_s)
    idx = idx8[:, :N_KV, :SEL_TOPN]
    osel_s, owin_s = _selwin_sample(idx, page_table, cache_kv_sel[0], q_s, ks_s, state_kv_win[0], kw_s)
    ssd_y_s, ssm_s = _mamba_sample(z_s, xbc_s, gd_s, state_conv[0], state_ssm[0], mc)
    y_sample = _finish(xs, _unpad_heads_by_parity(ocmp16), osel_s, owin_s, gd_s, ssd_y_s,
                       w_out_b, lnm, w_up_b, w_down_b, lnf, nb, nb, 1024)

    def token_major(t):
        return t.reshape(BATCH, *kv_shape, t.shape[-1]).transpose(0, 4, 1, 2, 3)[None]

    conv_p = xbc.reshape(BATCH, SEQ, CONV_DIM)[:, -(CONV_W - 1):]
    kv_win_s = jnp.concatenate([state_kv_win[0], kw_s.reshape(nb, 1, *kv_shape)], axis=1)[:, -WINDOW:]
    conv_s = jnp.concatenate([state_conv[0], xbc_s[:, None]], axis=1)[:, -(CONV_W - 1):]
    return (y_prompt.reshape(BATCH, SEQ, D_MODEL), y_sample.reshape(nb, 1, D_MODEL),
            token_major(kct), token_major(kst), token_major(kwt[:, :, -min(WINDOW, SEQ):]),
            conv_p[None], ssm_p[None],
            kc_s.reshape(1, nb, 1, *kv_shape), ks_s.reshape(1, nb, 1, *kv_shape), kv_win_s[None],
            conv_s[None], ssm_s[None])
```

```python
import functools
import math

import jax
import jax.numpy as jnp
import numpy as np
from jax import lax
from jax.experimental import pallas as pl
from jax.experimental.pallas import tpu as pltpu

f32 = jnp.float32
bf16 = jnp.bfloat16

D_MODEL = 2048
BATCH = 2
SEQ = 4096
DEC_BATCH = 32
PAST_LEN = 8192
PAGE_SIZE = 128
HEAD_DIM = 64
N_HEADS = 16
N_KV = 4
Q_PER_KV = 4
ATTN_W = 1024
KV_W = 512
CMP_BLOCK = 32
CMP_STRIDE = 16
CMP_HIDDEN = 128
SEL_BLOCK = 64
SEL_TOPN = 16
WINDOW = 512
SELECT_FORCE = 1.0e4
ATTN_SCALE = HEAD_DIM ** -0.5
D_INNER = 1024
SSM_HEAD_DIM = 64
N_SSM_HEADS = 16
N_SSM_GROUPS = 2
D_STATE = 128
CONV_W = 4
CONV_DIM = D_INNER + 2 * N_SSM_GROUPS * D_STATE
D_FF = 4 * D_MODEL
IN_SPLITS = (ATTN_W, KV_W, KV_W, KV_W, 3 * N_HEADS, D_INNER, CONV_DIM, N_SSM_HEADS)
NORM_EPS = 1e-5

LANES = 128
GD_W = LANES
DT_LANE0 = 3 * N_HEADS
N_PAGES = PAST_LEN // PAGE_SIZE
SEG_PER_PAGE = PAGE_SIZE // CMP_STRIDE
NEG = -1.0e30
SEL_NEG = -1.0e9
VMEM_LIMIT = 56 * 1024 * 1024


def _cparams(sem):
    return pltpu.CompilerParams(dimension_semantics=sem, vmem_limit_bytes=VMEM_LIMIT)


def _nt(a, b):
    return lax.dot_general(a, b, (((1,), (1,)), ((), ())), preferred_element_type=f32)


def _split2(x):
    hi = x.astype(bf16)
    lo = (x - hi.astype(f32)).astype(bf16)
    return hi, lo


def _split3(x):
    hi = x.astype(bf16)
    r = x - hi.astype(f32)
    mid = r.astype(bf16)
    lo = (r - mid.astype(f32)).astype(bf16)
    return hi, mid, lo


def _dot_exact(x, w):
    return sum(jnp.dot(t, w, preferred_element_type=f32) for t in _split3(x))


def _silu(x):
    return x * (1.0 / (1.0 + jnp.exp(-x)))


def _sigmoid(x):
    return 1.0 / (1.0 + jnp.exp(-x))


def _softplus(x):
    return jnp.maximum(x, 0.0) + jnp.log1p(jnp.exp(-jnp.abs(x)))


IN_OUT_WIDTHS = (ATTN_W, D_INNER, CONV_DIM, GD_W, KV_W, KV_W, KV_W)
KD_W = N_KV * LANES
IN_COLS_ROWMAJOR = sum(IN_OUT_WIDTHS)


def _normed(x_ref, lnw_ref):
    x = x_ref[...]
    return (x * lax.rsqrt(jnp.mean(x * x, axis=-1, keepdims=True) + NORM_EPS) * lnw_ref[...]).astype(bf16)


def _inproj_kernel(x_ref, lnw_ref, w_ref, *out_refs):
    h = _normed(x_ref, lnw_ref)
    off = 0
    for ref in out_refs:
        n = ref.shape[-1]
        ref[...] = jnp.dot(h, w_ref[:, off:off + n], preferred_element_type=f32)
        off += n


def _inproj_prompt_kernel(x_ref, lnw_ref, w_ref, q_ref, z_ref, xbc_ref, gd_ref,
                          kct_ref, kst_ref, kwt_ref, kds_ref, kdw_ref, t_ref):
    h = _normed(x_ref, lnw_ref)
    off = 0

    def proj(n):
        nonlocal off
        y = jnp.dot(h, w_ref[:, off:off + n], preferred_element_type=f32)
        off += n
        return y

    for ref in (q_ref, z_ref, xbc_ref, gd_ref):
        ref[...] = proj(ref.shape[-1])
    for t_out in (kct_ref, kst_ref, kwt_ref):
        t_ref[...] = proj(KV_W)
        t_out[...] = t_ref[...].T
    kds_ref[...] = proj(KD_W).astype(bf16)
    kdw_ref[...] = proj(KD_W).astype(bf16)


def _inproj(x, lnw, w_perm, tm):
    m = x.shape[0]
    return pl.pallas_call(
        _inproj_kernel,
        grid=(m // tm,),
        in_specs=[pl.BlockSpec((tm, D_MODEL), lambda i: (i, 0)),
                  pl.BlockSpec((1, D_MODEL), lambda i: (0, 0)),
                  pl.BlockSpec((D_MODEL, IN_COLS_ROWMAJOR), lambda i: (0, 0), pipeline_mode=pl.Buffered(1))],
        out_specs=[pl.BlockSpec((tm, n), lambda i: (i, 0)) for n in IN_OUT_WIDTHS],
        out_shape=[jax.ShapeDtypeStruct((m, n), f32) for n in IN_OUT_WIDTHS],
        compiler_params=_cparams(("parallel",)),
        name="inproj",
    )(x, lnw, w_perm)


def _inproj_prompt(x, lnw, w_perm, tm):
    nq = SEQ // tm
    row = lambda n: pl.BlockSpec((tm, n), lambda i: (i, 0))
    tmin = pl.BlockSpec((None, KV_W, tm), lambda i: (i // nq, 0, i % nq))
    rm = lambda n, dt=f32: jax.ShapeDtypeStruct((BATCH * SEQ, n), dt)
    tshape = jax.ShapeDtypeStruct((BATCH, KV_W, SEQ), f32)
    return pl.pallas_call(
        _inproj_prompt_kernel,
        grid=(BATCH * nq,),
        in_specs=[pl.BlockSpec((tm, D_MODEL), lambda i: (i, 0)),
                  pl.BlockSpec((1, D_MODEL), lambda i: (0, 0)),
                  pl.BlockSpec(w_perm.shape, lambda i: (0, 0), pipeline_mode=pl.Buffered(1))],
        out_specs=[row(ATTN_W), row(D_INNER), row(CONV_DIM), row(GD_W), tmin, tmin, tmin, row(KD_W), row(KD_W)],
        out_shape=[rm(ATTN_W), rm(D_INNER), rm(CONV_DIM), rm(GD_W), tshape, tshape, tshape,
                   rm(KD_W, bf16), rm(KD_W, bf16)],
        scratch_shapes=[pltpu.VMEM((tm, KV_W), f32)],
        compiler_params=_cparams(("parallel",)),
        name="inproj_prompt",
    )(x, lnw, w_perm)


def _prep_w_in(w_in):
    parts, off = [], 0
    for width in IN_SPLITS:
        parts.append(w_in[:, off:off + width])
        off += width
    q, kc, ks, kw, g, z, xbc, dt = parts
    gd = jnp.concatenate([g, dt, jnp.zeros((D_MODEL, GD_W - 4 * N_HEADS), w_in.dtype)], axis=1)

    def dup_keys(kv):
        k = kv[:, :KV_W // 2].reshape(D_MODEL, N_KV, 1, HEAD_DIM)
        return jnp.broadcast_to(k, (D_MODEL, N_KV, 2, HEAD_DIM)).reshape(D_MODEL, KD_W)

    return jnp.concatenate([q, z, xbc, gd, kc, ks, kw, dup_keys(ks), dup_keys(kw)], axis=1).astype(bf16)


def _outproj_kernel(x_ref, oc_ref, os_ref, ow_ref, gd_ref, y_ref, eg_ref, w_ref, o_ref):
    gates = _sigmoid(gd_ref[...])
    hi, lo = _split2(gates)
    attn = None
    for c, br in enumerate((oc_ref, os_ref, ow_ref)):
        ge = (jnp.dot(hi, eg_ref[c], preferred_element_type=f32)
              + jnp.dot(lo, eg_ref[c], preferred_element_type=f32))
        term = ge * br[...]
        attn = term if attn is None else attn + term
    mix = (jnp.dot(attn.astype(bf16), w_ref[:ATTN_W, :], preferred_element_type=f32)
           + jnp.dot(y_ref[...], w_ref[ATTN_W:, :], preferred_element_type=f32))
    o_ref[...] = x_ref[...] + mix


def _outproj_mixed_kernel(x_ref, a_ref, y_ref, w_ref, o_ref):
    mix = (jnp.dot(a_ref[...], w_ref[:ATTN_W, :], preferred_element_type=f32)
           + jnp.dot(y_ref[...], w_ref[ATTN_W:, :], preferred_element_type=f32))
    o_ref[...] = x_ref[...] + mix


def _outproj_mixed(x, attn, ssd_y, w_out, tm):
    m = x.shape[0]
    row = lambda w: pl.BlockSpec((tm, w), lambda i: (i, 0))
    return pl.pallas_call(
        _outproj_mixed_kernel,
        grid=(m // tm,),
        in_specs=[row(D_MODEL), row(ATTN_W), row(D_INNER),
                  pl.BlockSpec((ATTN_W + D_INNER, D_MODEL), lambda i: (0, 0))],
        out_specs=row(D_MODEL),
        out_shape=jax.ShapeDtypeStruct((m, D_MODEL), f32),
        compiler_params=_cparams(("parallel",)),
        name="outproj_mixed",
    )(x, attn, ssd_y, w_out)


def _outproj(x, ocmp, osel, owin, gd, ssd_y, eg, w_out, tm):
    m = x.shape[0]
    row = lambda w: pl.BlockSpec((tm, w), lambda i: (i, 0))
    return pl.pallas_call(
        _outproj_kernel,
        grid=(m // tm,),
        in_specs=[row(D_MODEL), row(ATTN_W), row(ATTN_W), row(ATTN_W), row(GD_W), row(D_INNER),
                  pl.BlockSpec((3, GD_W, ATTN_W), lambda i: (0, 0, 0)),
                  pl.BlockSpec((ATTN_W + D_INNER, D_MODEL), lambda i: (0, 0))],
        out_specs=row(D_MODEL),
        out_shape=jax.ShapeDtypeStruct((m, D_MODEL), f32),
        compiler_params=_cparams(("parallel",)),
        name="outproj",
    )(x, ocmp, osel, owin, gd, ssd_y, eg, w_out)


def _gate_expand():
    eg = np.zeros((3, GD_W, ATTN_W), np.float32)
    for h in range(N_HEADS):
        for c in range(3):
            eg[c, h * 3 + c, h * HEAD_DIM:(h + 1) * HEAD_DIM] = 1.0
    return jnp.asarray(eg, bf16)


def _mlp_kernel(x_ref, ln_ref, wu_ref, wd_ref, lnf_ref, o_ref, h_ref, acc_ref):
    k = pl.program_id(1)

    @pl.when(k == 0)
    def _():
        x = x_ref[...]
        h = x * lax.rsqrt(jnp.mean(x * x, axis=-1, keepdims=True) + NORM_EPS) * ln_ref[...]
        h_ref[...] = h.astype(bf16)
        acc_ref[...] = jnp.zeros_like(acc_ref)

    u = jnp.dot(h_ref[...], wu_ref[...], preferred_element_type=f32)
    u = jnp.square(jnp.maximum(u, 0.0)).astype(bf16)
    acc_ref[...] += jnp.dot(u, wd_ref[...], preferred_element_type=f32)

    @pl.when(k == pl.num_programs(1) - 1)
    def _():
        y = x_ref[...] + acc_ref[...]
        y = y * lax.rsqrt(jnp.mean(y * y, axis=-1, keepdims=True) + NORM_EPS) * lnf_ref[...]
        o_ref[...] = y


def _mlp(x, ln_mlp, w_up, w_down, ln_final, tm, tf):
    m = x.shape[0]
    return pl.pallas_call(
        _mlp_kernel,
        grid=(m // tm, D_FF // tf),
        in_specs=[pl.BlockSpec((tm, D_MODEL), lambda i, k: (i, 0)),
                  pl.BlockSpec((1, D_MODEL), lambda i, k: (0, 0)),
                  pl.BlockSpec((D_MODEL, tf), lambda i, k: (0, k)),
                  pl.BlockSpec((tf, D_MODEL), lambda i, k: (k, 0)),
                  pl.BlockSpec((1, D_MODEL), lambda i, k: (0, 0))],
        out_specs=pl.BlockSpec((tm, D_MODEL), lambda i, k: (i, 0)),
        out_shape=jax.ShapeDtypeStruct((m, D_MODEL), f32),
        scratch_shapes=[pltpu.VMEM((tm, D_MODEL), bf16), pltpu.VMEM((tm, D_MODEL), f32)],
        compiler_params=_cparams(("parallel", "arbitrary")),
        name="mlp",
    )(x, ln_mlp, w_up, w_down, ln_final)


N_SLAB_K = CMP_STRIDE * LANES


def _compress_weights(cmp_pe, cmp_w1, cmp_b1, cmp_w2, cmp_b2):
    eye2 = jnp.eye(2, dtype=f32)
    w1 = cmp_w1.reshape(2, 2, CMP_STRIDE, HEAD_DIM, CMP_HIDDEN)
    w1cat = jnp.einsum('crjdf,ab->cjadrbf', w1, eye2).reshape(2, N_SLAB_K, 4 * CMP_HIDDEN).astype(bf16)
    w1flat = cmp_w1.reshape(2, CMP_BLOCK * HEAD_DIM, CMP_HIDDEN).astype(bf16)
    pe8 = jnp.broadcast_to(cmp_pe.reshape(2, 1, CMP_BLOCK * HEAD_DIM), (2, 8, CMP_BLOCK * HEAD_DIM))
    b1 = cmp_b1.reshape(2, 1, CMP_HIDDEN)
    wk, wv = cmp_w2[0], cmp_w2[1]
    zk = jnp.zeros_like(wk)
    w2k = jnp.einsum('fd,ab,e->afbed', wk, eye2, jnp.ones((2,), f32)).reshape(2 * CMP_HIDDEN, 4 * HEAD_DIM)
    b2k = jnp.tile(cmp_b2[0], 4)[None]
    vl = jnp.einsum('fd,ab,e->afbed', wv, eye2, jnp.array([1.0, 0.0], f32)).reshape(2 * CMP_HIDDEN, 4 * HEAD_DIM)
    vh = jnp.einsum('fd,ab,e->afbed', wv, eye2, jnp.array([0.0, 1.0], f32)).reshape(2 * CMP_HIDDEN, 4 * HEAD_DIM)
    w2v = jnp.concatenate([vl, vh], axis=1)
    zb = jnp.zeros((HEAD_DIM,), f32)
    b2v = jnp.concatenate([cmp_b2[1], zb, cmp_b2[1], zb, zb, cmp_b2[1], zb, cmp_b2[1]])[None]
    del zk
    return pe8, w1flat, b1, w1cat, w2k.astype(bf16), b2k, w2v.astype(bf16), b2v


def _compress_slab(a, c, gp, prm_refs, out_refs):
    pe_ref, w1f_ref, b1_ref, w1_ref, w2k_ref, b2k_ref, w2v_ref, b2v_ref = prm_refs
    kd_ref, vl_ref, vh_ref = out_refs
    pet = jnp.dot(pe_ref[c].astype(bf16), w1f_ref[c], preferred_element_type=f32)[0:1] + b1_ref[c]
    bias = jnp.concatenate([pet, pet], axis=1)
    p = jnp.dot(a, w1_ref[c], preferred_element_type=f32)
    s = p.shape[0]
    hid = p[:, :2 * CMP_HIDDEN] + pltpu.roll(p[:, 2 * CMP_HIDDEN:], s - 1, axis=0)
    hid = _silu(hid + bias).astype(bf16)
    w = 4 * HEAD_DIM
    if c == 0:
        kd_ref[:, gp * w:(gp + 1) * w] = (jnp.dot(hid, w2k_ref[...], preferred_element_type=f32)
                                          + b2k_ref[...]).astype(bf16)
    else:
        o = jnp.dot(hid, w2v_ref[...], preferred_element_type=f32) + b2v_ref[...]
        vl_ref[:, gp * w:(gp + 1) * w] = o[:, :w].astype(bf16)
        vh_ref[:, gp * w:(gp + 1) * w] = o[:, w:].astype(bf16)


def _compress_token_minor(pair, n_pairs, perm_ref, prm_refs, out_refs, o_ref):
    rows = 2 * SEG_PER_PAGE
    for k in range(n_pairs):
        o_ref[k] = _nt(perm_ref[...], pair(k).astype(bf16)).astype(bf16)
    nseg = n_pairs * rows
    for c in range(2):
        for gp in range(2):
            l0 = c * (KV_W // 2) + gp * LANES
            a = jnp.concatenate([o_ref[:, j * rows:(j + 1) * rows, l0:l0 + LANES].reshape(nseg, LANES)
                                 for j in range(CMP_STRIDE)], axis=1)
            _compress_slab(a, c, gp, prm_refs, out_refs)


def _compress_prompt_kernel(x_ref, perm_ref, *refs):
    chunk = 2 * PAGE_SIZE
    _compress_token_minor(lambda k: x_ref[:, k * chunk:(k + 1) * chunk], SEQ // chunk, perm_ref,
                          refs[:8], refs[8:11], refs[12])
    vl_ref, vt_ref = refs[9], refs[11]
    for g in range(N_KV):
        v_pad = vl_ref[:, g * LANES:(g + 1) * LANES].astype(f32)
        vt_ref[g * HEAD_DIM:(g + 1) * HEAD_DIM, :] = v_pad.T[:HEAD_DIM].astype(bf16)


def _seg_perm():
    m = np.zeros((2 * PAGE_SIZE, 2 * PAGE_SIZE), np.float32)
    for pp in range(2):
        for s in range(SEG_PER_PAGE):
            for j in range(CMP_STRIDE):
                m[j * 2 * SEG_PER_PAGE + pp * SEG_PER_PAGE + s, pp * PAGE_SIZE + s * CMP_STRIDE + j] = 1.0
    return jnp.asarray(m, bf16)


def _compress_sample_kernel(pt_ref, *refs):
    del pt_ref
    pages, perm_ref = refs[:N_PAGES], refs[N_PAGES]
    prm_refs, out_refs, o_ref = refs[N_PAGES + 1:N_PAGES + 9], refs[N_PAGES + 9:N_PAGES + 12], refs[-1]
    _compress_token_minor(lambda k: jnp.concatenate([pages[2 * k][...], pages[2 * k + 1][...]], axis=1),
                          N_PAGES // 2, perm_ref, prm_refs, out_refs, o_ref)


def _const_spec(a, n_grid, n_prefetch=0):
    zeros = (0,) * a.ndim
    if n_grid == 1:
        return pl.BlockSpec(a.shape, (lambda b, *_: zeros))
    return pl.BlockSpec(a.shape, (lambda b, c, *_: zeros))


def _compress_prompt(kc_t, cw):
    nseg = SEQ // CMP_STRIDE
    perm = _seg_perm()
    out = jax.ShapeDtypeStruct((BATCH, nseg, KV_W), bf16)
    vt_rows = N_KV * HEAD_DIM
    return pl.pallas_call(
        _compress_prompt_kernel,
        grid=(BATCH,),
        in_specs=[pl.BlockSpec((None, KV_W, SEQ), lambda b: (b, 0, 0)), _const_spec(perm, 1)]
                 + [_const_spec(a, 1) for a in cw],
        out_specs=[pl.BlockSpec((None, nseg, KV_W), lambda b: (b, 0, 0))] * 3
                  + [pl.BlockSpec((None, vt_rows, nseg), lambda b: (b, 0, 0))],
        out_shape=[out] * 3 + [jax.ShapeDtypeStruct((BATCH, vt_rows, nseg), bf16)],
        scratch_shapes=[pltpu.VMEM((SEQ // (2 * PAGE_SIZE), 2 * PAGE_SIZE, KV_W), bf16)],
        compiler_params=_cparams(("parallel",)),
        name="compress_prompt",
    )(kc_t, perm, *cw)


def _compress_sample(cache, page_table, cw):
    nseg = PAST_LEN // CMP_STRIDE
    x = _token_minor(cache)
    perm = _seg_perm()
    out = jax.ShapeDtypeStruct((DEC_BATCH, nseg, KV_W), bf16)
    page_specs = [pl.BlockSpec((None, KV_W, PAGE_SIZE), functools.partial(lambda b, pt, p: (pt[b * N_PAGES + p], 0, 0), p=p))
                  for p in range(N_PAGES)]
    grid_spec = pltpu.PrefetchScalarGridSpec(
        num_scalar_prefetch=1,
        grid=(DEC_BATCH,),
        in_specs=page_specs + [_const_spec(perm, 1)] + [_const_spec(a, 1) for a in cw],
        out_specs=[pl.BlockSpec((None, nseg, KV_W), lambda b, pt: (b, 0, 0))] * 3,
        scratch_shapes=[pltpu.VMEM((N_PAGES // 2, 2 * PAGE_SIZE, KV_W), bf16)])
    return pl.pallas_call(
        _compress_sample_kernel,
        grid_spec=grid_spec,
        out_shape=[out] * 3,
        compiler_params=_cparams(("arbitrary",)),
        name="compress_sample",
    )(page_table.reshape(-1), *([x] * N_PAGES), perm, *cw)


def _token_minor(kv):
    n, t = kv.shape[0], kv.shape[1]
    return kv.transpose(0, 2, 3, 4, 1).reshape(n, KV_W, t)


TQ = 256
N_SEL = SEQ // SEL_BLOCK


def _half_mask(shape, hi):
    lane = lax.broadcasted_iota(jnp.int32, shape, 1)
    return (lane >= HEAD_DIM) if hi else (lane < HEAD_DIM)


def _cmp_branch(i, qs_ref, kd_ref, vt_ref, mt_ref):
    t0 = i * TQ
    ncmp = kd_ref.shape[0]
    nidx = lax.broadcasted_iota(jnp.int32, (ncmp, Q_PER_KV * TQ), 0)
    tpos = t0 + (lax.broadcasted_iota(jnp.int32, (ncmp, Q_PER_KV * TQ), 1) & (TQ - 1))
    mask = nidx * CMP_STRIDE + (CMP_BLOCK - 1) <= tpos
    s = jnp.where(mask, _nt(kd_ref[...], qs_ref[:, 0:LANES]), NEG)
    m = jnp.max(s, axis=0, keepdims=True)
    ex = jnp.where(mask, jnp.exp2(s - m), 0.0)
    p = ex * (1.0 / jnp.maximum(jnp.sum(ex, axis=0, keepdims=True), 1e-30))
    pb = p.astype(bf16)
    vt = vt_ref[...]
    heads = [jnp.dot(vt, pb[:, r * TQ:(r + 1) * TQ], preferred_element_type=f32) for r in range(Q_PER_KV)]
    heads = jnp.concatenate(heads, axis=0)
    psum = sum(p[:, r * TQ:(r + 1) * TQ] for r in range(Q_PER_KV))
    hi, lo = _split2(psum)
    imp = (jnp.dot(mt_ref[...], hi, preferred_element_type=f32)
           + jnp.dot(mt_ref[...], lo, preferred_element_type=f32))
    imp = imp[:N_SEL]
    j = lax.broadcasted_iota(jnp.int32, (N_SEL, TQ), 0)
    jt = (t0 + lax.broadcasted_iota(jnp.int32, (N_SEL, TQ), 1)) // SEL_BLOCK
    imp = jnp.where((j == 0) | (j == jt) | (j == jt - 1), SELECT_FORCE, imp)
    imp = jnp.where(j > jt, -SELECT_FORCE, imp)
    cnt = jnp.zeros((N_SEL, TQ), f32)
    for jp in range(N_SEL):
        row = imp[jp:jp + 1, :]
        cnt = cnt + jnp.where(j > jp, jnp.where(row >= imp, 1.0, 0.0), jnp.where(row > imp, 1.0, 0.0))
    selm = jnp.where((cnt < SEL_TOPN) & (j <= jt), 0.0, SEL_NEG)
    return heads, jnp.concatenate([selm, selm], axis=0).T


def _cmp_to_sel_matrix_t(n_cmp_pad, n_slc, n_cmp):
    ratio = SEL_BLOCK // CMP_STRIDE
    i = np.arange(n_cmp_pad)[None, :]
    jj = np.arange(n_slc)[:, None]
    diff = i - ratio * jj
    mat = np.zeros((n_slc, n_cmp_pad), np.float32)
    for n in range(CMP_BLOCK // CMP_STRIDE):
        mat += ((diff + n >= 0) & (diff + n < ratio)).astype(np.float32)
    mat[:, n_cmp:] = 0.0
    return mat


VT_ROWS = HEAD_DIM + 16
SEL_KT = 4

def _flash_tile(qs, k, vt, mask, m, acc):
    s = _nt(k, qs)
    if mask is not None:
        s = jnp.where(mask, s, NEG)
    m_next = jnp.maximum(m, jnp.max(s, axis=0, keepdims=True))
    alpha = jnp.exp2(m - m_next)
    p = jnp.exp2(s - m_next).astype(bf16)
    pv = jnp.concatenate([jnp.dot(vt, p[:, r * TQ:(r + 1) * TQ], preferred_element_type=f32)
                          for r in range(Q_PER_KV)], axis=1)
    return m_next, alpha * acc + pv


def _flash_finish(acc):
    o_t = acc[:HEAD_DIM] * (1.0 / acc[HEAD_DIM:HEAD_DIM + 1])
    return jnp.concatenate([o_t[:, r * TQ:(r + 1) * TQ] for r in range(Q_PER_KV)], axis=0)


LOG2E = math.log2(math.e)


def _stack_queries(q_ref, qs_ref):
    for r in range(Q_PER_KV):
        q2 = q_ref[:, (r // 2) * LANES:(r // 2 + 1) * LANES] * (ATTN_SCALE * LOG2E)
        qs_ref[r * TQ:(r + 1) * TQ, 0:LANES] = jnp.where(_half_mask((TQ, LANES), r % 2), q2, 0.0).astype(bf16)


def _flash_init():
    return jnp.full((1, Q_PER_KV * TQ), NEG, f32), jnp.zeros((VT_ROWS, Q_PER_KV * TQ), f32)


def _sel_branch(i, selm, kd_ref, oh_ref, vt_ref, qs_ref, m_ref, acc_ref):
    selm = selm.astype(bf16)
    for r in range(Q_PER_KV):
        qs_ref[r * TQ:(r + 1) * TQ, LANES:2 * LANES] = selm
    qs = qs_ref[...]

    def span(first, n, mask, m, acc):
        rows = pl.ds(pl.multiple_of(first * TQ, TQ), n * TQ)
        k = jnp.concatenate([kd_ref[rows, :], oh_ref[rows, :]], axis=1)
        return _flash_tile(qs, k, _with_ones(vt_ref[:, rows]), mask, m, acc)

    full = i // SEL_KT
    m, acc = lax.fori_loop(0, full, lambda t, carry: span(SEL_KT * t, SEL_KT, None, *carry), _flash_init())
    m_ref[...] = m
    acc_ref[...] = acc
    for n in range(1, SEL_KT + 1):
        @pl.when(i % SEL_KT == n - 1)
        def _(n=n):
            key = lax.broadcasted_iota(jnp.int32, (n * TQ, Q_PER_KV * TQ), 0)
            qry = (lax.broadcasted_iota(jnp.int32, (n * TQ, Q_PER_KV * TQ), 1) & (TQ - 1)) + (n - 1) * TQ
            _, acc_n = span(full * SEL_KT, n, key <= qry, m_ref[...], acc_ref[...])
            acc_ref[...] = acc_n

    return _flash_finish(acc_ref[...])


def _win_branch(i, kd_ref, vt_ref, qs_ref, acc_ref):
    qs = qs_ref[:, 0:LANES]

    def span(first, n):
        rows = pl.ds(pl.multiple_of(first * TQ, TQ), n * TQ)
        key = lax.broadcasted_iota(jnp.int32, (n * TQ, Q_PER_KV * TQ), 0)
        qry = (lax.broadcasted_iota(jnp.int32, (n * TQ, Q_PER_KV * TQ), 1) & (TQ - 1)) + (n - 1) * TQ
        dist = qry - key
        _, acc = _flash_tile(qs, kd_ref[rows, :], _with_ones(vt_ref[:, rows]), (dist >= 0) & (dist < WINDOW),
                             *_flash_init())
        acc_ref[...] = acc

    n_back = WINDOW // TQ

    @pl.when(i >= n_back)
    def _():
        span(i - n_back, n_back + 1)

    for early in range(n_back):
        @pl.when(i == early)
        def _(early=early):
            span(0, early + 1)

    return _flash_finish(acc_ref[...])


def _with_ones(vt):
    return jnp.concatenate([vt.astype(bf16), jnp.ones((VT_ROWS - HEAD_DIM, vt.shape[1]), bf16)], axis=0)


def _prompt_attn_kernel(q_ref, gd_ref, kdc_ref, vtc_ref, mt_ref, kds_ref, oh_ref, vts_ref, kdw_ref, vtw_ref,
                        o_ref, qs_ref, m_ref, acc_ref, gt_ref):
    g, i = pl.program_id(1), pl.program_id(2)
    _stack_queries(q_ref, qs_ref)
    gt_ref[...] = _sigmoid(gd_ref[...]).T

    def gated(heads, branch):
        rows = [gt_ref[pl.ds((g * Q_PER_KV + r) * 3 + branch, 1), :] for r in range(Q_PER_KV)]
        return jnp.concatenate([heads[r * HEAD_DIM:(r + 1) * HEAD_DIM] * rows[r] for r in range(Q_PER_KV)], axis=0)

    heads, selm = _cmp_branch(i, qs_ref, kdc_ref, vtc_ref, mt_ref)
    total = gated(heads, 0)
    total = total + gated(_sel_branch(i, selm, kds_ref, oh_ref, vts_ref, qs_ref, m_ref, acc_ref), 1)
    total = total + gated(_win_branch(i, kdw_ref, vtw_ref, qs_ref, acc_ref), 2)
    o_ref[...] = total.T.astype(bf16)


def _prompt_attention(q, gd, kd_c, vt_c, kd_s, kv_s_t, kd_w, kv_w_t):
    assert WINDOW % TQ == 0
    ncmp = SEQ // CMP_STRIDE
    mt = np.zeros((LANES, ncmp), np.float32)
    mt[:N_SEL] = _cmp_to_sel_matrix_t(ncmp, N_SEL, ncmp - 1)
    mt = jnp.asarray(mt, bf16)
    blk = np.arange(SEQ)[:, None] // SEL_BLOCK == np.arange(LANES)[None, :]
    oh = jnp.asarray(blk.astype(np.float32), bf16)
    nq = SEQ // TQ
    q_spec = pl.BlockSpec((TQ, 2 * LANES), lambda b, g, i: (b * nq + i, g))
    cmp_spec = pl.BlockSpec((None, ncmp, LANES), lambda b, g, i: (b, 0, g))
    cmp_vt_spec = pl.BlockSpec((None, HEAD_DIM, ncmp), lambda b, g, i: (b, g, 0))
    kd_spec = pl.BlockSpec((SEQ, LANES), lambda b, g, i: (b, g))
    vt_spec = pl.BlockSpec((None, HEAD_DIM, SEQ), lambda b, g, i: (b, N_KV + g, 0))
    const = lambda a: pl.BlockSpec(a.shape, lambda b, g, i: (0, 0))
    gd_spec = pl.BlockSpec((TQ, GD_W), lambda b, g, i: (b * nq + i, 0))
    return pl.pallas_call(
        _prompt_attn_kernel,
        grid=(BATCH, N_KV, nq),
        in_specs=[q_spec, gd_spec, cmp_spec, cmp_vt_spec, const(mt), kd_spec, const(oh), vt_spec, kd_spec, vt_spec],
        out_specs=q_spec,
        out_shape=jax.ShapeDtypeStruct((BATCH * SEQ, ATTN_W), bf16),
        scratch_shapes=[pltpu.VMEM((Q_PER_KV * TQ, 2 * LANES), bf16), pltpu.VMEM((1, Q_PER_KV * TQ), f32),
                        pltpu.VMEM((VT_ROWS, Q_PER_KV * TQ), f32), pltpu.VMEM((GD_W, TQ), f32)],
        compiler_params=_cparams(("parallel", "parallel", "parallel")),
        name="prompt_attention",
    )(q, gd, kd_c, vt_c, mt, kd_s, oh, kv_s_t, kd_w, kv_w_t)


SSD_Q = 256
HALF_INNER = D_INNER // N_SSM_GROUPS
BC_W = N_SSM_GROUPS * D_STATE


def _mamba_consts(conv_w, conv_b, dt_bias, a_log, d_skip, ssm_norm):
    pad = lambda v: jnp.zeros((1, GD_W), f32).at[0, DT_LANE0:DT_LANE0 + N_SSM_HEADS].set(v)
    e16 = np.zeros((GD_W, D_INNER), np.float32)
    for h in range(N_SSM_HEADS):
        e16[DT_LANE0 + h, h * SSM_HEAD_DIM:(h + 1) * SSM_HEAD_DIM] = 1.0
    tri = np.tril(np.ones((SSD_Q, SSD_Q), np.float32))
    return (conv_w, conv_b[None], pad(dt_bias), pad(a_log), jnp.asarray(e16, bf16),
            jnp.repeat(d_skip, SSM_HEAD_DIM)[None], ssm_norm[None], jnp.asarray(tri, bf16))


def _dt_and_decay(gd, dtb_ref, alog_ref):
    lane = lax.broadcasted_iota(jnp.int32, gd.shape, 1)
    live = (lane >= DT_LANE0) & (lane < DT_LANE0 + N_SSM_HEADS)
    dt = jnp.where(live, _softplus(gd + dtb_ref[...]), 0.0)
    return dt, dt * (-jnp.exp(alog_ref[...]))


def _gated_group_norm(y, z, nw_ref):
    y = y * _silu(z)
    outs = []
    for g in range(N_SSM_GROUPS):
        yg = y[:, g * HALF_INNER:(g + 1) * HALF_INNER]
        outs.append(yg * lax.rsqrt(jnp.mean(yg * yg, axis=-1, keepdims=True) + NORM_EPS))
    return jnp.concatenate(outs, axis=1) * nw_ref[...]


def _mamba_prompt_kernel(z_ref, xbc_ref, gd_ref, cw_ref, cb_ref, dtb_ref, alog_ref, e16_ref, dsk_ref, nw_ref, tri_ref,
                         y_ref, st_ref, xpad_ref, state_ref):
    c = pl.program_id(1)
    nq = SSD_Q

    @pl.when(c == 0)
    def _():
        state_ref[...] = jnp.zeros(state_ref.shape, f32)
        xpad_ref[0:8, :] = jnp.zeros((8, CONV_DIM), f32)

    xpad_ref[8:8 + nq, :] = xbc_ref[...]
    conv = cb_ref[...]
    for w in range(CONV_W):
        conv = conv + xpad_ref[8 - (CONV_W - 1) + w:8 - (CONV_W - 1) + w + nq, :] * cw_ref[w:w + 1, :]
    xpad_ref[0:8, :] = xpad_ref[nq:nq + 8, :]
    act = _silu(conv)
    xs, bm, cm = act[:, :D_INNER], act[:, D_INNER:D_INNER + BC_W], act[:, D_INNER + BC_W:]

    dt, a = _dt_and_decay(gd_ref[...], dtb_ref, alog_ref)
    a_cs = sum(jnp.dot(tri_ref[...], t, preferred_element_type=f32) for t in _split3(a))
    ea = jnp.exp(a_cs)
    te = jnp.exp(a_cs[nq - 1:nq, :] - a_cs)
    e16 = e16_ref[...]
    dt_x, ea_x, te_x = _dot_exact(dt, e16), _dot_exact(ea, e16), _dot_exact(te, e16)
    xdt = xs * dt_x
    xw = (xdt * te_x).astype(bf16)
    a_cst = a_cs.T
    tril = lax.broadcasted_iota(jnp.int32, (nq, nq), 1) <= lax.broadcasted_iota(jnp.int32, (nq, nq), 0)

    ys = []
    for g in range(N_SSM_GROUPS):
        bg = bm[:, g * D_STATE:(g + 1) * D_STATE]
        cgb = cm[:, g * D_STATE:(g + 1) * D_STATE].astype(bf16)
        cb = _nt(cgb, bg.astype(bf16))
        st = state_ref[g]
        yoff = jnp.dot(cgb, st.astype(bf16), preferred_element_type=f32)
        gl = slice(g * HALF_INNER, (g + 1) * HALF_INNER)
        state_ref[g] = ea_x[nq - 1:nq, gl] * st + jnp.dot(bg.T.astype(bf16), xw[:, gl], preferred_element_type=f32)
        for hp in range(HALF_INNER // LANES):
            pl_ = slice(g * HALF_INNER + hp * LANES, g * HALF_INNER + (hp + 1) * LANES)
            xp = xdt[:, pl_]
            yp = ea_x[:, pl_] * yoff[:, hp * LANES:(hp + 1) * LANES] + dsk_ref[:, pl_] * xs[:, pl_]
            for e in range(2):
                lane_h = DT_LANE0 + g * (N_SSM_HEADS // N_SSM_GROUPS) + hp * 2 + e
                seg = a_cs[:, lane_h:lane_h + 1] - a_cst[lane_h:lane_h + 1, :]
                wgt = (cb * jnp.exp(jnp.where(tril, seg, NEG))).astype(bf16)
                xz = jnp.where(_half_mask((nq, LANES), e), xp, 0.0).astype(bf16)
                yp = yp + jnp.dot(wgt, xz, preferred_element_type=f32)
            ys.append(yp)
    y = jnp.concatenate(ys, axis=1)
    y_ref[...] = _gated_group_norm(y, z_ref[...], nw_ref).astype(bf16)

    @pl.when(c == pl.num_programs(1) - 1)
    def _():
        st_ref[...] = state_ref[...]


def _mamba_prompt(z, xbc, gd, mc):
    nc = SEQ // SSD_Q
    row = lambda w: pl.BlockSpec((SSD_Q, w), lambda b, c: (b * nc + c, 0))
    y, st = pl.pallas_call(
        _mamba_prompt_kernel,
        grid=(BATCH, nc),
        in_specs=[row(D_INNER), row(CONV_DIM), row(GD_W)] + [_const_spec(a, 2) for a in mc],
        out_specs=[row(D_INNER), pl.BlockSpec((None, N_SSM_GROUPS, D_STATE, HALF_INNER), lambda b, c: (b, 0, 0, 0))],
        out_shape=[jax.ShapeDtypeStruct((BATCH * SEQ, D_INNER), bf16),
                   jax.ShapeDtypeStruct((BATCH, N_SSM_GROUPS, D_STATE, HALF_INNER), f32)],
        scratch_shapes=[pltpu.VMEM((SSD_Q + 8, CONV_DIM), f32), pltpu.VMEM((N_SSM_GROUPS, D_STATE, HALF_INNER), f32)],
        compiler_params=_cparams(("parallel", "arbitrary")),
        name="mamba_prompt",
    )(z, xbc, gd, *mc)
    hpg = N_SSM_HEADS // N_SSM_GROUPS
    st = st.reshape(BATCH, N_SSM_GROUPS, D_STATE, hpg, SSM_HEAD_DIM).transpose(0, 1, 3, 4, 2)
    return y, st.reshape(BATCH, N_SSM_HEADS, SSM_HEAD_DIM, D_STATE)


def _mamba_sample_kernel(z_ref, xbc_ref, gd_ref, sc_ref, s_ref, cw_ref, cb_ref, dtb_ref, alog_ref, e16_ref, dsk_ref,
                         nw_ref, y_ref, so_ref):
    conv = cb_ref[...] + xbc_ref[...] * cw_ref[CONV_W - 1:CONV_W, :]
    for w in range(CONV_W - 1):
        conv = conv + sc_ref[w:w + 1, :] * cw_ref[w:w + 1, :]
    act = _silu(conv)
    xs, bm, cm = act[:, :D_INNER], act[:, D_INNER:D_INNER + BC_W], act[:, D_INNER + BC_W:]
    dt, a = _dt_and_decay(jnp.broadcast_to(gd_ref[...], (8, GD_W)), dtb_ref, alog_ref)
    e16 = e16_ref[...]
    dt_x = _dot_exact(dt, e16)[0:1]
    da_x = _dot_exact(jnp.exp(a), e16)[0:1]
    rows = N_SSM_HEADS * SSM_HEAD_DIM
    xcol = jnp.broadcast_to(xs * dt_x, (D_STATE, rows)).T
    acol = jnp.broadcast_to(da_x, (D_STATE, rows)).T
    rowi = lax.broadcasted_iota(jnp.int32, (rows, D_STATE), 0)
    bfull = jnp.where(rowi < HALF_INNER, bm[:, :D_STATE], bm[:, D_STATE:])
    snew = s_ref[...] * acol + xcol * bfull
    so_ref[...] = snew
    r8 = lax.broadcasted_iota(jnp.int32, (8, D_STATE), 0)
    c8 = jnp.where(r8 == 0, cm[:, :D_STATE], jnp.where(r8 == 1, cm[:, D_STATE:], 0.0)).astype(bf16)
    yy = _nt(c8, snew.astype(bf16))
    lane = lax.broadcasted_iota(jnp.int32, (1, rows), 1)
    y = jnp.where(lane < HALF_INNER, yy[0:1], yy[1:2]) + dsk_ref[...] * xs
    y_ref[...] = _gated_group_norm(y, z_ref[...], nw_ref).astype(bf16)


def _mamba_sample(z, xbc, gd, state_conv, state_ssm, mc):
    nb = z.shape[0]
    mc = mc[:7]
    rows = N_SSM_HEADS * SSM_HEAD_DIM
    one = lambda w: pl.BlockSpec((None, 1, w), lambda b: (b, 0, 0))
    s_spec = pl.BlockSpec((None, rows, D_STATE), lambda b: (b, 0, 0))
    y, so = pl.pallas_call(
        _mamba_sample_kernel,
        grid=(nb,),
        in_specs=[one(D_INNER), one(CONV_DIM), one(GD_W),
                  pl.BlockSpec((None, CONV_W - 1, CONV_DIM), lambda b: (b, 0, 0)), s_spec]
                 + [_const_spec(a, 1) for a in mc],
        out_specs=[one(D_INNER), s_spec],
        out_shape=[jax.ShapeDtypeStruct((nb, 1, D_INNER), bf16), jax.ShapeDtypeStruct((nb, rows, D_STATE), f32)],
        compiler_params=_cparams(("parallel",)),
        name="mamba_sample",
    )(z[:, None], xbc[:, None], gd[:, None], state_conv, state_ssm.reshape(nb, rows, D_STATE), *mc)
    return y[:, 0], so.reshape(nb, N_SSM_HEADS, SSM_HEAD_DIM, D_STATE)


N_CMP_S = PAST_LEN // CMP_STRIDE - 1
N_SLC_S = PAST_LEN // SEL_BLOCK + 1
N_SLC_PAD = 2 * LANES
HALF_PAGE = PAGE_SIZE // SEL_BLOCK
GONE = -3.0e38


def _cmp_attn_sample_kernel(qz_ref, kd_ref, vl_ref, vh_ref, mts_ref, o_ref, imp_ref):
    qz = (qz_ref[...] * ATTN_SCALE).astype(bf16)
    ncmp = kd_ref.shape[0]
    nidx = lax.broadcasted_iota(jnp.int32, (N_HEADS, ncmp), 1)
    rowh = lax.broadcasted_iota(jnp.int32, (N_HEADS, ncmp), 0)
    mask = (nidx < N_CMP_S) & (nidx * CMP_STRIDE + (CMP_BLOCK - 1) <= PAST_LEN)
    rowo = lax.broadcasted_iota(jnp.int32, (N_HEADS, LANES), 0)
    r8 = lax.broadcasted_iota(jnp.int32, (8, N_SLC_PAD), 0)
    o_acc = jnp.zeros((N_HEADS, LANES), f32)
    imp = jnp.zeros((8, N_SLC_PAD), f32)
    for g in range(N_KV):
        gl = slice(g * LANES, (g + 1) * LANES)
        s = jnp.where(mask, _nt(qz, kd_ref[:, gl]), NEG)
        m = jnp.max(s, axis=1, keepdims=True)
        ex = jnp.where(mask, jnp.exp(s - m), 0.0)
        p = ex * (1.0 / jnp.maximum(jnp.sum(ex, axis=1, keepdims=True), 1e-30))
        p = jnp.where(rowh // Q_PER_KV == g, p, 0.0)
        pb = p.astype(bf16)
        o_l = jnp.dot(pb, vl_ref[:, gl], preferred_element_type=f32)
        o_h = jnp.dot(pb, vh_ref[:, gl], preferred_element_type=f32)
        o_acc = o_acc + jnp.where(rowo % 2 == 0, o_l, o_h)
        hi, lo = _split2(p)
        imp_h = jnp.dot(hi, mts_ref[...], preferred_element_type=f32) + jnp.dot(lo, mts_ref[...], preferred_element_type=f32)
        imp = imp + jnp.where(r8 == g, jnp.sum(imp_h, axis=0, keepdims=True), 0.0)
    o_ref[...] = o_acc
    imp_ref[...] = imp


def _topk_sample_kernel(imp_ref, idx_ref):
    rows = imp_ref.shape[0]
    j = lax.broadcasted_iota(jnp.int32, (rows, N_SLC_PAD), 1)
    jt = PAST_LEN // SEL_BLOCK
    imp = imp_ref[...]
    imp = jnp.where((j == 0) | (j == jt) | (j == jt - 1), SELECT_FORCE, imp)
    imp = jnp.where(j > jt, -SELECT_FORCE, imp)
    imp = jnp.where(j >= N_SLC_S, NEG, imp)
    jf = j.astype(f32)
    lane = lax.broadcasted_iota(jnp.int32, (rows, LANES), 1)
    picked = jnp.zeros((rows, LANES), f32)
    for k in range(SEL_TOPN):
        m = jnp.max(imp, axis=1, keepdims=True)
        ix = jnp.min(jnp.where(imp == m, jf, float(N_SLC_PAD)), axis=1, keepdims=True)
        picked = jnp.where(lane == k, ix, picked)
        imp = jnp.where(jf == ix, GONE, imp)
    idx_ref[...] = picked.astype(jnp.int32)


def _cmp_attn_sample(qz, kd, vl, vh):
    nb, ncmp = kd.shape[0], kd.shape[1]
    mts = np.zeros((ncmp, N_SLC_PAD), np.float32)
    mts[:, :N_SLC_S] = _cmp_to_sel_matrix_t(ncmp, N_SLC_S, N_CMP_S).T
    mts = jnp.asarray(mts, bf16)
    kv_spec = pl.BlockSpec((None, ncmp, KV_W), lambda b: (b, 0, 0))
    o16, imp = pl.pallas_call(
        _cmp_attn_sample_kernel,
        grid=(nb,),
        in_specs=[pl.BlockSpec((None, N_HEADS, LANES), lambda b: (b, 0, 0)), kv_spec, kv_spec, kv_spec,
                  pl.BlockSpec(mts.shape, lambda b: (0, 0))],
        out_specs=[pl.BlockSpec((None, N_HEADS, LANES), lambda b: (b, 0, 0)),
                   pl.BlockSpec((None, 8, N_SLC_PAD), lambda b: (b, 0, 0))],
        out_shape=[jax.ShapeDtypeStruct((nb, N_HEADS, LANES), f32), jax.ShapeDtypeStruct((nb, 8, N_SLC_PAD), f32)],
        compiler_params=_cparams(("parallel",)),
        name="cmp_attn_sample",
    )(qz, kd, vl, vh, mts)
    idx = pl.pallas_call(
        _topk_sample_kernel,
        out_shape=jax.ShapeDtypeStruct((nb * 8, LANES), jnp.int32),
        name="topk_sample",
    )(imp.reshape(nb * 8, N_SLC_PAD))
    return o16, idx.reshape(nb, 8, LANES)


def _one_query_softmax(s, s_new, vt, v_new):
    m = jnp.maximum(jnp.max(s, axis=1, keepdims=True), s_new)
    p = jnp.exp(s - m)
    p_new = jnp.exp(s_new - m)
    den = jnp.sum(p, axis=1, keepdims=True) + p_new
    num = _nt(p.astype(bf16), vt) + p_new * v_new
    return num * (1.0 / den)


def _selwin_sample_kernel(idx_ref, pg_ref, *refs):
    del pg_ref
    kv = refs[:SEL_TOPN]
    q_ref, ksn_ref, win_ref, kwn_ref, os_ref, ow_ref = refs[SEL_TOPN:]
    b, g = pl.program_id(0), pl.program_id(1)
    q8 = q_ref[...] * ATTN_SCALE
    qb = q8.astype(bf16)
    kcat = jnp.concatenate([r[0] for r in kv], axis=1).astype(bf16)
    vcat = jnp.concatenate([r[1] for r in kv], axis=1).astype(bf16)
    n = SEL_TOPN * PAGE_SIZE
    lane = lax.broadcasted_iota(jnp.int32, (8, n), 1)
    slot = lane // PAGE_SIZE
    blk = jnp.zeros((8, n), jnp.int32)
    for k in range(SEL_TOPN):
        blk = jnp.where(slot == k, idx_ref[(b * N_KV + g) * SEL_TOPN + k], blk)
    live = (blk < N_SLC_S - 1) & ((lane % PAGE_SIZE) // SEL_BLOCK == blk % HALF_PAGE)
    s = jnp.where(live, jnp.dot(qb, kcat, preferred_element_type=f32), NEG)
    k_new, v_new = ksn_ref[pl.ds(g, 1), :], ksn_ref[pl.ds(N_KV + g, 1), :]
    s_new = jnp.sum(q8 * k_new, axis=1, keepdims=True)
    os_ref[...] = _one_query_softmax(s, s_new, vcat, v_new)
    nbuf = win_ref.shape[-1]
    i = lax.broadcasted_iota(jnp.int32, (8, nbuf), 1)
    s = jnp.where(nbuf - i < WINDOW, jnp.dot(qb, win_ref[0].astype(bf16), preferred_element_type=f32), NEG)
    k_new, v_new = kwn_ref[pl.ds(g, 1), :], kwn_ref[pl.ds(N_KV + g, 1), :]
    s_new = jnp.sum(q8 * k_new, axis=1, keepdims=True)
    ow_ref[...] = _one_query_softmax(s, s_new, win_ref[1].astype(bf16), v_new)


def _selwin_sample(idx, page_table, cache_s, q_s, ks_new, win_buf, kw_new):
    nb = q_s.shape[0]
    nbuf = win_buf.shape[1]
    pages = cache_s.transpose(0, 2, 3, 4, 1)
    wb = win_buf.transpose(0, 2, 3, 4, 1)
    qg = q_s.reshape(nb, N_KV, Q_PER_KV, HEAD_DIM)
    q8 = jnp.concatenate([qg, jnp.zeros_like(qg)], axis=2)

    logical = jnp.minimum(idx, N_SLC_S - 2) // HALF_PAGE
    page_of = jnp.take_along_axis(page_table, logical.reshape(nb, N_KV * SEL_TOPN), axis=1)

    def blk_map(b, g, idx_ref, pg_ref, k):
        return (pg_ref[(b * N_KV + g) * SEL_TOPN + k], 0, g, 0, 0)

    blk_specs = [pl.BlockSpec((None, 2, None, HEAD_DIM, PAGE_SIZE), functools.partial(blk_map, k=k))
                 for k in range(SEL_TOPN)]
    new_spec = pl.BlockSpec((None, 2 * N_KV, HEAD_DIM), lambda b, g, *_: (b, 0, 0))
    win_spec = pl.BlockSpec((None, 2, None, HEAD_DIM, nbuf), lambda b, g, *_: (b, 0, g, 0, 0))
    o_spec = pl.BlockSpec((None, None, 8, HEAD_DIM), lambda b, g, *_: (b, g, 0, 0))
    grid_spec = pltpu.PrefetchScalarGridSpec(
        num_scalar_prefetch=2,
        grid=(nb, N_KV),
        in_specs=blk_specs + [o_spec, new_spec, win_spec, new_spec],
        out_specs=[o_spec, o_spec])
    o_shape = jax.ShapeDtypeStruct((nb, N_KV, 8, HEAD_DIM), f32)
    ks3, kw3 = ks_new.reshape(nb, 2 * N_KV, HEAD_DIM), kw_new.reshape(nb, 2 * N_KV, HEAD_DIM)
    os8, ow8 = pl.pallas_call(
        _selwin_sample_kernel,
        grid_spec=grid_spec,
        out_shape=[o_shape, o_shape],
        compiler_params=_cparams(("arbitrary", "arbitrary")),
        name="selwin_sample",
    )(idx.reshape(-1), page_of.reshape(-1), *([pages] * SEL_TOPN), q8, ks3, wb, kw3)
    return os8[:, :, :Q_PER_KV].reshape(nb, ATTN_W), ow8[:, :, :Q_PER_KV].reshape(nb, ATTN_W)


def _sample_query_layouts(q):
    nb = q.shape[0]
    q16 = q.reshape(nb, N_HEADS, HEAD_DIM)
    z = jnp.zeros_like(q16)
    lo, hi = jnp.concatenate([q16, z], axis=-1), jnp.concatenate([z, q16], axis=-1)
    h = jnp.arange(N_HEADS)[None, :, None]
    return jnp.where(h % 2 == 0, lo, hi)


def _unpad_heads_by_parity(o16):
    nb = o16.shape[0]
    o = o16.reshape(nb, N_HEADS // 2, 2, 2, HEAD_DIM)
    return jnp.stack([o[:, :, 0, 0], o[:, :, 1, 1]], axis=2).reshape(nb, ATTN_W)


def _finish(x, ocmp, osel, owin, gd, ssd_y, w_out, ln_mlp, w_up, w_down, ln_final, tm, tm_mlp, tf):
    x1 = _outproj(x, ocmp, osel, owin, gd, ssd_y, _gate_expand(), w_out, tm)
    return _mlp(x1, ln_mlp, w_up, w_down, ln_final, tm_mlp, tf)


def kernel(x_prompt, x_sample, cache_kv_cmp, cache_kv_sel, state_kv_win, state_conv, state_ssm, page_table, ln_mix, w_in, cmp_pe, cmp_w1, cmp_b1, cmp_w2, cmp_b2, conv_w, conv_b, dt_bias, a_log, d_skip, ssm_norm, w_out, ln_mlp, w_up, w_down, ln_final):
    nb = x_sample.shape[0]
    w_in_p = _prep_w_in(w_in[0])
    w_out_b, w_up_b, w_down_b = w_out[0].astype(bf16), w_up[0].astype(bf16), w_down[0].astype(bf16)
    lnw, lnm, lnf = ln_mix[0][None], ln_mlp[0][None], ln_final[None]
    cw = _compress_weights(cmp_pe[0], cmp_w1[0], cmp_b1[0], cmp_w2[0], cmp_b2[0])
    mc = _mamba_consts(conv_w[0], conv_b[0], dt_bias[0], a_log[0], d_skip[0], ssm_norm[0])
    kv_shape = (2, N_KV, HEAD_DIM)

    xp = x_prompt.reshape(BATCH * SEQ, D_MODEL)
    q, z, xbc, gd, kct, kst, kwt, kds, kdw = _inproj_prompt(xp, lnw, w_in_p, 256)
    kd, _, _, vt = _compress_prompt(kct, cw)
    attn = _prompt_attention(q, gd, kd, vt, kds, kst, kdw, kwt)
    ssd_y, ssm_p = _mamba_prompt(z, xbc, gd, mc)
    y_prompt = _mlp(_outproj_mixed(xp, attn, ssd_y, w_out_b, 512), lnm, w_up_b, w_down_b, lnf, 512, 1024)

    xs = x_sample.reshape(nb, D_MODEL)
    q_s, z_s, xbc_s, gd_s, kc_s, ks_s, kw_s = _inproj(xs, lnw, w_in_p, nb)
    kd_s, vl_s, vh_s = _compress_sample(cache_kv_cmp[0], page_table, cw)
    ocmp16, idx8 = _cmp_attn_sample(_sample_query_layouts(q_s), kd_s, vl_s, vh_s)
    idx = idx8[:, :N_KV, :SEL_TOPN]
    osel_s, owin_s = _selwin_sample(idx, page_table, cache_kv_sel[0], q_s, ks_s, state_kv_win[0], kw_s)
    ssd_y_s, ssm_s = _mamba_sample(z_s, xbc_s, gd_s, state_conv[0], state_ssm[0], mc)
    y_sample = _finish(xs, _unpad_heads_by_parity(ocmp16), osel_s, owin_s, gd_s, ssd_y_s,
                       w_out_b, lnm, w_up_b, w_down_b, lnf, nb, nb, 1024)

    def token_major(t):
        return t.reshape(BATCH, *kv_shape, t.shape[-1]).transpose(0, 4, 1, 2, 3)[None]

    conv_p = xbc.reshape(BATCH, SEQ, CONV_DIM)[:, -(CONV_W - 1):]
    kv_win_s = jnp.concatenate([state_kv_win[0], kw_s.reshape(nb, 1, *kv_shape)], axis=1)[:, -WINDOW:]
    conv_s = jnp.concatenate([state_conv[0], xbc_s[:, None]], axis=1)[:, -(CONV_W - 1):]
    return (y_prompt.reshape(BATCH, SEQ, D_MODEL), y_sample.reshape(nb, 1, D_MODEL),
            token_major(kct), token_major(kst), token_major(kwt[:, :, -min(WINDOW, SEQ):]),
            conv_p[None], ssm_p[None],
            kc_s.reshape(1, nb, 1, *kv_shape), ks_s.reshape(1, nb, 1, *kv_shape), kv_win_s[None],
            conv_s[None], ssm_s[None])
```

```python
import functools
import math

import jax
import jax.numpy as jnp
import numpy as np
from jax import lax
from jax.experimental import pallas as pl
from jax.experimental.pallas import tpu as pltpu

f32 = jnp.float32
bf16 = jnp.bfloat16

D_MODEL = 2048
BATCH = 2
SEQ = 4096
DEC_BATCH = 32
PAST_LEN = 8192
PAGE_SIZE = 128
HEAD_DIM = 64
N_HEADS = 16
N_KV = 4
Q_PER_KV = 4
ATTN_W = 1024
KV_W = 512
CMP_BLOCK = 32
CMP_STRIDE = 16
CMP_HIDDEN = 128
SEL_BLOCK = 64
SEL_TOPN = 16
WINDOW = 512
SELECT_FORCE = 1.0e4
ATTN_SCALE = HEAD_DIM ** -0.5
D_INNER = 1024
SSM_HEAD_DIM = 64
N_SSM_HEADS = 16
N_SSM_GROUPS = 2
D_STATE = 128
CONV_W = 4
CONV_DIM = D_INNER + 2 * N_SSM_GROUPS * D_STATE
D_FF = 4 * D_MODEL
IN_SPLITS = (ATTN_W, KV_W, KV_W, KV_W, 3 * N_HEADS, D_INNER, CONV_DIM, N_SSM_HEADS)
NORM_EPS = 1e-5

LANES = 128
GD_W = LANES
DT_LANE0 = 3 * N_HEADS
N_PAGES = PAST_LEN // PAGE_SIZE
SEG_PER_PAGE = PAGE_SIZE // CMP_STRIDE
NEG = -1.0e30
SEL_NEG = -1.0e9
VMEM_LIMIT = 56 * 1024 * 1024


def _cparams(sem):
    return pltpu.CompilerParams(dimension_semantics=sem, vmem_limit_bytes=VMEM_LIMIT)


def _nt(a, b):
    return lax.dot_general(a, b, (((1,), (1,)), ((), ())), preferred_element_type=f32)


def _split2(x):
    hi = x.astype(bf16)
    lo = (x - hi.astype(f32)).astype(bf16)
    return hi, lo


def _split3(x):
    hi = x.astype(bf16)
    r = x - hi.astype(f32)
    mid = r.astype(bf16)
    lo = (r - mid.astype(f32)).astype(bf16)
    return hi, mid, lo


def _dot_exact(x, w):
    return sum(jnp.dot(t, w, preferred_element_type=f32) for t in _split3(x))


def _silu(x):
    return x * (1.0 / (1.0 + jnp.exp(-x)))


def _sigmoid(x):
    return 1.0 / (1.0 + jnp.exp(-x))


def _softplus(x):
    return jnp.maximum(x, 0.0) + jnp.log1p(jnp.exp(-jnp.abs(x)))


IN_OUT_WIDTHS = (ATTN_W, D_INNER, CONV_DIM, GD_W, KV_W, KV_W, KV_W)
KD_W = N_KV * LANES
IN_COLS_ROWMAJOR = sum(IN_OUT_WIDTHS)


def _normed(x_ref, lnw_ref):
    x = x_ref[...]
    return (x * lax.rsqrt(jnp.mean(x * x, axis=-1, keepdims=True) + NORM_EPS) * lnw_ref[...]).astype(bf16)


def _inproj_kernel(x_ref, lnw_ref, w_ref, *out_refs):
    h = _normed(x_ref, lnw_ref)
    off = 0
    for ref in out_refs:
        n = ref.shape[-1]
        ref[...] = jnp.dot(h, w_ref[:, off:off + n], preferred_element_type=f32)
        off += n


def _inproj_prompt_kernel(x_ref, lnw_ref, w_ref, q_ref, z_ref, xbc_ref, gd_ref,
                          kct_ref, kst_ref, kwt_ref, kds_ref, kdw_ref, t_ref):
    h = _normed(x_ref, lnw_ref)
    off = 0

    def proj(n):
        nonlocal off
        y = jnp.dot(h, w_ref[:, off:off + n], preferred_element_type=f32)
        off += n
        return y

    for ref in (q_ref, z_ref, xbc_ref, gd_ref):
        ref[...] = proj(ref.shape[-1])
    for t_out in (kct_ref, kst_ref, kwt_ref):
        t_ref[...] = proj(KV_W)
        t_out[...] = t_ref[...].T
    kds_ref[...] = proj(KD_W).astype(bf16)
    kdw_ref[...] = proj(KD_W).astype(bf16)


def _inproj(x, lnw, w_perm, tm):
    m = x.shape[0]
    return pl.pallas_call(
        _inproj_kernel,
        grid=(m // tm,),
        in_specs=[pl.BlockSpec((tm, D_MODEL), lambda i: (i, 0)),
                  pl.BlockSpec((1, D_MODEL), lambda i: (0, 0)),
                  pl.BlockSpec((D_MODEL, IN_COLS_ROWMAJOR), lambda i: (0, 0), pipeline_mode=pl.Buffered(1))],
        out_specs=[pl.BlockSpec((tm, n), lambda i: (i, 0)) for n in IN_OUT_WIDTHS],
        out_shape=[jax.ShapeDtypeStruct((m, n), f32) for n in IN_OUT_WIDTHS],
        compiler_params=_cparams(("parallel",)),
        name="inproj",
    )(x, lnw, w_perm)


def _inproj_prompt(x, lnw, w_perm, tm):
    nq = SEQ // tm
    row = lambda n: pl.BlockSpec((tm, n), lambda i: (i, 0))
    tmin = pl.BlockSpec((None, KV_W, tm), lambda i: (i // nq, 0, i % nq))
    rm = lambda n, dt=f32: jax.ShapeDtypeStruct((BATCH * SEQ, n), dt)
    tshape = jax.ShapeDtypeStruct((BATCH, KV_W, SEQ), f32)
    return pl.pallas_call(
        _inproj_prompt_kernel,
        grid=(BATCH * nq,),
        in_specs=[pl.BlockSpec((tm, D_MODEL), lambda i: (i, 0)),
                  pl.BlockSpec((1, D_MODEL), lambda i: (0, 0)),
                  pl.BlockSpec(w_perm.shape, lambda i: (0, 0), pipeline_mode=pl.Buffered(1))],
        out_specs=[row(ATTN_W), row(D_INNER), row(CONV_DIM), row(GD_W), tmin, tmin, tmin, row(KD_W), row(KD_W)],
        out_shape=[rm(ATTN_W), rm(D_INNER), rm(CONV_DIM), rm(GD_W), tshape, tshape, tshape,
                   rm(KD_W, bf16), rm(KD_W, bf16)],
        scratch_shapes=[pltpu.VMEM((tm, KV_W), f32)],
        compiler_params=_cparams(("parallel",)),
        name="inproj_prompt",
    )(x, lnw, w_perm)


def _prep_w_in(w_in):
    parts, off = [], 0
    for width in IN_SPLITS:
        parts.append(w_in[:, off:off + width])
        off += width
    q, kc, ks, kw, g, z, xbc, dt = parts
    gd = jnp.concatenate([g, dt, jnp.zeros((D_MODEL, GD_W - 4 * N_HEADS), w_in.dtype)], axis=1)

    def dup_keys(kv):
        k = kv[:, :KV_W // 2].reshape(D_MODEL, N_KV, 1, HEAD_DIM)
        return jnp.broadcast_to(k, (D_MODEL, N_KV, 2, HEAD_DIM)).reshape(D_MODEL, KD_W)

    return jnp.concatenate([q, z, xbc, gd, kc, ks, kw, dup_keys(ks), dup_keys(kw)], axis=1).astype(bf16)


def _outproj_kernel(x_ref, oc_ref, os_ref, ow_ref, gd_ref, y_ref, eg_ref, w_ref, o_ref):
    gates = _sigmoid(gd_ref[...])
    hi, lo = _split2(gates)
    attn = None
    for c, br in enumerate((oc_ref, os_ref, ow_ref)):
        ge = (jnp.dot(hi, eg_ref[c], preferred_element_type=f32)
              + jnp.dot(lo, eg_ref[c], preferred_element_type=f32))
        term = ge * br[...]
        attn = term if attn is None else attn + term
    mix = (jnp.dot(attn.astype(bf16), w_ref[:ATTN_W, :], preferred_element_type=f32)
           + jnp.dot(y_ref[...], w_ref[ATTN_W:, :], preferred_element_type=f32))
    o_ref[...] = x_ref[...] + mix


def _outproj_mixed_kernel(x_ref, a_ref, y_ref, w_ref, o_ref):
    mix = (jnp.dot(a_ref[...], w_ref[:ATTN_W, :], preferred_element_type=f32)
           + jnp.dot(y_ref[...], w_ref[ATTN_W:, :], preferred_element_type=f32))
    o_ref[...] = x_ref[...] + mix


def _outproj_mixed(x, attn, ssd_y, w_out, tm):
    m = x.shape[0]
    row = lambda w: pl.BlockSpec((tm, w), lambda i: (i, 0))
    return pl.pallas_call(
        _outproj_mixed_kernel,
        grid=(m // tm,),
        in_specs=[row(D_MODEL), row(ATTN_W), row(D_INNER),
                  pl.BlockSpec((ATTN_W + D_INNER, D_MODEL), lambda i: (0, 0))],
        out_specs=row(D_MODEL),
        out_shape=jax.ShapeDtypeStruct((m, D_MODEL), f32),
        compiler_params=_cparams(("parallel",)),
        name="outproj_mixed",
    )(x, attn, ssd_y, w_out)


def _outproj(x, ocmp, osel, owin, gd, ssd_y, eg, w_out, tm):
    m = x.shape[0]
    row = lambda w: pl.BlockSpec((tm, w), lambda i: (i, 0))
    return pl.pallas_call(
        _outproj_kernel,
        grid=(m // tm,),
        in_specs=[row(D_MODEL), row(ATTN_W), row(ATTN_W), row(ATTN_W), row(GD_W), row(D_INNER),
                  pl.BlockSpec((3, GD_W, ATTN_W), lambda i: (0, 0, 0)),
                  pl.BlockSpec((ATTN_W + D_INNER, D_MODEL), lambda i: (0, 0))],
        out_specs=row(D_MODEL),
        out_shape=jax.ShapeDtypeStruct((m, D_MODEL), f32),
        compiler_params=_cparams(("parallel",)),
        name="outproj",
    )(x, ocmp, osel, owin, gd, ssd_y, eg, w_out)


def _gate_expand():
    eg = np.zeros((3, GD_W, ATTN_W), np.float32)
    for h in range(N_HEADS):
        for c in range(3):
            eg[c, h * 3 + c, h * HEAD_DIM:(h + 1) * HEAD_DIM] = 1.0
    return jnp.asarray(eg, bf16)


def _mlp_kernel(x_ref, ln_ref, wu_ref, wd_ref, lnf_ref, o_ref, h_ref, acc_ref):
    k = pl.program_id(1)

    @pl.when(k == 0)
    def _():
        x = x_ref[...]
        h = x * lax.rsqrt(jnp.mean(x * x, axis=-1, keepdims=True) + NORM_EPS) * ln_ref[...]
        h_ref[...] = h.astype(bf16)
        acc_ref[...] = jnp.zeros_like(acc_ref)

    u = jnp.dot(h_ref[...], wu_ref[...], preferred_element_type=f32)
    u = jnp.square(jnp.maximum(u, 0.0)).astype(bf16)
    acc_ref[...] += jnp.dot(u, wd_ref[...], preferred_element_type=f32)

    @pl.when(k == pl.num_programs(1) - 1)
    def _():
        y = x_ref[...] + acc_ref[...]
        y = y * lax.rsqrt(jnp.mean(y * y, axis=-1, keepdims=True) + NORM_EPS) * lnf_ref[...]
        o_ref[...] = y


def _mlp(x, ln_mlp, w_up, w_down, ln_final, tm, tf):
    m = x.shape[0]
    return pl.pallas_call(
        _mlp_kernel,
        grid=(m // tm, D_FF // tf),
        in_specs=[pl.BlockSpec((tm, D_MODEL), lambda i, k: (i, 0)),
                  pl.BlockSpec((1, D_MODEL), lambda i, k: (0, 0)),
                  pl.BlockSpec((D_MODEL, tf), lambda i, k: (0, k)),
                  pl.BlockSpec((tf, D_MODEL), lambda i, k: (k, 0)),
                  pl.BlockSpec((1, D_MODEL), lambda i, k: (0, 0))],
        out_specs=pl.BlockSpec((tm, D_MODEL), lambda i, k: (i, 0)),
        out_shape=jax.ShapeDtypeStruct((m, D_MODEL), f32),
        scratch_shapes=[pltpu.VMEM((tm, D_MODEL), bf16), pltpu.VMEM((tm, D_MODEL), f32)],
        compiler_params=_cparams(("parallel", "arbitrary")),
        name="mlp",
    )(x, ln_mlp, w_up, w_down, ln_final)


N_SLAB_K = CMP_STRIDE * LANES


def _compress_weights(cmp_pe, cmp_w1, cmp_b1, cmp_w2, cmp_b2):
    eye2 = jnp.eye(2, dtype=f32)
    w1 = cmp_w1.reshape(2, 2, CMP_STRIDE, HEAD_DIM, CMP_HIDDEN)
    w1cat = jnp.einsum('crjdf,ab->cjadrbf', w1, eye2).reshape(2, N_SLAB_K, 4 * CMP_HIDDEN).astype(bf16)
    w1flat = cmp_w1.reshape(2, CMP_BLOCK * HEAD_DIM, CMP_HIDDEN).astype(bf16)
    pe8 = jnp.broadcast_to(cmp_pe.reshape(2, 1, CMP_BLOCK * HEAD_DIM), (2, 8, CMP_BLOCK * HEAD_DIM))
    b1 = cmp_b1.reshape(2, 1, CMP_HIDDEN)
    wk, wv = cmp_w2[0], cmp_w2[1]
    zk = jnp.zeros_like(wk)
    w2k = jnp.einsum('fd,ab,e->afbed', wk, eye2, jnp.ones((2,), f32)).reshape(2 * CMP_HIDDEN, 4 * HEAD_DIM)
    b2k = jnp.tile(cmp_b2[0], 4)[None]
    vl = jnp.einsum('fd,ab,e->afbed', wv, eye2, jnp.array([1.0, 0.0], f32)).reshape(2 * CMP_HIDDEN, 4 * HEAD_DIM)
    vh = jnp.einsum('fd,ab,e->afbed', wv, eye2, jnp.array([0.0, 1.0], f32)).reshape(2 * CMP_HIDDEN, 4 * HEAD_DIM)
    w2v = jnp.concatenate([vl, vh], axis=1)
    zb = jnp.zeros((HEAD_DIM,), f32)
    b2v = jnp.concatenate([cmp_b2[1], zb, cmp_b2[1], zb, zb, cmp_b2[1], zb, cmp_b2[1]])[None]
    del zk
    return pe8, w1flat, b1, w1cat, w2k.astype(bf16), b2k, w2v.astype(bf16), b2v


def _compress_slab(a, c, gp, prm_refs, out_refs):
    pe_ref, w1f_ref, b1_ref, w1_ref, w2k_ref, b2k_ref, w2v_ref, b2v_ref = prm_refs
    kd_ref, vl_ref, vh_ref = out_refs
    pet = jnp.dot(pe_ref[c].astype(bf16), w1f_ref[c], preferred_element_type=f32)[0:1] + b1_ref[c]
    bias = jnp.concatenate([pet, pet], axis=1)
    p = jnp.dot(a, w1_ref[c], preferred_element_type=f32)
    s = p.shape[0]
    hid = p[:, :2 * CMP_HIDDEN] + pltpu.roll(p[:, 2 * CMP_HIDDEN:], s - 1, axis=0)
    hid = _silu(hid + bias).astype(bf16)
    w = 4 * HEAD_DIM
    if c == 0:
        kd_ref[:, gp * w:(gp + 1) * w] = (jnp.dot(hid, w2k_ref[...], preferred_element_type=f32)
                                          + b2k_ref[...]).astype(bf16)
    else:
        o = jnp.dot(hid, w2v_ref[...], preferred_element_type=f32) + b2v_ref[...]
        vl_ref[:, gp * w:(gp + 1) * w] = o[:, :w].astype(bf16)
        vh_ref[:, gp * w:(gp + 1) * w] = o[:, w:].astype(bf16)


def _compress_token_minor(pair, n_pairs, perm_ref, prm_refs, out_refs, o_ref):
    rows = 2 * SEG_PER_PAGE
    for k in range(n_pairs):
        o_ref[k] = _nt(perm_ref[...], pair(k).astype(bf16)).astype(bf16)
    nseg = n_pairs * rows
    for c in range(2):
        for gp in range(2):
            l0 = c * (KV_W // 2) + gp * LANES
            a = jnp.concatenate([o_ref[:, j * rows:(j + 1) * rows, l0:l0 + LANES].reshape(nseg, LANES)
                                 for j in range(CMP_STRIDE)], axis=1)
            _compress_slab(a, c, gp, prm_refs, out_refs)


def _compress_prompt_kernel(x_ref, perm_ref, *refs):
    chunk = 2 * PAGE_SIZE
    _compress_token_minor(lambda k: x_ref[:, k * chunk:(k + 1) * chunk], SEQ // chunk, perm_ref,
                          refs[:8], refs[8:11], refs[12])
    vl_ref, vt_ref = refs[9], refs[11]
    for g in range(N_KV):
        v_pad = vl_ref[:, g * LANES:(g + 1) * LANES].astype(f32)
        vt_ref[g * HEAD_DIM:(g + 1) * HEAD_DIM, :] = v_pad.T[:HEAD_DIM].astype(bf16)


def _seg_perm():
    m = np.zeros((2 * PAGE_SIZE, 2 * PAGE_SIZE), np.float32)
    for pp in range(2):
        for s in range(SEG_PER_PAGE):
            for j in range(CMP_STRIDE):
                m[j * 2 * SEG_PER_PAGE + pp * SEG_PER_PAGE + s, pp * PAGE_SIZE + s * CMP_STRIDE + j] = 1.0
    return jnp.asarray(m, bf16)


def _compress_sample_kernel(pt_ref, *refs):
    del pt_ref
    pages, perm_ref = refs[:N_PAGES], refs[N_PAGES]
    prm_refs, out_refs, o_ref = refs[N_PAGES + 1:N_PAGES + 9], refs[N_PAGES + 9:N_PAGES + 12], refs[-1]
    _compress_token_minor(lambda k: jnp.concatenate([pages[2 * k][...], pages[2 * k + 1][...]], axis=1),
                          N_PAGES // 2, perm_ref, prm_refs, out_refs, o_ref)


def _const_spec(a, n_grid, n_prefetch=0):
    zeros = (0,) * a.ndim
    if n_grid == 1:
        return pl.BlockSpec(a.shape, (lambda b, *_: zeros))
    return pl.BlockSpec(a.shape, (lambda b, c, *_: zeros))


def _compress_prompt(kc_t, cw):
    nseg = SEQ // CMP_STRIDE
    perm = _seg_perm()
    out = jax.ShapeDtypeStruct((BATCH, nseg, KV_W), bf16)
    vt_rows = N_KV * HEAD_DIM
    return pl.pallas_call(
        _compress_prompt_kernel,
        grid=(BATCH,),
        in_specs=[pl.BlockSpec((None, KV_W, SEQ), lambda b: (b, 0, 0)), _const_spec(perm, 1)]
                 + [_const_spec(a, 1) for a in cw],
        out_specs=[pl.BlockSpec((None, nseg, KV_W), lambda b: (b, 0, 0))] * 3
                  + [pl.BlockSpec((None, vt_rows, nseg), lambda b: (b, 0, 0))],
        out_shape=[out] * 3 + [jax.ShapeDtypeStruct((BATCH, vt_rows, nseg), bf16)],
        scratch_shapes=[pltpu.VMEM((SEQ // (2 * PAGE_SIZE), 2 * PAGE_SIZE, KV_W), bf16)],
        compiler_params=_cparams(("parallel",)),
        name="compress_prompt",
    )(kc_t, perm, *cw)


def _compress_sample(cache, page_table, cw):
    nseg = PAST_LEN // CMP_STRIDE
    x = _token_minor(cache)
    perm = _seg_perm()
    out = jax.ShapeDtypeStruct((DEC_BATCH, nseg, KV_W), bf16)
    page_specs = [pl.BlockSpec((None, KV_W, PAGE_SIZE), functools.partial(lambda b, pt, p: (pt[b * N_PAGES + p], 0, 0), p=p))
                  for p in range(N_PAGES)]
    grid_spec = pltpu.PrefetchScalarGridSpec(
        num_scalar_prefetch=1,
        grid=(DEC_BATCH,),
        in_specs=page_specs + [_const_spec(perm, 1)] + [_const_spec(a, 1) for a in cw],
        out_specs=[pl.BlockSpec((None, nseg, KV_W), lambda b, pt: (b, 0, 0))] * 3,
        scratch_shapes=[pltpu.VMEM((N_PAGES // 2, 2 * PAGE_SIZE, KV_W), bf16)])
    return pl.pallas_call(
        _compress_sample_kernel,
        grid_spec=grid_spec,
        out_shape=[out] * 3,
        compiler_params=_cparams(("arbitrary",)),
        name="compress_sample",
    )(page_table.reshape(-1), *([x] * N_PAGES), perm, *cw)


def _token_minor(kv):
    n, t = kv.shape[0], kv.shape[1]
    return kv.transpose(0, 2, 3, 4, 1).reshape(n, KV_W, t)


TQ = 256
N_SEL = SEQ // SEL_BLOCK


def _half_mask(shape, hi):
    lane = lax.broadcasted_iota(jnp.int32, shape, 1)
    return (lane >= HEAD_DIM) if hi else (lane < HEAD_DIM)


def _cmp_branch(i, qs_ref, kd_ref, vt_ref, mt_ref):
    t0 = i * TQ
    ncmp = kd_ref.shape[0]
    nidx = lax.broadcasted_iota(jnp.int32, (ncmp, Q_PER_KV * TQ), 0)
    tpos = t0 + (lax.broadcasted_iota(jnp.int32, (ncmp, Q_PER_KV * TQ), 1) & (TQ - 1))
    mask = nidx * CMP_STRIDE + (CMP_BLOCK - 1) <= tpos
    s = jnp.where(mask, _nt(kd_ref[...], qs_ref[:, 0:LANES]), NEG)
    m = jnp.max(s, axis=0, keepdims=True)
    ex = jnp.where(mask, jnp.exp2(s - m), 0.0)
    p = ex * (1.0 / jnp.maximum(jnp.sum(ex, axis=0, keepdims=True), 1e-30))
    pb = p.astype(bf16)
    vt = vt_ref[...]
    heads = [jnp.dot(vt, pb[:, r * TQ:(r + 1) * TQ], preferred_element_type=f32) for r in range(Q_PER_KV)]
    heads = jnp.concatenate(heads, axis=0)
    psum = sum(p[:, r * TQ:(r + 1) * TQ] for r in range(Q_PER_KV))
    hi, lo = _split2(psum)
    imp = (jnp.dot(mt_ref[...], hi, preferred_element_type=f32)
           + jnp.dot(mt_ref[...], lo, preferred_element_type=f32))
    imp = imp[:N_SEL]
    j = lax.broadcasted_iota(jnp.int32, (N_SEL, TQ), 0)
    jt = (t0 + lax.broadcasted_iota(jnp.int32, (N_SEL, TQ), 1)) // SEL_BLOCK
    imp = jnp.where((j == 0) | (j == jt) | (j == jt - 1), SELECT_FORCE, imp)
    imp = jnp.where(j > jt, -SELECT_FORCE, imp)
    cnt = jnp.zeros((N_SEL, TQ), f32)
    for jp in range(N_SEL):
        row = imp[jp:jp + 1, :]
        cnt = cnt + jnp.where(j > jp, jnp.where(row >= imp, 1.0, 0.0), jnp.where(row > imp, 1.0, 0.0))
    selm = jnp.where((cnt < SEL_TOPN) & (j <= jt), 0.0, SEL_NEG)
    return heads, jnp.concatenate([selm, selm], axis=0).T


def _cmp_to_sel_matrix_t(n_cmp_pad, n_slc, n_cmp):
    ratio = SEL_BLOCK // CMP_STRIDE
    i = np.arange(n_cmp_pad)[None, :]
    jj = np.arange(n_slc)[:, None]
    diff = i - ratio * jj
    mat = np.zeros((n_slc, n_cmp_pad), np.float32)
    for n in range(CMP_BLOCK // CMP_STRIDE):
        mat += ((diff + n >= 0) & (diff + n < ratio)).astype(np.float32)
    mat[:, n_cmp:] = 0.0
    return mat


VT_ROWS = HEAD_DIM + 16
SEL_KT = 8

def _flash_tile(qs, k, vt, mask, m, acc):
    s = _nt(k, qs)
    if mask is not None:
        s = jnp.where(mask, s, NEG)
    m_next = jnp.maximum(m, jnp.max(s, axis=0, keepdims=True))
    alpha = jnp.exp2(m - m_next)
    p = jnp.exp2(s - m_next).astype(bf16)
    pv = jnp.concatenate([jnp.dot(vt, p[:, r * TQ:(r + 1) * TQ], preferred_element_type=f32)
                          for r in range(Q_PER_KV)], axis=1)
    return m_next, alpha * acc + pv


def _flash_finish(acc):
    o_t = acc[:HEAD_DIM] * (1.0 / acc[HEAD_DIM:HEAD_DIM + 1])
    return jnp.concatenate([o_t[:, r * TQ:(r + 1) * TQ] for r in range(Q_PER_KV)], axis=0)


LOG2E = math.log2(math.e)


def _stack_queries(q_ref, qs_ref):
    for r in range(Q_PER_KV):
        q2 = q_ref[:, (r // 2) * LANES:(r // 2 + 1) * LANES] * (ATTN_SCALE * LOG2E)
        qs_ref[r * TQ:(r + 1) * TQ, 0:LANES] = jnp.where(_half_mask((TQ, LANES), r % 2), q2, 0.0).astype(bf16)


def _flash_init():
    return jnp.full((1, Q_PER_KV * TQ), NEG, f32), jnp.zeros((VT_ROWS, Q_PER_KV * TQ), f32)


def _sel_branch(i, selm, kd_ref, oh_ref, vt_ref, qs_ref, m_ref, acc_ref):
    selm = selm.astype(bf16)
    for r in range(Q_PER_KV):
        qs_ref[r * TQ:(r + 1) * TQ, LANES:2 * LANES] = selm
    qs = qs_ref[...]

    def span(first, n, mask, m, acc):
        rows = pl.ds(pl.multiple_of(first * TQ, TQ), n * TQ)
        k = jnp.concatenate([kd_ref[rows, :], oh_ref[rows, :]], axis=1)
        return _flash_tile(qs, k, _with_ones(vt_ref[:, rows]), mask, m, acc)

    full = i // SEL_KT
    m, acc = lax.fori_loop(0, full, lambda t, carry: span(SEL_KT * t, SEL_KT, None, *carry), _flash_init())
    m_ref[...] = m
    acc_ref[...] = acc
    for n in range(1, SEL_KT + 1):
        @pl.when(i % SEL_KT == n - 1)
        def _(n=n):
            key = lax.broadcasted_iota(jnp.int32, (n * TQ, Q_PER_KV * TQ), 0)
            qry = (lax.broadcasted_iota(jnp.int32, (n * TQ, Q_PER_KV * TQ), 1) & (TQ - 1)) + (n - 1) * TQ
            _, acc_n = span(full * SEL_KT, n, key <= qry, m_ref[...], acc_ref[...])
            acc_ref[...] = acc_n

    return _flash_finish(acc_ref[...])


def _win_branch(i, kd_ref, vt_ref, qs_ref, acc_ref):
    qs = qs_ref[:, 0:LANES]

    def span(first, n):
        rows = pl.ds(pl.multiple_of(first * TQ, TQ), n * TQ)
        key = lax.broadcasted_iota(jnp.int32, (n * TQ, Q_PER_KV * TQ), 0)
        qry = (lax.broadcasted_iota(jnp.int32, (n * TQ, Q_PER_KV * TQ), 1) & (TQ - 1)) + (n - 1) * TQ
        dist = qry - key
        _, acc = _flash_tile(qs, kd_ref[rows, :], _with_ones(vt_ref[:, rows]), (dist >= 0) & (dist < WINDOW),
                             *_flash_init())
        acc_ref[...] = acc

    n_back = WINDOW // TQ

    @pl.when(i >= n_back)
    def _():
        span(i - n_back, n_back + 1)

    for early in range(n_back):
        @pl.when(i == early)
        def _(early=early):
            span(0, early + 1)

    return _flash_finish(acc_ref[...])


def _with_ones(vt):
    return jnp.concatenate([vt.astype(bf16), jnp.ones((VT_ROWS - HEAD_DIM, vt.shape[1]), bf16)], axis=0)


def _prompt_attn_kernel(q_ref, gd_ref, kdc_ref, vtc_ref, mt_ref, kds_ref, oh_ref, vts_ref, kdw_ref, vtw_ref,
                        o_ref, qs_ref, m_ref, acc_ref, gt_ref):
    g, i = pl.program_id(1), pl.program_id(2)
    _stack_queries(q_ref, qs_ref)
    gt_ref[...] = _sigmoid(gd_ref[...]).T

    def gated(heads, branch):
        rows = [gt_ref[pl.ds((g * Q_PER_KV + r) * 3 + branch, 1), :] for r in range(Q_PER_KV)]
        return jnp.concatenate([heads[r * HEAD_DIM:(r + 1) * HEAD_DIM] * rows[r] for r in range(Q_PER_KV)], axis=0)

    heads, selm = _cmp_branch(i, qs_ref, kdc_ref, vtc_ref, mt_ref)
    total = gated(heads, 0)
    total = total + gated(_sel_branch(i, selm, kds_ref, oh_ref, vts_ref, qs_ref, m_ref, acc_ref), 1)
    total = total + gated(_win_branch(i, kdw_ref, vtw_ref, qs_ref, acc_ref), 2)
    o_ref[...] = total.T.astype(bf16)


def _prompt_attention(q, gd, kd_c, vt_c, kd_s, kv_s_t, kd_w, kv_w_t):
    assert WINDOW % TQ == 0
    ncmp = SEQ // CMP_STRIDE
    mt = np.zeros((LANES, ncmp), np.float32)
    mt[:N_SEL] = _cmp_to_sel_matrix_t(ncmp, N_SEL, ncmp - 1)
    mt = jnp.asarray(mt, bf16)
    blk = np.arange(SEQ)[:, None] // SEL_BLOCK == np.arange(LANES)[None, :]
    oh = jnp.asarray(blk.astype(np.float32), bf16)
    nq = SEQ // TQ
    q_spec = pl.BlockSpec((TQ, 2 * LANES), lambda b, g, i: (b * nq + i, g))
    cmp_spec = pl.BlockSpec((None, ncmp, LANES), lambda b, g, i: (b, 0, g))
    cmp_vt_spec = pl.BlockSpec((None, HEAD_DIM, ncmp), lambda b, g, i: (b, g, 0))
    kd_spec = pl.BlockSpec((SEQ, LANES), lambda b, g, i: (b, g))
    vt_spec = pl.BlockSpec((None, HEAD_DIM, SEQ), lambda b, g, i: (b, N_KV + g, 0))
    const = lambda a: pl.BlockSpec(a.shape, lambda b, g, i: (0, 0))
    gd_spec = pl.BlockSpec((TQ, GD_W), lambda b, g, i: (b * nq + i, 0))
    return pl.pallas_call(
        _prompt_attn_kernel,
        grid=(BATCH, N_KV, nq),
        in_specs=[q_spec, gd_spec, cmp_spec, cmp_vt_spec, const(mt), kd_spec, const(oh), vt_spec, kd_spec, vt_spec],
        out_specs=q_spec,
        out_shape=jax.ShapeDtypeStruct((BATCH * SEQ, ATTN_W), bf16),
        scratch_shapes=[pltpu.VMEM((Q_PER_KV * TQ, 2 * LANES), bf16), pltpu.VMEM((1, Q_PER_KV * TQ), f32),
                        pltpu.VMEM((VT_ROWS, Q_PER_KV * TQ), f32), pltpu.VMEM((GD_W, TQ), f32)],
        compiler_params=_cparams(("parallel", "parallel", "parallel")),
        name="prompt_attention",
    )(q, gd, kd_c, vt_c, mt, kd_s, oh, kv_s_t, kd_w, kv_w_t)


SSD_Q = 256
HALF_INNER = D_INNER // N_SSM_GROUPS
BC_W = N_SSM_GROUPS * D_STATE


def _mamba_consts(conv_w, conv_b, dt_bias, a_log, d_skip, ssm_norm):
    pad = lambda v: jnp.zeros((1, GD_W), f32).at[0, DT_LANE0:DT_LANE0 + N_SSM_HEADS].set(v)
    e16 = np.zeros((GD_W, D_INNER), np.float32)
    for h in range(N_SSM_HEADS):
        e16[DT_LANE0 + h, h * SSM_HEAD_DIM:(h + 1) * SSM_HEAD_DIM] = 1.0
    tri = np.tril(np.ones((SSD_Q, SSD_Q), np.float32))
    return (conv_w, conv_b[None], pad(dt_bias), pad(a_log), jnp.asarray(e16, bf16),
            jnp.repeat(d_skip, SSM_HEAD_DIM)[None], ssm_norm[None], jnp.asarray(tri, bf16))


def _dt_and_decay(gd, dtb_ref, alog_ref):
    lane = lax.broadcasted_iota(jnp.int32, gd.shape, 1)
    live = (lane >= DT_LANE0) & (lane < DT_LANE0 + N_SSM_HEADS)
    dt = jnp.where(live, _softplus(gd + dtb_ref[...]), 0.0)
    return dt, dt * (-jnp.exp(alog_ref[...]))


def _gated_group_norm(y, z, nw_ref):
    y = y * _silu(z)
    outs = []
    for g in range(N_SSM_GROUPS):
        yg = y[:, g * HALF_INNER:(g + 1) * HALF_INNER]
        outs.append(yg * lax.rsqrt(jnp.mean(yg * yg, axis=-1, keepdims=True) + NORM_EPS))
    return jnp.concatenate(outs, axis=1) * nw_ref[...]


def _mamba_prompt_kernel(z_ref, xbc_ref, gd_ref, cw_ref, cb_ref, dtb_ref, alog_ref, e16_ref, dsk_ref, nw_ref, tri_ref,
                         y_ref, st_ref, xpad_ref, state_ref):
    c = pl.program_id(1)
    nq = SSD_Q

    @pl.when(c == 0)
    def _():
        state_ref[...] = jnp.zeros(state_ref.shape, f32)
        xpad_ref[0:8, :] = jnp.zeros((8, CONV_DIM), f32)

    xpad_ref[8:8 + nq, :] = xbc_ref[...]
    conv = cb_ref[...]
    for w in range(CONV_W):
        conv = conv + xpad_ref[8 - (CONV_W - 1) + w:8 - (CONV_W - 1) + w + nq, :] * cw_ref[w:w + 1, :]
    xpad_ref[0:8, :] = xpad_ref[nq:nq + 8, :]
    act = _silu(conv)
    xs, bm, cm = act[:, :D_INNER], act[:, D_INNER:D_INNER + BC_W], act[:, D_INNER + BC_W:]

    dt, a = _dt_and_decay(gd_ref[...], dtb_ref, alog_ref)
    a_cs = sum(jnp.dot(tri_ref[...], t, preferred_element_type=f32) for t in _split3(a))
    ea = jnp.exp(a_cs)
    te = jnp.exp(a_cs[nq - 1:nq, :] - a_cs)
    e16 = e16_ref[...]
    dt_x, ea_x, te_x = _dot_exact(dt, e16), _dot_exact(ea, e16), _dot_exact(te, e16)
    xdt = xs * dt_x
    xw = (xdt * te_x).astype(bf16)
    a_cst = a_cs.T
    tril = lax.broadcasted_iota(jnp.int32, (nq, nq), 1) <= lax.broadcasted_iota(jnp.int32, (nq, nq), 0)

    ys = []
    for g in range(N_SSM_GROUPS):
        bg = bm[:, g * D_STATE:(g + 1) * D_STATE]
        cgb = cm[:, g * D_STATE:(g + 1) * D_STATE].astype(bf16)
        cb = _nt(cgb, bg.astype(bf16))
        st = state_ref[g]
        yoff = jnp.dot(cgb, st.astype(bf16), preferred_element_type=f32)
        gl = slice(g * HALF_INNER, (g + 1) * HALF_INNER)
        state_ref[g] = ea_x[nq - 1:nq, gl] * st + jnp.dot(bg.T.astype(bf16), xw[:, gl], preferred_element_type=f32)
        for hp in range(HALF_INNER // LANES):
            pl_ = slice(g * HALF_INNER + hp * LANES, g * HALF_INNER + (hp + 1) * LANES)
            xp = xdt[:, pl_]
            yp = ea_x[:, pl_] * yoff[:, hp * LANES:(hp + 1) * LANES] + dsk_ref[:, pl_] * xs[:, pl_]
            for e in range(2):
                lane_h = DT_LANE0 + g * (N_SSM_HEADS // N_SSM_GROUPS) + hp * 2 + e
                seg = a_cs[:, lane_h:lane_h + 1] - a_cst[lane_h:lane_h + 1, :]
                wgt = (cb * jnp.exp(jnp.where(tril, seg, NEG))).astype(bf16)
                xz = jnp.where(_half_mask((nq, LANES), e), xp, 0.0).astype(bf16)
                yp = yp + jnp.dot(wgt, xz, preferred_element_type=f32)
            ys.append(yp)
    y = jnp.concatenate(ys, axis=1)
    y_ref[...] = _gated_group_norm(y, z_ref[...], nw_ref).astype(bf16)

    @pl.when(c == pl.num_programs(1) - 1)
    def _():
        st_ref[...] = state_ref[...]


def _mamba_prompt(z, xbc, gd, mc):
    nc = SEQ // SSD_Q
    row = lambda w: pl.BlockSpec((SSD_Q, w), lambda b, c: (b * nc + c, 0))
    y, st = pl.pallas_call(
        _mamba_prompt_kernel,
        grid=(BATCH, nc),
        in_specs=[row(D_INNER), row(CONV_DIM), row(GD_W)] + [_const_spec(a, 2) for a in mc],
        out_specs=[row(D_INNER), pl.BlockSpec((None, N_SSM_GROUPS, D_STATE, HALF_INNER), lambda b, c: (b, 0, 0, 0))],
        out_shape=[jax.ShapeDtypeStruct((BATCH * SEQ, D_INNER), bf16),
                   jax.ShapeDtypeStruct((BATCH, N_SSM_GROUPS, D_STATE, HALF_INNER), f32)],
        scratch_shapes=[pltpu.VMEM((SSD_Q + 8, CONV_DIM), f32), pltpu.VMEM((N_SSM_GROUPS, D_STATE, HALF_INNER), f32)],
        compiler_params=_cparams(("parallel", "arbitrary")),
        name="mamba_prompt",
    )(z, xbc, gd, *mc)
    hpg = N_SSM_HEADS // N_SSM_GROUPS
    st = st.reshape(BATCH, N_SSM_GROUPS, D_STATE, hpg, SSM_HEAD_DIM).transpose(0, 1, 3, 4, 2)
    return y, st.reshape(BATCH, N_SSM_HEADS, SSM_HEAD_DIM, D_STATE)


def _mamba_sample_kernel(z_ref, xbc_ref, gd_ref, sc_ref, s_ref, cw_ref, cb_ref, dtb_ref, alog_ref, e16_ref, dsk_ref,
                         nw_ref, y_ref, so_ref):
    conv = cb_ref[...] + xbc_ref[...] * cw_ref[CONV_W - 1:CONV_W, :]
    for w in range(CONV_W - 1):
        conv = conv + sc_ref[w:w + 1, :] * cw_ref[w:w + 1, :]
    act = _silu(conv)
    xs, bm, cm = act[:, :D_INNER], act[:, D_INNER:D_INNER + BC_W], act[:, D_INNER + BC_W:]
    dt, a = _dt_and_decay(jnp.broadcast_to(gd_ref[...], (8, GD_W)), dtb_ref, alog_ref)
    e16 = e16_ref[...]
    dt_x = _dot_exact(dt, e16)[0:1]
    da_x = _dot_exact(jnp.exp(a), e16)[0:1]
    rows = N_SSM_HEADS * SSM_HEAD_DIM
    xcol = jnp.broadcast_to(xs * dt_x, (D_STATE, rows)).T
    acol = jnp.broadcast_to(da_x, (D_STATE, rows)).T
    rowi = lax.broadcasted_iota(jnp.int32, (rows, D_STATE), 0)
    bfull = jnp.where(rowi < HALF_INNER, bm[:, :D_STATE], bm[:, D_STATE:])
    snew = s_ref[...] * acol + xcol * bfull
    so_ref[...] = snew
    r8 = lax.broadcasted_iota(jnp.int32, (8, D_STATE), 0)
    c8 = jnp.where(r8 == 0, cm[:, :D_STATE], jnp.where(r8 == 1, cm[:, D_STATE:], 0.0)).astype(bf16)
    yy = _nt(c8, snew.astype(bf16))
    lane = lax.broadcasted_iota(jnp.int32, (1, rows), 1)
    y = jnp.where(lane < HALF_INNER, yy[0:1], yy[1:2]) + dsk_ref[...] * xs
    y_ref[...] = _gated_group_norm(y, z_ref[...], nw_ref).astype(bf16)


def _mamba_sample(z, xbc, gd, state_conv, state_ssm, mc):
    nb = z.shape[0]
    mc = mc[:7]
    rows = N_SSM_HEADS * SSM_HEAD_DIM
    one = lambda w: pl.BlockSpec((None, 1, w), lambda b: (b, 0, 0))
    s_spec = pl.BlockSpec((None, rows, D_STATE), lambda b: (b, 0, 0))
    y, so = pl.pallas_call(
        _mamba_sample_kernel,
        grid=(nb,),
        in_specs=[one(D_INNER), one(CONV_DIM), one(GD_W),
                  pl.BlockSpec((None, CONV_W - 1, CONV_DIM), lambda b: (b, 0, 0)), s_spec]
                 + [_const_spec(a, 1) for a in mc],
        out_specs=[one(D_INNER), s_spec],
        out_shape=[jax.ShapeDtypeStruct((nb, 1, D_INNER), bf16), jax.ShapeDtypeStruct((nb, rows, D_STATE), f32)],
        compiler_params=_cparams(("parallel",)),
        name="mamba_sample",
    )(z[:, None], xbc[:, None], gd[:, None], state_conv, state_ssm.reshape(nb, rows, D_STATE), *mc)
    return y[:, 0], so.reshape(nb, N_SSM_HEADS, SSM_HEAD_DIM, D_STATE)


N_CMP_S = PAST_LEN // CMP_STRIDE - 1
N_SLC_S = PAST_LEN // SEL_BLOCK + 1
N_SLC_PAD = 2 * LANES
HALF_PAGE = PAGE_SIZE // SEL_BLOCK
GONE = -3.0e38


def _cmp_attn_sample_kernel(qz_ref, kd_ref, vl_ref, vh_ref, mts_ref, o_ref, imp_ref):
    qz = (qz_ref[...] * ATTN_SCALE).astype(bf16)
    ncmp = kd_ref.shape[0]
    nidx = lax.broadcasted_iota(jnp.int32, (N_HEADS, ncmp), 1)
    rowh = lax.broadcasted_iota(jnp.int32, (N_HEADS, ncmp), 0)
    mask = (nidx < N_CMP_S) & (nidx * CMP_STRIDE + (CMP_BLOCK - 1) <= PAST_LEN)
    rowo = lax.broadcasted_iota(jnp.int32, (N_HEADS, LANES), 0)
    r8 = lax.broadcasted_iota(jnp.int32, (8, N_SLC_PAD), 0)
    o_acc = jnp.zeros((N_HEADS, LANES), f32)
    imp = jnp.zeros((8, N_SLC_PAD), f32)
    for g in range(N_KV):
        gl = slice(g * LANES, (g + 1) * LANES)
        s = jnp.where(mask, _nt(qz, kd_ref[:, gl]), NEG)
        m = jnp.max(s, axis=1, keepdims=True)
        ex = jnp.where(mask, jnp.exp(s - m), 0.0)
        p = ex * (1.0 / jnp.maximum(jnp.sum(ex, axis=1, keepdims=True), 1e-30))
        p = jnp.where(rowh // Q_PER_KV == g, p, 0.0)
        pb = p.astype(bf16)
        o_l = jnp.dot(pb, vl_ref[:, gl], preferred_element_type=f32)
        o_h = jnp.dot(pb, vh_ref[:, gl], preferred_element_type=f32)
        o_acc = o_acc + jnp.where(rowo % 2 == 0, o_l, o_h)
        hi, lo = _split2(p)
        imp_h = jnp.dot(hi, mts_ref[...], preferred_element_type=f32) + jnp.dot(lo, mts_ref[...], preferred_element_type=f32)
        imp = imp + jnp.where(r8 == g, jnp.sum(imp_h, axis=0, keepdims=True), 0.0)
    o_ref[...] = o_acc
    imp_ref[...] = imp


def _topk_sample_kernel(imp_ref, idx_ref):
    rows = imp_ref.shape[0]
    j = lax.broadcasted_iota(jnp.int32, (rows, N_SLC_PAD), 1)
    jt = PAST_LEN // SEL_BLOCK
    imp = imp_ref[...]
    imp = jnp.where((j == 0) | (j == jt) | (j == jt - 1), SELECT_FORCE, imp)
    imp = jnp.where(j > jt, -SELECT_FORCE, imp)
    imp = jnp.where(j >= N_SLC_S, NEG, imp)
    jf = j.astype(f32)
    lane = lax.broadcasted_iota(jnp.int32, (rows, LANES), 1)
    picked = jnp.zeros((rows, LANES), f32)
    for k in range(SEL_TOPN):
        m = jnp.max(imp, axis=1, keepdims=True)
        ix = jnp.min(jnp.where(imp == m, jf, float(N_SLC_PAD)), axis=1, keepdims=True)
        picked = jnp.where(lane == k, ix, picked)
        imp = jnp.where(jf == ix, GONE, imp)
    idx_ref[...] = picked.astype(jnp.int32)


def _cmp_attn_sample(qz, kd, vl, vh):
    nb, ncmp = kd.shape[0], kd.shape[1]
    mts = np.zeros((ncmp, N_SLC_PAD), np.float32)
    mts[:, :N_SLC_S] = _cmp_to_sel_matrix_t(ncmp, N_SLC_S, N_CMP_S).T
    mts = jnp.asarray(mts, bf16)
    kv_spec = pl.BlockSpec((None, ncmp, KV_W), lambda b: (b, 0, 0))
    o16, imp = pl.pallas_call(
        _cmp_attn_sample_kernel,
        grid=(nb,),
        in_specs=[pl.BlockSpec((None, N_HEADS, LANES), lambda b: (b, 0, 0)), kv_spec, kv_spec, kv_spec,
                  pl.BlockSpec(mts.shape, lambda b: (0, 0))],
        out_specs=[pl.BlockSpec((None, N_HEADS, LANES), lambda b: (b, 0, 0)),
                   pl.BlockSpec((None, 8, N_SLC_PAD), lambda b: (b, 0, 0))],
        out_shape=[jax.ShapeDtypeStruct((nb, N_HEADS, LANES), f32), jax.ShapeDtypeStruct((nb, 8, N_SLC_PAD), f32)],
        compiler_params=_cparams(("parallel",)),
        name="cmp_attn_sample",
    )(qz, kd, vl, vh, mts)
    idx = pl.pallas_call(
        _topk_sample_kernel,
        out_shape=jax.ShapeDtypeStruct((nb * 8, LANES), jnp.int32),
        name="topk_sample",
    )(imp.reshape(nb * 8, N_SLC_PAD))
    return o16, idx.reshape(nb, 8, LANES)


def _one_query_softmax(s, s_new, vt, v_new):
    m = jnp.maximum(jnp.max(s, axis=1, keepdims=True), s_new)
    p = jnp.exp(s - m)
    p_new = jnp.exp(s_new - m)
    den = jnp.sum(p, axis=1, keepdims=True) + p_new
    num = _nt(p.astype(bf16), vt) + p_new * v_new
    return num * (1.0 / den)


def _selwin_sample_kernel(idx_ref, pg_ref, *refs):
    del pg_ref
    kv = refs[:SEL_TOPN]
    q_ref, ksn_ref, win_ref, kwn_ref, os_ref, ow_ref = refs[SEL_TOPN:]
    b, g = pl.program_id(0), pl.program_id(1)
    q8 = q_ref[...] * ATTN_SCALE
    qb = q8.astype(bf16)
    kcat = jnp.concatenate([r[0] for r in kv], axis=1).astype(bf16)
    vcat = jnp.concatenate([r[1] for r in kv], axis=1).astype(bf16)
    n = SEL_TOPN * PAGE_SIZE
    lane = lax.broadcasted_iota(jnp.int32, (8, n), 1)
    slot = lane // PAGE_SIZE
    blk = jnp.zeros((8, n), jnp.int32)
    for k in range(SEL_TOPN):
        blk = jnp.where(slot == k, idx_ref[(b * N_KV + g) * SEL_TOPN + k], blk)
    live = (blk < N_SLC_S - 1) & ((lane % PAGE_SIZE) // SEL_BLOCK == blk % HALF_PAGE)
    s = jnp.where(live, jnp.dot(qb, kcat, preferred_element_type=f32), NEG)
    k_new, v_new = ksn_ref[pl.ds(g, 1), :], ksn_ref[pl.ds(N_KV + g, 1), :]
    s_new = jnp.sum(q8 * k_new, axis=1, keepdims=True)
    os_ref[...] = _one_query_softmax(s, s_new, vcat, v_new)
    nbuf = win_ref.shape[-1]
    i = lax.broadcasted_iota(jnp.int32, (8, nbuf), 1)
    s = jnp.where(nbuf - i < WINDOW, jnp.dot(qb, win_ref[0].astype(bf16), preferred_element_type=f32), NEG)
    k_new, v_new = kwn_ref[pl.ds(g, 1), :], kwn_ref[pl.ds(N_KV + g, 1), :]
    s_new = jnp.sum(q8 * k_new, axis=1, keepdims=True)
    ow_ref[...] = _one_query_softmax(s, s_new, win_ref[1].astype(bf16), v_new)


def _selwin_sample(idx, page_table, cache_s, q_s, ks_new, win_buf, kw_new):
    nb = q_s.shape[0]
    nbuf = win_buf.shape[1]
    pages = cache_s.transpose(0, 2, 3, 4, 1)
    wb = win_buf.transpose(0, 2, 3, 4, 1)
    qg = q_s.reshape(nb, N_KV, Q_PER_KV, HEAD_DIM)
    q8 = jnp.concatenate([qg, jnp.zeros_like(qg)], axis=2)

    logical = jnp.minimum(idx, N_SLC_S - 2) // HALF_PAGE
    page_of = jnp.take_along_axis(page_table, logical.reshape(nb, N_KV * SEL_TOPN), axis=1)

    def blk_map(b, g, idx_ref, pg_ref, k):
        return (pg_ref[(b * N_KV + g) * SEL_TOPN + k], 0, g, 0, 0)

    blk_specs = [pl.BlockSpec((None, 2, None, HEAD_DIM, PAGE_SIZE), functools.partial(blk_map, k=k))
                 for k in range(SEL_TOPN)]
    new_spec = pl.BlockSpec((None, 2 * N_KV, HEAD_DIM), lambda b, g, *_: (b, 0, 0))
    win_spec = pl.BlockSpec((None, 2, None, HEAD_DIM, nbuf), lambda b, g, *_: (b, 0, g, 0, 0))
    o_spec = pl.BlockSpec((None, None, 8, HEAD_DIM), lambda b, g, *_: (b, g, 0, 0))
    grid_spec = pltpu.PrefetchScalarGridSpec(
        num_scalar_prefetch=2,
        grid=(nb, N_KV),
        in_specs=blk_specs + [o_spec, new_spec, win_spec, new_spec],
        out_specs=[o_spec, o_spec])
    o_shape = jax.ShapeDtypeStruct((nb, N_KV, 8, HEAD_DIM), f32)
    ks3, kw3 = ks_new.reshape(nb, 2 * N_KV, HEAD_DIM), kw_new.reshape(nb, 2 * N_KV, HEAD_DIM)
    os8, ow8 = pl.pallas_call(
        _selwin_sample_kernel,
        grid_spec=grid_spec,
        out_shape=[o_shape, o_shape],
        compiler_params=_cparams(("arbitrary", "arbitrary")),
        name="selwin_sample",
    )(idx.reshape(-1), page_of.reshape(-1), *([pages] * SEL_TOPN), q8, ks3, wb, kw3)
    return os8[:, :, :Q_PER_KV].reshape(nb, ATTN_W), ow8[:, :, :Q_PER_KV].reshape(nb, ATTN_W)


def _sample_query_layouts(q):
    nb = q.shape[0]
    q16 = q.reshape(nb, N_HEADS, HEAD_DIM)
    z = jnp.zeros_like(q16)
    lo, hi = jnp.concatenate([q16, z], axis=-1), jnp.concatenate([z, q16], axis=-1)
    h = jnp.arange(N_HEADS)[None, :, None]
    return jnp.where(h % 2 == 0, lo, hi)


def _unpad_heads_by_parity(o16):
    nb = o16.shape[0]
    o = o16.reshape(nb, N_HEADS // 2, 2, 2, HEAD_DIM)
    return jnp.stack([o[:, :, 0, 0], o[:, :, 1, 1]], axis=2).reshape(nb, ATTN_W)


def _finish(x, ocmp, osel, owin, gd, ssd_y, w_out, ln_mlp, w_up, w_down, ln_final, tm, tm_mlp, tf):
    x1 = _outproj(x, ocmp, osel, owin, gd, ssd_y, _gate_expand(), w_out, tm)
    return _mlp(x1, ln_mlp, w_up, w_down, ln_final, tm_mlp, tf)


def kernel(x_prompt, x_sample, cache_kv_cmp, cache_kv_sel, state_kv_win, state_conv, state_ssm, page_table, ln_mix, w_in, cmp_pe, cmp_w1, cmp_b1, cmp_w2, cmp_b2, conv_w, conv_b, dt_bias, a_log, d_skip, ssm_norm, w_out, ln_mlp, w_up, w_down, ln_final):
    nb = x_sample.shape[0]
    w_in_p = _prep_w_in(w_in[0])
    w_out_b, w_up_b, w_down_b = w_out[0].astype(bf16), w_up[0].astype(bf16), w_down[0].astype(bf16)
    lnw, lnm, lnf = ln_mix[0][None], ln_mlp[0][None], ln_final[None]
    cw = _compress_weights(cmp_pe[0], cmp_w1[0], cmp_b1[0], cmp_w2[0], cmp_b2[0])
    mc = _mamba_consts(conv_w[0], conv_b[0], dt_bias[0], a_log[0], d_skip[0], ssm_norm[0])
    kv_shape = (2, N_KV, HEAD_DIM)

    xp = x_prompt.reshape(BATCH * SEQ, D_MODEL)
    q, z, xbc, gd, kct, kst, kwt, kds, kdw = _inproj_prompt(xp, lnw, w_in_p, 256)
    kd, _, _, vt = _compress_prompt(kct, cw)
    attn = _prompt_attention(q, gd, kd, vt, kds, kst, kdw, kwt)
    ssd_y, ssm_p = _mamba_prompt(z, xbc, gd, mc)
    y_prompt = _mlp(_outproj_mixed(xp, attn, ssd_y, w_out_b, 512), lnm, w_up_b, w_down_b, lnf, 512, 1024)

    xs = x_sample.reshape(nb, D_MODEL)
    q_s, z_s, xbc_s, gd_s, kc_s, ks_s, kw_s = _inproj(xs, lnw, w_in_p, nb)
    kd_s, vl_s, vh_s = _compress_sample(cache_kv_cmp[0], page_table, cw)
    ocmp16, idx8 = _cmp_attn_sample(_sample_query_layouts(q_s), kd_s, vl_s, vh_s)
    idx = idx8[:, :N_KV, :SEL_TOPN]
    osel_s, owin_s = _selwin_sample(idx, page_table, cache_kv_sel[0], q_s, ks_s, state_kv_win[0], kw_s)
    ssd_y_s, ssm_s = _mamba_sample(z_s, xbc_s, gd_s, state_conv[0], state_ssm[0], mc)
    y_sample = _finish(xs, _unpad_heads_by_parity(ocmp16), osel_s, owin_s, gd_s, ssd_y_s,
                       w_out_b, lnm, w_up_b, w_down_b, lnf, nb, nb, 1024)

    def token_major(t):
        return t.reshape(BATCH, *kv_shape, t.shape[-1]).transpose(0, 4, 1, 2, 3)[None]

    conv_p = xbc.reshape(BATCH, SEQ, CONV_DIM)[:, -(CONV_W - 1):]
    kv_win_s = jnp.concatenate([state_kv_win[0], kw_s.reshape(nb, 1, *kv_shape)], axis=1)[:, -WINDOW:]
    conv_s = jnp.concatenate([state_conv[0], xbc_s[:, None]], axis=1)[:, -(CONV_W - 1):]
    return (y_prompt.reshape(BATCH, SEQ, D_MODEL), y_sample.reshape(nb, 1, D_MODEL),
            token_major(kct), token_major(kst), token_major(kwt[:, :, -min(WINDOW, SEQ):]),
            conv_p[None], ssm_p[None],
            kc_s.reshape(1, nb, 1, *kv_shape), ks_s.reshape(1, nb, 1, *kv_shape), kv_win_s[None],
            conv_s[None], ssm_s[None])
```

```python
import functools
import math

import jax
import jax.numpy as jnp
import numpy as np
from jax import lax
from jax.experimental import pallas as pl
from jax.experimental.pallas import tpu as pltpu

f32 = jnp.float32
bf16 = jnp.bfloat16

D_MODEL = 2048
BATCH = 2
SEQ = 4096
DEC_BATCH = 32
PAST_LEN = 8192
PAGE_SIZE = 128
HEAD_DIM = 64
N_HEADS = 16
N_KV = 4
Q_PER_KV = 4
ATTN_W = 1024
KV_W = 512
CMP_BLOCK = 32
CMP_STRIDE = 16
CMP_HIDDEN = 128
SEL_BLOCK = 64
SEL_TOPN = 16
WINDOW = 512
SELECT_FORCE = 1.0e4
ATTN_SCALE = HEAD_DIM ** -0.5
D_INNER = 1024
SSM_HEAD_DIM = 64
N_SSM_HEADS = 16
N_SSM_GROUPS = 2
D_STATE = 128
CONV_W = 4
CONV_DIM = D_INNER + 2 * N_SSM_GROUPS * D_STATE
D_FF = 4 * D_MODEL
IN_SPLITS = (ATTN_W, KV_W, KV_W, KV_W, 3 * N_HEADS, D_INNER, CONV_DIM, N_SSM_HEADS)
NORM_EPS = 1e-5

LANES = 128
GD_W = LANES
DT_LANE0 = 3 * N_HEADS
N_PAGES = PAST_LEN // PAGE_SIZE
SEG_PER_PAGE = PAGE_SIZE // CMP_STRIDE
NEG = -1.0e30
SEL_NEG = -1.0e9
VMEM_LIMIT = 56 * 1024 * 1024


def _cparams(sem):
    return pltpu.CompilerParams(dimension_semantics=sem, vmem_limit_bytes=VMEM_LIMIT)


def _nt(a, b):
    return lax.dot_general(a, b, (((1,), (1,)), ((), ())), preferred_element_type=f32)


def _split2(x):
    hi = x.astype(bf16)
    lo = (x - hi.astype(f32)).astype(bf16)
    return hi, lo


def _split3(x):
    hi = x.astype(bf16)
    r = x - hi.astype(f32)
    mid = r.astype(bf16)
    lo = (r - mid.astype(f32)).astype(bf16)
    return hi, mid, lo


def _dot_exact(x, w):
    return sum(jnp.dot(t, w, preferred_element_type=f32) for t in _split3(x))


def _silu(x):
    return x * (1.0 / (1.0 + jnp.exp(-x)))


def _sigmoid(x):
    return 1.0 / (1.0 + jnp.exp(-x))


def _softplus(x):
    return jnp.maximum(x, 0.0) + jnp.log1p(jnp.exp(-jnp.abs(x)))


IN_OUT_WIDTHS = (ATTN_W, D_INNER, CONV_DIM, GD_W, KV_W, KV_W, KV_W)
KD_W = N_KV * LANES
IN_COLS_ROWMAJOR = sum(IN_OUT_WIDTHS)


def _normed(x_ref, lnw_ref):
    x = x_ref[...]
    return (x * lax.rsqrt(jnp.mean(x * x, axis=-1, keepdims=True) + NORM_EPS) * lnw_ref[...]).astype(bf16)


def _inproj_kernel(x_ref, lnw_ref, w_ref, *out_refs):
    h = _normed(x_ref, lnw_ref)
    off = 0
    for ref in out_refs:
        n = ref.shape[-1]
        ref[...] = jnp.dot(h, w_ref[:, off:off + n], preferred_element_type=f32)
        off += n


def _inproj_prompt_kernel(x_ref, lnw_ref, w_ref, q_ref, z_ref, xbc_ref, gd_ref,
                          kct_ref, kst_ref, kwt_ref, kds_ref, kdw_ref, t_ref):
    h = _normed(x_ref, lnw_ref)
    off = 0

    def proj(n):
        nonlocal off
        y = jnp.dot(h, w_ref[:, off:off + n], preferred_element_type=f32)
        off += n
        return y

    q_ref[...] = (proj(ATTN_W) * (ATTN_SCALE * LOG2E)).astype(bf16)
    for ref in (z_ref, xbc_ref, gd_ref):
        ref[...] = proj(ref.shape[-1])
    for t_out in (kct_ref, kst_ref, kwt_ref):
        t_ref[...] = proj(KV_W)
        t_out[...] = t_ref[...].T
    kds_ref[...] = proj(KD_W).astype(bf16)
    kdw_ref[...] = proj(KD_W).astype(bf16)


def _inproj(x, lnw, w_perm, tm):
    m = x.shape[0]
    return pl.pallas_call(
        _inproj_kernel,
        grid=(m // tm,),
        in_specs=[pl.BlockSpec((tm, D_MODEL), lambda i: (i, 0)),
                  pl.BlockSpec((1, D_MODEL), lambda i: (0, 0)),
                  pl.BlockSpec((D_MODEL, IN_COLS_ROWMAJOR), lambda i: (0, 0), pipeline_mode=pl.Buffered(1))],
        out_specs=[pl.BlockSpec((tm, n), lambda i: (i, 0)) for n in IN_OUT_WIDTHS],
        out_shape=[jax.ShapeDtypeStruct((m, n), f32) for n in IN_OUT_WIDTHS],
        compiler_params=_cparams(("parallel",)),
        name="inproj",
    )(x, lnw, w_perm)


def _inproj_prompt(x, lnw, w_perm, tm):
    nq = SEQ // tm
    row = lambda n: pl.BlockSpec((tm, n), lambda i: (i, 0))
    tmin = pl.BlockSpec((None, KV_W, tm), lambda i: (i // nq, 0, i % nq))
    rm = lambda n, dt=f32: jax.ShapeDtypeStruct((BATCH * SEQ, n), dt)
    tshape = jax.ShapeDtypeStruct((BATCH, KV_W, SEQ), f32)
    return pl.pallas_call(
        _inproj_prompt_kernel,
        grid=(BATCH * nq,),
        in_specs=[pl.BlockSpec((tm, D_MODEL), lambda i: (i, 0)),
                  pl.BlockSpec((1, D_MODEL), lambda i: (0, 0)),
                  pl.BlockSpec(w_perm.shape, lambda i: (0, 0), pipeline_mode=pl.Buffered(1))],
        out_specs=[row(ATTN_W), row(D_INNER), row(CONV_DIM), row(GD_W), tmin, tmin, tmin, row(KD_W), row(KD_W)],
        out_shape=[rm(ATTN_W, bf16), rm(D_INNER), rm(CONV_DIM), rm(GD_W), tshape, tshape, tshape,
                   rm(KD_W, bf16), rm(KD_W, bf16)],
        scratch_shapes=[pltpu.VMEM((tm, KV_W), f32)],
        compiler_params=_cparams(("parallel",)),
        name="inproj_prompt",
    )(x, lnw, w_perm)


def _prep_w_in(w_in):
    parts, off = [], 0
    for width in IN_SPLITS:
        parts.append(w_in[:, off:off + width])
        off += width
    q, kc, ks, kw, g, z, xbc, dt = parts
    gd = jnp.concatenate([g, dt, jnp.zeros((D_MODEL, GD_W - 4 * N_HEADS), w_in.dtype)], axis=1)

    def dup_keys(kv):
        k = kv[:, :KV_W // 2].reshape(D_MODEL, N_KV, 1, HEAD_DIM)
        return jnp.broadcast_to(k, (D_MODEL, N_KV, 2, HEAD_DIM)).reshape(D_MODEL, KD_W)

    return jnp.concatenate([q, z, xbc, gd, kc, ks, kw, dup_keys(ks), dup_keys(kw)], axis=1).astype(bf16)


def _outproj_kernel(x_ref, oc_ref, os_ref, ow_ref, gd_ref, y_ref, eg_ref, w_ref, o_ref):
    gates = _sigmoid(gd_ref[...])
    hi, lo = _split2(gates)
    attn = None
    for c, br in enumerate((oc_ref, os_ref, ow_ref)):
        ge = (jnp.dot(hi, eg_ref[c], preferred_element_type=f32)
              + jnp.dot(lo, eg_ref[c], preferred_element_type=f32))
        term = ge * br[...]
        attn = term if attn is None else attn + term
    mix = (jnp.dot(attn.astype(bf16), w_ref[:ATTN_W, :], preferred_element_type=f32)
           + jnp.dot(y_ref[...], w_ref[ATTN_W:, :], preferred_element_type=f32))
    o_ref[...] = x_ref[...] + mix


def _outproj_mixed_kernel(x_ref, a_ref, y_ref, w_ref, o_ref):
    mix = (jnp.dot(a_ref[...], w_ref[:ATTN_W, :], preferred_element_type=f32)
           + jnp.dot(y_ref[...], w_ref[ATTN_W:, :], preferred_element_type=f32))
    o_ref[...] = x_ref[...] + mix


def _outproj_mixed(x, attn, ssd_y, w_out, tm):
    m = x.shape[0]
    row = lambda w: pl.BlockSpec((tm, w), lambda i: (i, 0))
    return pl.pallas_call(
        _outproj_mixed_kernel,
        grid=(m // tm,),
        in_specs=[row(D_MODEL), row(ATTN_W), row(D_INNER),
                  pl.BlockSpec((ATTN_W + D_INNER, D_MODEL), lambda i: (0, 0))],
        out_specs=row(D_MODEL),
        out_shape=jax.ShapeDtypeStruct((m, D_MODEL), f32),
        compiler_params=_cparams(("parallel",)),
        name="outproj_mixed",
    )(x, attn, ssd_y, w_out)


def _outproj(x, ocmp, osel, owin, gd, ssd_y, eg, w_out, tm):
    m = x.shape[0]
    row = lambda w: pl.BlockSpec((tm, w), lambda i: (i, 0))
    return pl.pallas_call(
        _outproj_kernel,
        grid=(m // tm,),
        in_specs=[row(D_MODEL), row(ATTN_W), row(ATTN_W), row(ATTN_W), row(GD_W), row(D_INNER),
                  pl.BlockSpec((3, GD_W, ATTN_W), lambda i: (0, 0, 0)),
                  pl.BlockSpec((ATTN_W + D_INNER, D_MODEL), lambda i: (0, 0))],
        out_specs=row(D_MODEL),
        out_shape=jax.ShapeDtypeStruct((m, D_MODEL), f32),
        compiler_params=_cparams(("parallel",)),
        name="outproj",
    )(x, ocmp, osel, owin, gd, ssd_y, eg, w_out)


def _gate_expand():
    eg = np.zeros((3, GD_W, ATTN_W), np.float32)
    for h in range(N_HEADS):
        for c in range(3):
            eg[c, h * 3 + c, h * HEAD_DIM:(h + 1) * HEAD_DIM] = 1.0
    return jnp.asarray(eg, bf16)


def _mlp_kernel(x_ref, ln_ref, wu_ref, wd_ref, lnf_ref, o_ref, h_ref, acc_ref):
    k = pl.program_id(1)

    @pl.when(k == 0)
    def _():
        x = x_ref[...]
        h = x * lax.rsqrt(jnp.mean(x * x, axis=-1, keepdims=True) + NORM_EPS) * ln_ref[...]
        h_ref[...] = h.astype(bf16)
        acc_ref[...] = jnp.zeros_like(acc_ref)

    u = jnp.dot(h_ref[...], wu_ref[...], preferred_element_type=f32)
    u = jnp.square(jnp.maximum(u, 0.0)).astype(bf16)
    acc_ref[...] += jnp.dot(u, wd_ref[...], preferred_element_type=f32)

    @pl.when(k == pl.num_programs(1) - 1)
    def _():
        y = x_ref[...] + acc_ref[...]
        y = y * lax.rsqrt(jnp.mean(y * y, axis=-1, keepdims=True) + NORM_EPS) * lnf_ref[...]
        o_ref[...] = y


def _mlp(x, ln_mlp, w_up, w_down, ln_final, tm, tf):
    m = x.shape[0]
    return pl.pallas_call(
        _mlp_kernel,
        grid=(m // tm, D_FF // tf),
        in_specs=[pl.BlockSpec((tm, D_MODEL), lambda i, k: (i, 0)),
                  pl.BlockSpec((1, D_MODEL), lambda i, k: (0, 0)),
                  pl.BlockSpec((D_MODEL, tf), lambda i, k: (0, k)),
                  pl.BlockSpec((tf, D_MODEL), lambda i, k: (k, 0)),
                  pl.BlockSpec((1, D_MODEL), lambda i, k: (0, 0))],
        out_specs=pl.BlockSpec((tm, D_MODEL), lambda i, k: (i, 0)),
        out_shape=jax.ShapeDtypeStruct((m, D_MODEL), f32),
        scratch_shapes=[pltpu.VMEM((tm, D_MODEL), bf16), pltpu.VMEM((tm, D_MODEL), f32)],
        compiler_params=_cparams(("parallel", "arbitrary")),
        name="mlp",
    )(x, ln_mlp, w_up, w_down, ln_final)


N_SLAB_K = CMP_STRIDE * LANES


def _compress_weights(cmp_pe, cmp_w1, cmp_b1, cmp_w2, cmp_b2):
    eye2 = jnp.eye(2, dtype=f32)
    w1 = cmp_w1.reshape(2, 2, CMP_STRIDE, HEAD_DIM, CMP_HIDDEN)
    w1cat = jnp.einsum('crjdf,ab->cjadrbf', w1, eye2).reshape(2, N_SLAB_K, 4 * CMP_HIDDEN).astype(bf16)
    w1flat = cmp_w1.reshape(2, CMP_BLOCK * HEAD_DIM, CMP_HIDDEN).astype(bf16)
    pe8 = jnp.broadcast_to(cmp_pe.reshape(2, 1, CMP_BLOCK * HEAD_DIM), (2, 8, CMP_BLOCK * HEAD_DIM))
    b1 = cmp_b1.reshape(2, 1, CMP_HIDDEN)
    wk, wv = cmp_w2[0], cmp_w2[1]
    zk = jnp.zeros_like(wk)
    w2k = jnp.einsum('fd,ab,e->afbed', wk, eye2, jnp.ones((2,), f32)).reshape(2 * CMP_HIDDEN, 4 * HEAD_DIM)
    b2k = jnp.tile(cmp_b2[0], 4)[None]
    vl = jnp.einsum('fd,ab,e->afbed', wv, eye2, jnp.array([1.0, 0.0], f32)).reshape(2 * CMP_HIDDEN, 4 * HEAD_DIM)
    vh = jnp.einsum('fd,ab,e->afbed', wv, eye2, jnp.array([0.0, 1.0], f32)).reshape(2 * CMP_HIDDEN, 4 * HEAD_DIM)
    w2v = jnp.concatenate([vl, vh], axis=1)
    zb = jnp.zeros((HEAD_DIM,), f32)
    b2v = jnp.concatenate([cmp_b2[1], zb, cmp_b2[1], zb, zb, cmp_b2[1], zb, cmp_b2[1]])[None]
    del zk
    return pe8, w1flat, b1, w1cat, w2k.astype(bf16), b2k, w2v.astype(bf16), b2v


def _compress_slab(a, c, gp, prm_refs, out_refs):
    pe_ref, w1f_ref, b1_ref, w1_ref, w2k_ref, b2k_ref, w2v_ref, b2v_ref = prm_refs
    kd_ref, vl_ref, vh_ref = out_refs
    pet = jnp.dot(pe_ref[c].astype(bf16), w1f_ref[c], preferred_element_type=f32)[0:1] + b1_ref[c]
    bias = jnp.concatenate([pet, pet], axis=1)
    p = jnp.dot(a, w1_ref[c], preferred_element_type=f32)
    s = p.shape[0]
    hid = p[:, :2 * CMP_HIDDEN] + pltpu.roll(p[:, 2 * CMP_HIDDEN:], s - 1, axis=0)
    hid = _silu(hid + bias).astype(bf16)
    w = 4 * HEAD_DIM
    if c == 0:
        kd_ref[:, gp * w:(gp + 1) * w] = (jnp.dot(hid, w2k_ref[...], preferred_element_type=f32)
                                          + b2k_ref[...]).astype(bf16)
    else:
        o = jnp.dot(hid, w2v_ref[...], preferred_element_type=f32) + b2v_ref[...]
        vl_ref[:, gp * w:(gp + 1) * w] = o[:, :w].astype(bf16)
        vh_ref[:, gp * w:(gp + 1) * w] = o[:, w:].astype(bf16)


def _compress_token_minor(pair, n_pairs, perm_ref, prm_refs, out_refs, o_ref):
    rows = 2 * SEG_PER_PAGE
    for k in range(n_pairs):
        o_ref[k] = _nt(perm_ref[...], pair(k).astype(bf16)).astype(bf16)
    nseg = n_pairs * rows
    for c in range(2):
        for gp in range(2):
            l0 = c * (KV_W // 2) + gp * LANES
            a = jnp.concatenate([o_ref[:, j * rows:(j + 1) * rows, l0:l0 + LANES].reshape(nseg, LANES)
                                 for j in range(CMP_STRIDE)], axis=1)
            _compress_slab(a, c, gp, prm_refs, out_refs)


def _compress_prompt_kernel(x_ref, perm_ref, *refs):
    chunk = 2 * PAGE_SIZE
    _compress_token_minor(lambda k: x_ref[:, k * chunk:(k + 1) * chunk], SEQ // chunk, perm_ref,
                          refs[:8], refs[8:11], refs[12])
    vl_ref, vt_ref = refs[9], refs[11]
    for g in range(N_KV):
        v_pad = vl_ref[:, g * LANES:(g + 1) * LANES].astype(f32)
        vt_ref[g * HEAD_DIM:(g + 1) * HEAD_DIM, :] = v_pad.T[:HEAD_DIM].astype(bf16)


def _seg_perm():
    m = np.zeros((2 * PAGE_SIZE, 2 * PAGE_SIZE), np.float32)
    for pp in range(2):
        for s in range(SEG_PER_PAGE):
            for j in range(CMP_STRIDE):
                m[j * 2 * SEG_PER_PAGE + pp * SEG_PER_PAGE + s, pp * PAGE_SIZE + s * CMP_STRIDE + j] = 1.0
    return jnp.asarray(m, bf16)


def _compress_sample_kernel(pt_ref, *refs):
    del pt_ref
    pages, perm_ref = refs[:N_PAGES], refs[N_PAGES]
    prm_refs, out_refs, o_ref = refs[N_PAGES + 1:N_PAGES + 9], refs[N_PAGES + 9:N_PAGES + 12], refs[-1]
    _compress_token_minor(lambda k: jnp.concatenate([pages[2 * k][...], pages[2 * k + 1][...]], axis=1),
                          N_PAGES // 2, perm_ref, prm_refs, out_refs, o_ref)


def _const_spec(a, n_grid, n_prefetch=0):
    zeros = (0,) * a.ndim
    if n_grid == 1:
        return pl.BlockSpec(a.shape, (lambda b, *_: zeros))
    return pl.BlockSpec(a.shape, (lambda b, c, *_: zeros))


def _compress_prompt(kc_t, cw):
    nseg = SEQ // CMP_STRIDE
    perm = _seg_perm()
    out = jax.ShapeDtypeStruct((BATCH, nseg, KV_W), bf16)
    vt_rows = N_KV * HEAD_DIM
    return pl.pallas_call(
        _compress_prompt_kernel,
        grid=(BATCH,),
        in_specs=[pl.BlockSpec((None, KV_W, SEQ), lambda b: (b, 0, 0)), _const_spec(perm, 1)]
                 + [_const_spec(a, 1) for a in cw],
        out_specs=[pl.BlockSpec((None, nseg, KV_W), lambda b: (b, 0, 0))] * 3
                  + [pl.BlockSpec((None, vt_rows, nseg), lambda b: (b, 0, 0))],
        out_shape=[out] * 3 + [jax.ShapeDtypeStruct((BATCH, vt_rows, nseg), bf16)],
        scratch_shapes=[pltpu.VMEM((SEQ // (2 * PAGE_SIZE), 2 * PAGE_SIZE, KV_W), bf16)],
        compiler_params=_cparams(("parallel",)),
        name="compress_prompt",
    )(kc_t, perm, *cw)


def _compress_sample(cache, page_table, cw):
    nseg = PAST_LEN // CMP_STRIDE
    x = _token_minor(cache)
    perm = _seg_perm()
    out = jax.ShapeDtypeStruct((DEC_BATCH, nseg, KV_W), bf16)
    page_specs = [pl.BlockSpec((None, KV_W, PAGE_SIZE), functools.partial(lambda b, pt, p: (pt[b * N_PAGES + p], 0, 0), p=p))
                  for p in range(N_PAGES)]
    grid_spec = pltpu.PrefetchScalarGridSpec(
        num_scalar_prefetch=1,
        grid=(DEC_BATCH,),
        in_specs=page_specs + [_const_spec(perm, 1)] + [_const_spec(a, 1) for a in cw],
        out_specs=[pl.BlockSpec((None, nseg, KV_W), lambda b, pt: (b, 0, 0))] * 3,
        scratch_shapes=[pltpu.VMEM((N_PAGES // 2, 2 * PAGE_SIZE, KV_W), bf16)])
    return pl.pallas_call(
        _compress_sample_kernel,
        grid_spec=grid_spec,
        out_shape=[out] * 3,
        compiler_params=_cparams(("arbitrary",)),
        name="compress_sample",
    )(page_table.reshape(-1), *([x] * N_PAGES), perm, *cw)


def _token_minor(kv):
    n, t = kv.shape[0], kv.shape[1]
    return kv.transpose(0, 2, 3, 4, 1).reshape(n, KV_W, t)


TQ = 256
N_SEL = SEQ // SEL_BLOCK


def _half_mask(shape, hi):
    lane = lax.broadcasted_iota(jnp.int32, shape, 1)
    return (lane >= HEAD_DIM) if hi else (lane < HEAD_DIM)


def _cmp_branch(i, qs_ref, kd_ref, vt_ref, mt_ref):
    t0 = i * TQ
    ncmp = kd_ref.shape[0]
    nidx = lax.broadcasted_iota(jnp.int32, (ncmp, Q_PER_KV * TQ), 0)
    tpos = t0 + (lax.broadcasted_iota(jnp.int32, (ncmp, Q_PER_KV * TQ), 1) & (TQ - 1))
    mask = nidx * CMP_STRIDE + (CMP_BLOCK - 1) <= tpos
    s = jnp.where(mask, _nt(kd_ref[...], qs_ref[:, 0:LANES]), NEG)
    m = jnp.max(s, axis=0, keepdims=True)
    ex = jnp.where(mask, jnp.exp2(s - m), 0.0)
    p = ex * (1.0 / jnp.maximum(jnp.sum(ex, axis=0, keepdims=True), 1e-30))
    pb = p.astype(bf16)
    vt = vt_ref[...]
    heads = [jnp.dot(vt, pb[:, r * TQ:(r + 1) * TQ], preferred_element_type=f32) for r in range(Q_PER_KV)]
    heads = jnp.concatenate(heads, axis=0)
    psum = sum(p[:, r * TQ:(r + 1) * TQ] for r in range(Q_PER_KV))
    hi, lo = _split2(psum)
    imp = (jnp.dot(mt_ref[...], hi, preferred_element_type=f32)
           + jnp.dot(mt_ref[...], lo, preferred_element_type=f32))
    imp = imp[:N_SEL]
    j = lax.broadcasted_iota(jnp.int32, (N_SEL, TQ), 0)
    jt = (t0 + lax.broadcasted_iota(jnp.int32, (N_SEL, TQ), 1)) // SEL_BLOCK
    imp = jnp.where((j == 0) | (j == jt) | (j == jt - 1), SELECT_FORCE, imp)
    imp = jnp.where(j > jt, -SELECT_FORCE, imp)
    cnt = jnp.zeros((N_SEL, TQ), f32)
    for jp in range(N_SEL):
        row = imp[jp:jp + 1, :]
        cnt = cnt + jnp.where(j > jp, jnp.where(row >= imp, 1.0, 0.0), jnp.where(row > imp, 1.0, 0.0))
    selm = jnp.where((cnt < SEL_TOPN) & (j <= jt), 0.0, SEL_NEG)
    return heads, jnp.concatenate([selm, selm], axis=0).T


def _cmp_to_sel_matrix_t(n_cmp_pad, n_slc, n_cmp):
    ratio = SEL_BLOCK // CMP_STRIDE
    i = np.arange(n_cmp_pad)[None, :]
    jj = np.arange(n_slc)[:, None]
    diff = i - ratio * jj
    mat = np.zeros((n_slc, n_cmp_pad), np.float32)
    for n in range(CMP_BLOCK // CMP_STRIDE):
        mat += ((diff + n >= 0) & (diff + n < ratio)).astype(np.float32)
    mat[:, n_cmp:] = 0.0
    return mat


VT_ROWS = HEAD_DIM + 16
SEL_KT = 8

def _flash_tile(qs, k, vt, mask, m, acc):
    s = _nt(k, qs)
    if mask is not None:
        s = jnp.where(mask, s, NEG)
    m_next = jnp.maximum(m, jnp.max(s, axis=0, keepdims=True))
    alpha = jnp.exp2(m - m_next)
    p = jnp.exp2(s - m_next).astype(bf16)
    pv = jnp.concatenate([jnp.dot(vt, p[:, r * TQ:(r + 1) * TQ], preferred_element_type=f32)
                          for r in range(Q_PER_KV)], axis=1)
    return m_next, alpha * acc + pv


def _flash_finish(acc):
    o_t = acc[:HEAD_DIM] * (1.0 / acc[HEAD_DIM:HEAD_DIM + 1])
    return jnp.concatenate([o_t[:, r * TQ:(r + 1) * TQ] for r in range(Q_PER_KV)], axis=0)


LOG2E = math.log2(math.e)


def _stack_queries(q_ref, qs_ref):
    for r in range(Q_PER_KV):
        q2 = q_ref[:, (r // 2) * LANES:(r // 2 + 1) * LANES]
        qs_ref[r * TQ:(r + 1) * TQ, 0:LANES] = jnp.where(_half_mask((TQ, LANES), r % 2), q2, jnp.zeros_like(q2))


def _flash_init():
    return jnp.full((1, Q_PER_KV * TQ), NEG, f32), jnp.zeros((VT_ROWS, Q_PER_KV * TQ), f32)


def _sel_branch(i, selm, kd_ref, oh_ref, vt_ref, qs_ref, m_ref, acc_ref):
    selm = selm.astype(bf16)
    for r in range(Q_PER_KV):
        qs_ref[r * TQ:(r + 1) * TQ, LANES:2 * LANES] = selm
    qs = qs_ref[...]

    def span(first, n, mask, m, acc):
        rows = pl.ds(pl.multiple_of(first * TQ, TQ), n * TQ)
        k = jnp.concatenate([kd_ref[rows, :], oh_ref[rows, :]], axis=1)
        return _flash_tile(qs, k, _with_ones(vt_ref[:, rows]), mask, m, acc)

    full = i // SEL_KT
    m, acc = lax.fori_loop(0, full, lambda t, carry: span(SEL_KT * t, SEL_KT, None, *carry), _flash_init())
    m_ref[...] = m
    acc_ref[...] = acc
    for n in range(1, SEL_KT + 1):
        @pl.when(i % SEL_KT == n - 1)
        def _(n=n):
            key = lax.broadcasted_iota(jnp.int32, (n * TQ, Q_PER_KV * TQ), 0)
            qry = (lax.broadcasted_iota(jnp.int32, (n * TQ, Q_PER_KV * TQ), 1) & (TQ - 1)) + (n - 1) * TQ
            _, acc_n = span(full * SEL_KT, n, key <= qry, m_ref[...], acc_ref[...])
            acc_ref[...] = acc_n

    return _flash_finish(acc_ref[...])


def _win_branch(i, kd_ref, vt_ref, qs_ref, acc_ref):
    qs = qs_ref[:, 0:LANES]

    def span(first, n):
        rows = pl.ds(pl.multiple_of(first * TQ, TQ), n * TQ)
        key = lax.broadcasted_iota(jnp.int32, (n * TQ, Q_PER_KV * TQ), 0)
        qry = (lax.broadcasted_iota(jnp.int32, (n * TQ, Q_PER_KV * TQ), 1) & (TQ - 1)) + (n - 1) * TQ
        dist = qry - key
        _, acc = _flash_tile(qs, kd_ref[rows, :], _with_ones(vt_ref[:, rows]), (dist >= 0) & (dist < WINDOW),
                             *_flash_init())
        acc_ref[...] = acc

    n_back = WINDOW // TQ

    @pl.when(i >= n_back)
    def _():
        span(i - n_back, n_back + 1)

    for early in range(n_back):
        @pl.when(i == early)
        def _(early=early):
            span(0, early + 1)

    return _flash_finish(acc_ref[...])


def _with_ones(vt):
    return jnp.concatenate([vt.astype(bf16), jnp.ones((VT_ROWS - HEAD_DIM, vt.shape[1]), bf16)], axis=0)


def _prompt_attn_kernel(q_ref, gd_ref, kdc_ref, vtc_ref, mt_ref, kds_ref, oh_ref, vts_ref, kdw_ref, vtw_ref,
                        o_ref, qs_ref, m_ref, acc_ref, gt_ref):
    g, i = pl.program_id(1), pl.program_id(2)
    _stack_queries(q_ref, qs_ref)
    gt_ref[...] = _sigmoid(gd_ref[...]).T

    def gated(heads, branch):
        rows = [gt_ref[pl.ds((g * Q_PER_KV + r) * 3 + branch, 1), :] for r in range(Q_PER_KV)]
        return jnp.concatenate([heads[r * HEAD_DIM:(r + 1) * HEAD_DIM] * rows[r] for r in range(Q_PER_KV)], axis=0)

    heads, selm = _cmp_branch(i, qs_ref, kdc_ref, vtc_ref, mt_ref)
    total = gated(heads, 0)
    total = total + gated(_sel_branch(i, selm, kds_ref, oh_ref, vts_ref, qs_ref, m_ref, acc_ref), 1)
    total = total + gated(_win_branch(i, kdw_ref, vtw_ref, qs_ref, acc_ref), 2)
    o_ref[...] = total.T.astype(bf16)


def _prompt_attention(q, gd, kd_c, vt_c, kd_s, kv_s_t, kd_w, kv_w_t):
    assert WINDOW % TQ == 0
    ncmp = SEQ // CMP_STRIDE
    mt = np.zeros((LANES, ncmp), np.float32)
    mt[:N_SEL] = _cmp_to_sel_matrix_t(ncmp, N_SEL, ncmp - 1)
    mt = jnp.asarray(mt, bf16)
    blk = np.arange(SEQ)[:, None] // SEL_BLOCK == np.arange(LANES)[None, :]
    oh = jnp.asarray(blk.astype(np.float32), bf16)
    nq = SEQ // TQ
    q_spec = pl.BlockSpec((TQ, 2 * LANES), lambda b, g, i: (b * nq + i, g))
    cmp_spec = pl.BlockSpec((None, ncmp, LANES), lambda b, g, i: (b, 0, g))
    cmp_vt_spec = pl.BlockSpec((None, HEAD_DIM, ncmp), lambda b, g, i: (b, g, 0))
    kd_spec = pl.BlockSpec((SEQ, LANES), lambda b, g, i: (b, g))
    vt_spec = pl.BlockSpec((None, HEAD_DIM, SEQ), lambda b, g, i: (b, N_KV + g, 0))
    const = lambda a: pl.BlockSpec(a.shape, lambda b, g, i: (0, 0))
    gd_spec = pl.BlockSpec((TQ, GD_W), lambda b, g, i: (b * nq + i, 0))
    return pl.pallas_call(
        _prompt_attn_kernel,
        grid=(BATCH, N_KV, nq),
        in_specs=[q_spec, gd_spec, cmp_spec, cmp_vt_spec, const(mt), kd_spec, const(oh), vt_spec, kd_spec, vt_spec],
        out_specs=q_spec,
        out_shape=jax.ShapeDtypeStruct((BATCH * SEQ, ATTN_W), bf16),
        scratch_shapes=[pltpu.VMEM((Q_PER_KV * TQ, 2 * LANES), bf16), pltpu.VMEM((1, Q_PER_KV * TQ), f32),
                        pltpu.VMEM((VT_ROWS, Q_PER_KV * TQ), f32), pltpu.VMEM((GD_W, TQ), f32)],
        compiler_params=_cparams(("parallel", "parallel", "parallel")),
        name="prompt_attention",
    )(q, gd, kd_c, vt_c, mt, kd_s, oh, kv_s_t, kd_w, kv_w_t)


SSD_Q = 256
HALF_INNER = D_INNER // N_SSM_GROUPS
BC_W = N_SSM_GROUPS * D_STATE


def _mamba_consts(conv_w, conv_b, dt_bias, a_log, d_skip, ssm_norm):
    pad = lambda v: jnp.zeros((1, GD_W), f32).at[0, DT_LANE0:DT_LANE0 + N_SSM_HEADS].set(v)
    e16 = np.zeros((GD_W, D_INNER), np.float32)
    for h in range(N_SSM_HEADS):
        e16[DT_LANE0 + h, h * SSM_HEAD_DIM:(h + 1) * SSM_HEAD_DIM] = 1.0
    tri = np.tril(np.ones((SSD_Q, SSD_Q), np.float32))
    return (conv_w, conv_b[None], pad(dt_bias), pad(a_log), jnp.asarray(e16, bf16),
            jnp.repeat(d_skip, SSM_HEAD_DIM)[None], ssm_norm[None], jnp.asarray(tri, bf16))


def _dt_and_decay(gd, dtb_ref, alog_ref):
    lane = lax.broadcasted_iota(jnp.int32, gd.shape, 1)
    live = (lane >= DT_LANE0) & (lane < DT_LANE0 + N_SSM_HEADS)
    dt = jnp.where(live, _softplus(gd + dtb_ref[...]), 0.0)
    return dt, dt * (-jnp.exp(alog_ref[...]))


def _gated_group_norm(y, z, nw_ref):
    y = y * _silu(z)
    outs = []
    for g in range(N_SSM_GROUPS):
        yg = y[:, g * HALF_INNER:(g + 1) * HALF_INNER]
        outs.append(yg * lax.rsqrt(jnp.mean(yg * yg, axis=-1, keepdims=True) + NORM_EPS))
    return jnp.concatenate(outs, axis=1) * nw_ref[...]


def _mamba_prompt_kernel(z_ref, xbc_ref, gd_ref, cw_ref, cb_ref, dtb_ref, alog_ref, e16_ref, dsk_ref, nw_ref, tri_ref,
                         y_ref, st_ref, xpad_ref, state_ref):
    c = pl.program_id(1)
    nq = SSD_Q

    @pl.when(c == 0)
    def _():
        state_ref[...] = jnp.zeros(state_ref.shape, f32)
        xpad_ref[0:8, :] = jnp.zeros((8, CONV_DIM), f32)

    xpad_ref[8:8 + nq, :] = xbc_ref[...]
    conv = cb_ref[...]
    for w in range(CONV_W):
        conv = conv + xpad_ref[8 - (CONV_W - 1) + w:8 - (CONV_W - 1) + w + nq, :] * cw_ref[w:w + 1, :]
    xpad_ref[0:8, :] = xpad_ref[nq:nq + 8, :]
    act = _silu(conv)
    xs, bm, cm = act[:, :D_INNER], act[:, D_INNER:D_INNER + BC_W], act[:, D_INNER + BC_W:]

    dt, a = _dt_and_decay(gd_ref[...], dtb_ref, alog_ref)
    a_cs = sum(jnp.dot(tri_ref[...], t, preferred_element_type=f32) for t in _split3(a))
    ea = jnp.exp(a_cs)
    te = jnp.exp(a_cs[nq - 1:nq, :] - a_cs)
    e16 = e16_ref[...]
    dt_x, ea_x, te_x = _dot_exact(dt, e16), _dot_exact(ea, e16), _dot_exact(te, e16)
    xdt = xs * dt_x
    xw = (xdt * te_x).astype(bf16)
    a_cst = a_cs.T
    tril = lax.broadcasted_iota(jnp.int32, (nq, nq), 1) <= lax.broadcasted_iota(jnp.int32, (nq, nq), 0)

    ys = []
    for g in range(N_SSM_GROUPS):
        bg = bm[:, g * D_STATE:(g + 1) * D_STATE]
        cgb = cm[:, g * D_STATE:(g + 1) * D_STATE].astype(bf16)
        cb = _nt(cgb, bg.astype(bf16))
        st = state_ref[g]
        yoff = jnp.dot(cgb, st.astype(bf16), preferred_element_type=f32)
        gl = slice(g * HALF_INNER, (g + 1) * HALF_INNER)
        state_ref[g] = ea_x[nq - 1:nq, gl] * st + jnp.dot(bg.T.astype(bf16), xw[:, gl], preferred_element_type=f32)
        for hp in range(HALF_INNER // LANES):
            pl_ = slice(g * HALF_INNER + hp * LANES, g * HALF_INNER + (hp + 1) * LANES)
            xp = xdt[:, pl_]
            yp = ea_x[:, pl_] * yoff[:, hp * LANES:(hp + 1) * LANES] + dsk_ref[:, pl_] * xs[:, pl_]
            for e in range(2):
                lane_h = DT_LANE0 + g * (N_SSM_HEADS // N_SSM_GROUPS) + hp * 2 + e
                seg = a_cs[:, lane_h:lane_h + 1] - a_cst[lane_h:lane_h + 1, :]
                wgt = (cb * jnp.exp(jnp.where(tril, seg, NEG))).astype(bf16)
                xz = jnp.where(_half_mask((nq, LANES), e), xp, 0.0).astype(bf16)
                yp = yp + jnp.dot(wgt, xz, preferred_element_type=f32)
            ys.append(yp)
    y = jnp.concatenate(ys, axis=1)
    y_ref[...] = _gated_group_norm(y, z_ref[...], nw_ref).astype(bf16)

    @pl.when(c == pl.num_programs(1) - 1)
    def _():
        st_ref[...] = state_ref[...]


def _mamba_prompt(z, xbc, gd, mc):
    nc = SEQ // SSD_Q
    row = lambda w: pl.BlockSpec((SSD_Q, w), lambda b, c: (b * nc + c, 0))
    y, st = pl.pallas_call(
        _mamba_prompt_kernel,
        grid=(BATCH, nc),
        in_specs=[row(D_INNER), row(CONV_DIM), row(GD_W)] + [_const_spec(a, 2) for a in mc],
        out_specs=[row(D_INNER), pl.BlockSpec((None, N_SSM_GROUPS, D_STATE, HALF_INNER), lambda b, c: (b, 0, 0, 0))],
        out_shape=[jax.ShapeDtypeStruct((BATCH * SEQ, D_INNER), bf16),
                   jax.ShapeDtypeStruct((BATCH, N_SSM_GROUPS, D_STATE, HALF_INNER), f32)],
        scratch_shapes=[pltpu.VMEM((SSD_Q + 8, CONV_DIM), f32), pltpu.VMEM((N_SSM_GROUPS, D_STATE, HALF_INNER), f32)],
        compiler_params=_cparams(("parallel", "arbitrary")),
        name="mamba_prompt",
    )(z, xbc, gd, *mc)
    hpg = N_SSM_HEADS // N_SSM_GROUPS
    st = st.reshape(BATCH, N_SSM_GROUPS, D_STATE, hpg, SSM_HEAD_DIM).transpose(0, 1, 3, 4, 2)
    return y, st.reshape(BATCH, N_SSM_HEADS, SSM_HEAD_DIM, D_STATE)


def _mamba_sample_kernel(z_ref, xbc_ref, gd_ref, sc_ref, s_ref, cw_ref, cb_ref, dtb_ref, alog_ref, e16_ref, dsk_ref,
                         nw_ref, y_ref, so_ref):
    conv = cb_ref[...] + xbc_ref[...] * cw_ref[CONV_W - 1:CONV_W, :]
    for w in range(CONV_W - 1):
        conv = conv + sc_ref[w:w + 1, :] * cw_ref[w:w + 1, :]
    act = _silu(conv)
    xs, bm, cm = act[:, :D_INNER], act[:, D_INNER:D_INNER + BC_W], act[:, D_INNER + BC_W:]
    dt, a = _dt_and_decay(jnp.broadcast_to(gd_ref[...], (8, GD_W)), dtb_ref, alog_ref)
    e16 = e16_ref[...]
    dt_x = _dot_exact(dt, e16)[0:1]
    da_x = _dot_exact(jnp.exp(a), e16)[0:1]
    rows = N_SSM_HEADS * SSM_HEAD_DIM
    xcol = jnp.broadcast_to(xs * dt_x, (D_STATE, rows)).T
    acol = jnp.broadcast_to(da_x, (D_STATE, rows)).T
    rowi = lax.broadcasted_iota(jnp.int32, (rows, D_STATE), 0)
    bfull = jnp.where(rowi < HALF_INNER, bm[:, :D_STATE], bm[:, D_STATE:])
    snew = s_ref[...] * acol + xcol * bfull
    so_ref[...] = snew
    r8 = lax.broadcasted_iota(jnp.int32, (8, D_STATE), 0)
    c8 = jnp.where(r8 == 0, cm[:, :D_STATE], jnp.where(r8 == 1, cm[:, D_STATE:], 0.0)).astype(bf16)
    yy = _nt(c8, snew.astype(bf16))
    lane = lax.broadcasted_iota(jnp.int32, (1, rows), 1)
    y = jnp.where(lane < HALF_INNER, yy[0:1], yy[1:2]) + dsk_ref[...] * xs
    y_ref[...] = _gated_group_norm(y, z_ref[...], nw_ref).astype(bf16)


def _mamba_sample(z, xbc, gd, state_conv, state_ssm, mc):
    nb = z.shape[0]
    mc = mc[:7]
    rows = N_SSM_HEADS * SSM_HEAD_DIM
    one = lambda w: pl.BlockSpec((None, 1, w), lambda b: (b, 0, 0))
    s_spec = pl.BlockSpec((None, rows, D_STATE), lambda b: (b, 0, 0))
    y, so = pl.pallas_call(
        _mamba_sample_kernel,
        grid=(nb,),
        in_specs=[one(D_INNER), one(CONV_DIM), one(GD_W),
                  pl.BlockSpec((None, CONV_W - 1, CONV_DIM), lambda b: (b, 0, 0)), s_spec]
                 + [_const_spec(a, 1) for a in mc],
        out_specs=[one(D_INNER), s_spec],
        out_shape=[jax.ShapeDtypeStruct((nb, 1, D_INNER), bf16), jax.ShapeDtypeStruct((nb, rows, D_STATE), f32)],
        compiler_params=_cparams(("parallel",)),
        name="mamba_sample",
    )(z[:, None], xbc[:, None], gd[:, None], state_conv, state_ssm.reshape(nb, rows, D_STATE), *mc)
    return y[:, 0], so.reshape(nb, N_SSM_HEADS, SSM_HEAD_DIM, D_STATE)


N_CMP_S = PAST_LEN // CMP_STRIDE - 1
N_SLC_S = PAST_LEN // SEL_BLOCK + 1
N_SLC_PAD = 2 * LANES
HALF_PAGE = PAGE_SIZE // SEL_BLOCK
GONE = -3.0e38


def _cmp_attn_sample_kernel(qz_ref, kd_ref, vl_ref, vh_ref, mts_ref, o_ref, imp_ref):
    qz = (qz_ref[...] * ATTN_SCALE).astype(bf16)
    ncmp = kd_ref.shape[0]
    nidx = lax.broadcasted_iota(jnp.int32, (N_HEADS, ncmp), 1)
    rowh = lax.broadcasted_iota(jnp.int32, (N_HEADS, ncmp), 0)
    mask = (nidx < N_CMP_S) & (nidx * CMP_STRIDE + (CMP_BLOCK - 1) <= PAST_LEN)
    rowo = lax.broadcasted_iota(jnp.int32, (N_HEADS, LANES), 0)
    r8 = lax.broadcasted_iota(jnp.int32, (8, N_SLC_PAD), 0)
    o_acc = jnp.zeros((N_HEADS, LANES), f32)
    imp = jnp.zeros((8, N_SLC_PAD), f32)
    for g in range(N_KV):
        gl = slice(g * LANES, (g + 1) * LANES)
        s = jnp.where(mask, _nt(qz, kd_ref[:, gl]), NEG)
        m = jnp.max(s, axis=1, keepdims=True)
        ex = jnp.where(mask, jnp.exp(s - m), 0.0)
        p = ex * (1.0 / jnp.maximum(jnp.sum(ex, axis=1, keepdims=True), 1e-30))
        p = jnp.where(rowh // Q_PER_KV == g, p, 0.0)
        pb = p.astype(bf16)
        o_l = jnp.dot(pb, vl_ref[:, gl], preferred_element_type=f32)
        o_h = jnp.dot(pb, vh_ref[:, gl], preferred_element_type=f32)
        o_acc = o_acc + jnp.where(rowo % 2 == 0, o_l, o_h)
        hi, lo = _split2(p)
        imp_h = jnp.dot(hi, mts_ref[...], preferred_element_type=f32) + jnp.dot(lo, mts_ref[...], preferred_element_type=f32)
        imp = imp + jnp.where(r8 == g, jnp.sum(imp_h, axis=0, keepdims=True), 0.0)
    o_ref[...] = o_acc
    imp_ref[...] = imp


def _topk_sample_kernel(imp_ref, idx_ref):
    rows = imp_ref.shape[0]
    j = lax.broadcasted_iota(jnp.int32, (rows, N_SLC_PAD), 1)
    jt = PAST_LEN // SEL_BLOCK
    imp = imp_ref[...]
    imp = jnp.where((j == 0) | (j == jt) | (j == jt - 1), SELECT_FORCE, imp)
    imp = jnp.where(j > jt, -SELECT_FORCE, imp)
    imp = jnp.where(j >= N_SLC_S, NEG, imp)
    jf = j.astype(f32)
    lane = lax.broadcasted_iota(jnp.int32, (rows, LANES), 1)
    picked = jnp.zeros((rows, LANES), f32)
    for k in range(SEL_TOPN):
        m = jnp.max(imp, axis=1, keepdims=True)
        ix = jnp.min(jnp.where(imp == m, jf, float(N_SLC_PAD)), axis=1, keepdims=True)
        picked = jnp.where(lane == k, ix, picked)
        imp = jnp.where(jf == ix, GONE, imp)
    idx_ref[...] = picked.astype(jnp.int32)


def _cmp_attn_sample(qz, kd, vl, vh):
    nb, ncmp = kd.shape[0], kd.shape[1]
    mts = np.zeros((ncmp, N_SLC_PAD), np.float32)
    mts[:, :N_SLC_S] = _cmp_to_sel_matrix_t(ncmp, N_SLC_S, N_CMP_S).T
    mts = jnp.asarray(mts, bf16)
    kv_spec = pl.BlockSpec((None, ncmp, KV_W), lambda b: (b, 0, 0))
    o16, imp = pl.pallas_call(
        _cmp_attn_sample_kernel,
        grid=(nb,),
        in_specs=[pl.BlockSpec((None, N_HEADS, LANES), lambda b: (b, 0, 0)), kv_spec, kv_spec, kv_spec,
                  pl.BlockSpec(mts.shape, lambda b: (0, 0))],
        out_specs=[pl.BlockSpec((None, N_HEADS, LANES), lambda b: (b, 0, 0)),
                   pl.BlockSpec((None, 8, N_SLC_PAD), lambda b: (b, 0, 0))],
        out_shape=[jax.ShapeDtypeStruct((nb, N_HEADS, LANES), f32), jax.ShapeDtypeStruct((nb, 8, N_SLC_PAD), f32)],
        compiler_params=_cparams(("parallel",)),
        name="cmp_attn_sample",
    )(qz, kd, vl, vh, mts)
    idx = pl.pallas_call(
        _topk_sample_kernel,
        out_shape=jax.ShapeDtypeStruct((nb * 8, LANES), jnp.int32),
        name="topk_sample",
    )(imp.reshape(nb * 8, N_SLC_PAD))
    return o16, idx.reshape(nb, 8, LANES)


def _one_query_softmax(s, s_new, vt, v_new):
    m = jnp.maximum(jnp.max(s, axis=1, keepdims=True), s_new)
    p = jnp.exp(s - m)
    p_new = jnp.exp(s_new - m)
    den = jnp.sum(p, axis=1, keepdims=True) + p_new
    num = _nt(p.astype(bf16), vt) + p_new * v_new
    return num * (1.0 / den)


def _selwin_sample_kernel(idx_ref, pg_ref, *refs):
    del pg_ref
    kv = refs[:SEL_TOPN]
    q_ref, ksn_ref, win_ref, kwn_ref, os_ref, ow_ref = refs[SEL_TOPN:]
    b, g = pl.program_id(0), pl.program_id(1)
    q8 = q_ref[...] * ATTN_SCALE
    qb = q8.astype(bf16)
    kcat = jnp.concatenate([r[0] for r in kv], axis=1).astype(bf16)
    vcat = jnp.concatenate([r[1] for r in kv], axis=1).astype(bf16)
    n = SEL_TOPN * PAGE_SIZE
    lane = lax.broadcasted_iota(jnp.int32, (8, n), 1)
    slot = lane // PAGE_SIZE
    blk = jnp.zeros((8, n), jnp.int32)
    for k in range(SEL_TOPN):
        blk = jnp.where(slot == k, idx_ref[(b * N_KV + g) * SEL_TOPN + k], blk)
    live = (blk < N_SLC_S - 1) & ((lane % PAGE_SIZE) // SEL_BLOCK == blk % HALF_PAGE)
    s = jnp.where(live, jnp.dot(qb, kcat, preferred_element_type=f32), NEG)
    k_new, v_new = ksn_ref[pl.ds(g, 1), :], ksn_ref[pl.ds(N_KV + g, 1), :]
    s_new = jnp.sum(q8 * k_new, axis=1, keepdims=True)
    os_ref[...] = _one_query_softmax(s, s_new, vcat, v_new)
    nbuf = win_ref.shape[-1]
    i = lax.broadcasted_iota(jnp.int32, (8, nbuf), 1)
    s = jnp.where(nbuf - i < WINDOW, jnp.dot(qb, win_ref[0].astype(bf16), preferred_element_type=f32), NEG)
    k_new, v_new = kwn_ref[pl.ds(g, 1), :], kwn_ref[pl.ds(N_KV + g, 1), :]
    s_new = jnp.sum(q8 * k_new, axis=1, keepdims=True)
    ow_ref[...] = _one_query_softmax(s, s_new, win_ref[1].astype(bf16), v_new)


def _selwin_sample(idx, page_table, cache_s, q_s, ks_new, win_buf, kw_new):
    nb = q_s.shape[0]
    nbuf = win_buf.shape[1]
    pages = cache_s.transpose(0, 2, 3, 4, 1)
    wb = win_buf.transpose(0, 2, 3, 4, 1)
    qg = q_s.reshape(nb, N_KV, Q_PER_KV, HEAD_DIM)
    q8 = jnp.concatenate([qg, jnp.zeros_like(qg)], axis=2)

    logical = jnp.minimum(idx, N_SLC_S - 2) // HALF_PAGE
    page_of = jnp.take_along_axis(page_table, logical.reshape(nb, N_KV * SEL_TOPN), axis=1)

    def blk_map(b, g, idx_ref, pg_ref, k):
        return (pg_ref[(b * N_KV + g) * SEL_TOPN + k], 0, g, 0, 0)

    blk_specs = [pl.BlockSpec((None, 2, None, HEAD_DIM, PAGE_SIZE), functools.partial(blk_map, k=k))
                 for k in range(SEL_TOPN)]
    new_spec = pl.BlockSpec((None, 2 * N_KV, HEAD_DIM), lambda b, g, *_: (b, 0, 0))
    win_spec = pl.BlockSpec((None, 2, None, HEAD_DIM, nbuf), lambda b, g, *_: (b, 0, g, 0, 0))
    o_spec = pl.BlockSpec((None, None, 8, HEAD_DIM), lambda b, g, *_: (b, g, 0, 0))
    grid_spec = pltpu.PrefetchScalarGridSpec(
        num_scalar_prefetch=2,
        grid=(nb, N_KV),
        in_specs=blk_specs + [o_spec, new_spec, win_spec, new_spec],
        out_specs=[o_spec, o_spec])
    o_shape = jax.ShapeDtypeStruct((nb, N_KV, 8, HEAD_DIM), f32)
    ks3, kw3 = ks_new.reshape(nb, 2 * N_KV, HEAD_DIM), kw_new.reshape(nb, 2 * N_KV, HEAD_DIM)
    os8, ow8 = pl.pallas_call(
        _selwin_sample_kernel,
        grid_spec=grid_spec,
        out_shape=[o_shape, o_shape],
        compiler_params=_cparams(("arbitrary", "arbitrary")),
        name="selwin_sample",
    )(idx.reshape(-1), page_of.reshape(-1), *([pages] * SEL_TOPN), q8, ks3, wb, kw3)
    return os8[:, :, :Q_PER_KV].reshape(nb, ATTN_W), ow8[:, :, :Q_PER_KV].reshape(nb, ATTN_W)


def _sample_query_layouts(q):
    nb = q.shape[0]
    q16 = q.reshape(nb, N_HEADS, HEAD_DIM)
    z = jnp.zeros_like(q16)
    lo, hi = jnp.concatenate([q16, z], axis=-1), jnp.concatenate([z, q16], axis=-1)
    h = jnp.arange(N_HEADS)[None, :, None]
    return jnp.where(h % 2 == 0, lo, hi)


def _unpad_heads_by_parity(o16):
    nb = o16.shape[0]
    o = o16.reshape(nb, N_HEADS // 2, 2, 2, HEAD_DIM)
    return jnp.stack([o[:, :, 0, 0], o[:, :, 1, 1]], axis=2).reshape(nb, ATTN_W)


def _finish(x, ocmp, osel, owin, gd, ssd_y, w_out, ln_mlp, w_up, w_down, ln_final, tm, tm_mlp, tf):
    x1 = _outproj(x, ocmp, osel, owin, gd, ssd_y, _gate_expand(), w_out, tm)
    return _mlp(x1, ln_mlp, w_up, w_down, ln_final, tm_mlp, tf)


def kernel(x_prompt, x_sample, cache_kv_cmp, cache_kv_sel, state_kv_win, state_conv, state_ssm, page_table, ln_mix, w_in, cmp_pe, cmp_w1, cmp_b1, cmp_w2, cmp_b2, conv_w, conv_b, dt_bias, a_log, d_skip, ssm_norm, w_out, ln_mlp, w_up, w_down, ln_final):
    nb = x_sample.shape[0]
    w_in_p = _prep_w_in(w_in[0])
    w_out_b, w_up_b, w_down_b = w_out[0].astype(bf16), w_up[0].astype(bf16), w_down[0].astype(bf16)
    lnw, lnm, lnf = ln_mix[0][None], ln_mlp[0][None], ln_final[None]
    cw = _compress_weights(cmp_pe[0], cmp_w1[0], cmp_b1[0], cmp_w2[0], cmp_b2[0])
    mc = _mamba_consts(conv_w[0], conv_b[0], dt_bias[0], a_log[0], d_skip[0], ssm_norm[0])
    kv_shape = (2, N_KV, HEAD_DIM)

    xp = x_prompt.reshape(BATCH * SEQ, D_MODEL)
    q, z, xbc, gd, kct, kst, kwt, kds, kdw = _inproj_prompt(xp, lnw, w_in_p, 256)
    kd, _, _, vt = _compress_prompt(kct, cw)
    attn = _prompt_attention(q, gd, kd, vt, kds, kst, kdw, kwt)
    ssd_y, ssm_p = _mamba_prompt(z, xbc, gd, mc)
    y_prompt = _mlp(_outproj_mixed(xp, attn, ssd_y, w_out_b, 512), lnm, w_up_b, w_down_b, lnf, 512, 1024)

    xs = x_sample.reshape(nb, D_MODEL)
    q_s, z_s, xbc_s, gd_s, kc_s, ks_s, kw_s = _inproj(xs, lnw, w_in_p, nb)
    kd_s, vl_s, vh_s = _compress_sample(cache_kv_cmp[0], page_table, cw)
    ocmp16, idx8 = _cmp_attn_sample(_sample_query_layouts(q_s), kd_s, vl_s, vh_s)
    idx = idx8[:, :N_KV, :SEL_TOPN]
    osel_s, owin_s = _selwin_sample(idx, page_table, cache_kv_sel[0], q_s, ks_s, state_kv_win[0], kw_s)
    ssd_y_s, ssm_s = _mamba_sample(z_s, xbc_s, gd_s, state_conv[0], state_ssm[0], mc)
    y_sample = _finish(xs, _unpad_heads_by_parity(ocmp16), osel_s, owin_s, gd_s, ssd_y_s,
                       w_out_b, lnm, w_up_b, w_down_b, lnf, nb, nb, 1024)

    def token_major(t):
        return t.reshape(BATCH, *kv_shape, t.shape[-1]).transpose(0, 4, 1, 2, 3)[None]

    conv_p = xbc.reshape(BATCH, SEQ, CONV_DIM)[:, -(CONV_W - 1):]
    kv_win_s = jnp.concatenate([state_kv_win[0], kw_s.reshape(nb, 1, *kv_shape)], axis=1)[:, -WINDOW:]
    conv_s = jnp.concatenate([state_conv[0], xbc_s[:, None]], axis=1)[:, -(CONV_W - 1):]
    return (y_prompt.reshape(BATCH, SEQ, D_MODEL), y_sample.reshape(nb, 1, D_MODEL),
            token_major(kct), token_major(kst), token_major(kwt[:, :, -min(WINDOW, SEQ):]),
            conv_p[None], ssm_p[None],
            kc_s.reshape(1, nb, 1, *kv_shape), ks_s.reshape(1, nb, 1, *kv_shape), kv_win_s[None],
            conv_s[None], ssm_s[None])
```
